```python
import jax, jax.numpy as jnp
from jax import lax
import numpy as np

D_MODEL = 2048
BATCH = 2
SEQ = 4096
DEPTH = 2
DEC_BATCH = 128
DEC_SEQ = 1
PAST_LEN = 2048
PAGE_SIZE = 128

HEAD_DIM = 128
N_A_LAYERS = DEPTH // 2
MEM_TOKENS = 256
MEM_HEADS = 4
MEM_WIDTH = MEM_HEADS * HEAD_DIM
TOK_WIDTH = D_MODEL - MEM_WIDTH
A_HEADS = TOK_WIDTH // HEAD_DIM
B_HEADS = TOK_WIDTH // HEAD_DIM
B_KV_GROUPS = 4
B_GROUP = B_HEADS // B_KV_GROUPS
D_FF = ((8 * D_MODEL // 3 + 255) // 256) * 256
MLSTM_CHUNK = 128
CMP_STRIDE = 16
CMP_LEN = 2 * CMP_STRIDE
CMP_HIDDEN = HEAD_DIM
SLC_BLOCK = 64
N_SELECT = 16
WINDOW = 512
WIN_QBLK = 128
SEL_QBLK = 64
FORCED_SCORE = 1e4
NEG = -1e30
EPS = 1e-6
SCALE = HEAD_DIM ** -0.5
A_IN = 4 * TOK_WIDTH + 2 * A_HEADS + MEM_WIDTH
B_IN = TOK_WIDTH + 3 * B_HEADS + MEM_WIDTH
KV_WIDTH = 6 * B_KV_GROUPS * HEAD_DIM

kernel_name = "yoco_mlstm_nsa_macaron_memory_step"


def rmsnorm(x, g):
    xf = x.astype(jnp.float32)
    y = xf * lax.rsqrt(jnp.mean(xf * xf, axis=-1, keepdims=True) + EPS)
    return y.astype(x.dtype) * g


def swiglu(h, w_in, w_out):
    gate, up = jnp.split(h @ w_in, 2, axis=-1)
    return (jax.nn.silu(gate) * up) @ w_out


def masked_probs(s, mask):
    p = jax.nn.softmax(jnp.where(mask, s, NEG), axis=-1)
    return jnp.where(mask, p, 0.0)


def mem_attend(q, k, v):
    s = jnp.einsum('blhd,bmhd->bhlm', q, k).astype(jnp.float32) * SCALE
    p = jax.nn.softmax(s, axis=-1).astype(v.dtype)
    return jnp.einsum('bhlm,bmhd->blhd', p, v)


def head_rmsnorm(h, g):
    h = h * lax.rsqrt(jnp.mean(h * h, axis=-1, keepdims=True) + EPS)
    return h * g.reshape(h.shape[-2:]).astype(jnp.float32)


def mlstm_chunkwise(q, k, v, ig, lf, C0, n0, m0):
    B, L, H, D = q.shape
    f32 = jnp.float32
    c = min(MLSTM_CHUNK, L)
    nc = -(-L // c)
    pad = nc * c - L

    def blocks(a, fill):
        a = jnp.pad(a.astype(f32), [(0, 0), (0, pad)] + [(0, 0)] * (a.ndim - 2), constant_values=fill)
        a = a.reshape((B, nc, c) + a.shape[2:])
        return jnp.swapaxes(jnp.moveaxis(a, 1, 0), 2, 3)

    xs = (blocks(q, 0.0), blocks(k, 0.0), blocks(v, 0.0), blocks(ig, NEG), blocks(lf, 0.0))
    tril = jnp.tril(jnp.ones((c, c), dtype=bool))

    def step(carry, xs_c):
        C, n, m = carry
        qc, kc, vc, ic, fc = xs_c
        b = jnp.cumsum(fc, axis=-1)
        dlog = jnp.where(tril, b[..., :, None] - b[..., None, :] + ic[..., None, :], -jnp.inf)
        inter = b + m[..., None]
        mt = jnp.maximum(inter, jnp.max(dlog, axis=-1))
        a = jnp.exp(inter - mt)
        w = jnp.exp(dlog - mt[..., None]) * jnp.einsum('bhtd,bhsd->bhts', qc, kc)
        num = a[..., None] * jnp.einsum('bhtd,bhde->bhte', qc, C) + jnp.einsum('bhts,bhse->bhte', w, vc)
        den = a * jnp.einsum('bhtd,bhd->bht', qc, n) + jnp.sum(w, axis=-1)
        h = num / jnp.maximum(jnp.abs(den), jnp.exp(-mt))[..., None]
        bl = b[..., -1]
        g = bl[..., None] - b + ic
        m_new = jnp.maximum(bl + m, jnp.max(g, axis=-1))
        decay = jnp.exp(bl + m - m_new)
        wk = jnp.exp(g - m_new[..., None])
        C_new = decay[..., None, None] * C + jnp.einsum('bhs,bhsd,bhse->bhde', wk, kc, vc)
        n_new = decay[..., None] * n + jnp.einsum('bhs,bhsd->bhd', wk, kc)
        return (C_new, n_new, m_new), h

    (Cf, nf, mf), hs = lax.scan(step, (C0.astype(f32), n0.astype(f32), m0.astype(f32)), xs)
    hs = jnp.moveaxis(jnp.swapaxes(hs, 2, 3), 0, 1).reshape(B, nc * c, H, D)[:, :L]
    return hs, (Cf, nf, mf)


def compress(x, pe, w1, w2):
    B, T, G, D = x.shape
    nch = T // CMP_STRIDE
    xs = x[:, :nch * CMP_STRIDE].reshape(B, nch, CMP_STRIDE, G, D)
    pe2 = pe.reshape(2, CMP_STRIDE, D)
    w12 = w1.reshape(2, CMP_STRIDE, D, -1)
    first = jnp.einsum('bncgd,cde->bnge', xs + pe2[0][:, None, :], w12[0])
    second = jnp.einsum('bncgd,cde->bnge', xs + pe2[1][:, None, :], w12[1])
    hid = jax.nn.gelu(first[:, :-1] + second[:, 1:])
    return hid @ w2


def nsa_cmp_select(q, pos, kc, vc, n_slc):
    s = jnp.einsum('blgrd,bngd->bgrln', q, kc).astype(jnp.float32) * SCALE
    n_cmp = kc.shape[1]
    start = jnp.arange(n_cmp, dtype=jnp.int32) * CMP_STRIDE
    mask = (start + CMP_LEN - 1)[None, :] <= pos[:, None]
    p = masked_probs(s, mask)
    o_c = jnp.einsum('bgrln,bngd->blgrd', p.astype(vc.dtype), vc)
    j = jnp.arange(n_slc, dtype=jnp.int32)
    overlap = ((start[:, None] < (j[None, :] + 1) * SLC_BLOCK)
               & (start[:, None] + CMP_LEN > j[None, :] * SLC_BLOCK)).astype(jnp.float32)
    imp = jnp.einsum('bgrln,nj->bglj', p, overlap)
    cur = (pos // SLC_BLOCK)[:, None]
    forced = (j[None, :] == 0) | (j[None, :] == cur) | (j[None, :] == cur - 1)
    valid = j[None, :] * SLC_BLOCK <= pos[:, None]
    score = jnp.where(forced, FORCED_SCORE, jnp.where(valid, imp, -1.0))
    _, idx = lax.top_k(score, min(N_SELECT, n_slc))
    return o_c, jnp.transpose(idx, (0, 2, 1, 3))


def sel_attend(q, pos, idx, gather):
    kb, vb = gather(idx)
    B, L, G, K, C, D = kb.shape
    R = q.shape[3]
    s = jnp.einsum('blgrd,blgkcd->blgrkc', q, kb).astype(jnp.float32) * SCALE
    kpos = idx[..., None] * SLC_BLOCK + jnp.arange(SLC_BLOCK, dtype=jnp.int32)
    mask = (kpos <= pos[None, :, None, None, None]).reshape(B, L, G, 1, K * C)
    p = masked_probs(s.reshape(B, L, G, R, K * C), mask).astype(vb.dtype)
    return jnp.einsum('blgrn,blgnd->blgrd', p, vb.reshape(B, L, G, K * C, D))


def sel_attend_blocked(q, pos, idx, gather):
    B, L = q.shape[:2]
    qb = SEL_QBLK if L % SEL_QBLK == 0 else L
    nb = L // qb
    if nb == 1:
        return sel_attend(q, pos, idx, gather)
    qs = jnp.swapaxes(q.reshape((B, nb, qb) + q.shape[2:]), 0, 1)
    ps = pos.reshape(nb, qb)
    ids = jnp.swapaxes(idx.reshape((B, nb, qb) + idx.shape[2:]), 0, 1)
    out = lax.map(lambda a: sel_attend(a[0], a[1], a[2], gather), (qs, ps, ids))
    return jnp.swapaxes(out, 0, 1).reshape((B, L) + out.shape[3:])


def window_prompt(q, kw, vw):
    B, S, G, R, D = q.shape
    nb = S // WIN_QBLK
    span = WINDOW + WIN_QBLK
    padw = [(0, 0), (WINDOW, 0), (0, 0), (0, 0)]
    kp, vp = jnp.pad(kw, padw), jnp.pad(vw, padw)
    idx = jnp.arange(nb, dtype=jnp.int32)[:, None] * WIN_QBLK + jnp.arange(span, dtype=jnp.int32)[None, :]
    kb, vb = kp[:, idx], vp[:, idx]
    qb = q.reshape(B, nb, WIN_QBLK, G, R, D)
    s = jnp.einsum('bnqgrd,bnkgd->bngrqk', qb, kb).astype(jnp.float32) * SCALE
    kpos = idx - WINDOW
    qpos = jnp.arange(nb, dtype=jnp.int32)[:, None] * WIN_QBLK + jnp.arange(WIN_QBLK, dtype=jnp.int32)[None, :]
    mask = ((kpos[:, None, :] <= qpos[:, :, None]) & (kpos[:, None, :] > qpos[:, :, None] - WINDOW)
            & (kpos[:, None, :] >= 0))
    p = masked_probs(s, mask[None, :, None, None]).astype(vb.dtype)
    return jnp.einsum('bngrqk,bnkgd->bnqgrd', p, vb).reshape(B, S, G, R, D)


def window_sample(q, pos, kw_new, vw_new, kw_buf, vw_buf):
    w_buf = kw_buf.shape[1]
    kk = jnp.concatenate([kw_buf, kw_new], axis=1)
    vv = jnp.concatenate([vw_buf, vw_new], axis=1)
    kpos = PAST_LEN - w_buf + jnp.arange(kk.shape[1], dtype=jnp.int32)
    s = jnp.einsum('blgrd,bkgd->bgrlk', q, kk).astype(jnp.float32) * SCALE
    mask = (kpos[None, :] <= pos[:, None]) & (kpos[None, :] > pos[:, None] - WINDOW)
    p = masked_probs(s, mask).astype(vv.dtype)
    return jnp.einsum('bgrlk,bkgd->blgrd', p, vv)


def combine(gates, o_c, o_s, o_w):
    out = gates[..., 0:1] * o_c + gates[..., 1:2] * o_s + gates[..., 2:3] * o_w
    return out.reshape(out.shape[:2] + (-1,))


def nsa_prompt(q, gates, kv, pe_k, w1_k, w2_k, pe_v, w1_v, w2_v):
    kc_raw, vc_raw, ks, vs, kw, vw = kv
    B, L, G = ks.shape[:3]
    pos = jnp.arange(L, dtype=jnp.int32)
    kc = compress(kc_raw, pe_k, w1_k, w2_k)
    vc = compress(vc_raw, pe_v, w1_v, w2_v)
    n_slc = -(-L // SLC_BLOCK)
    o_c, idx = nsa_cmp_select(q, pos, kc, vc, n_slc)
    ksb = ks.reshape(B, n_slc, SLC_BLOCK, G, HEAD_DIM)
    vsb = vs.reshape(B, n_slc, SLC_BLOCK, G, HEAD_DIM)
    b_i = jnp.arange(B)[:, None, None, None]
    g_i = jnp.arange(G)[None, None, :, None]
    gather = lambda ix: (ksb[b_i, ix, :, g_i, :], vsb[b_i, ix, :, g_i, :])
    o_s = sel_attend_blocked(q, pos, idx, gather)
    o_w = window_prompt(q, kw, vw)
    return combine(gates, o_c, o_s, o_w)


def nsa_sample(q, gates, kv, cache_cmp_k, cache_cmp_v, cache_slc_k, cache_slc_v, cache_win_k, cache_win_v,
               page_table, pe_k, w1_k, w2_k, pe_v, w1_v, w2_v):
    kc_new, vc_new, ks_new, vs_new, kw_new, vw_new = kv
    B, L, G = ks_new.shape[:3]
    n_pages = PAST_LEN // PAGE_SIZE
    pos = PAST_LEN + jnp.arange(L, dtype=jnp.int32)

    def paged_rows(pool):
        return pool[page_table].reshape(B, n_pages * PAGE_SIZE, G, HEAD_DIM)

    kc = compress(jnp.concatenate([paged_rows(cache_cmp_k), kc_new], axis=1), pe_k, w1_k, w2_k)
    vc = compress(jnp.concatenate([paged_rows(cache_cmp_v), vc_new], axis=1), pe_v, w1_v, w2_v)
    n_slc = -(-(PAST_LEN + L) // SLC_BLOCK)
    o_c, idx = nsa_cmp_select(q, pos, kc, vc, n_slc)
    n_pb = PAST_LEN // SLC_BLOCK
    n_tail = n_slc - n_pb
    bpp = PAGE_SIZE // SLC_BLOCK

    def tail_blocks(x):
        x = jnp.pad(x, ((0, 0), (0, n_tail * SLC_BLOCK - L), (0, 0), (0, 0)))
        return x.reshape(B, n_tail, SLC_BLOCK, G, HEAD_DIM)

    ks_tail, vs_tail = tail_blocks(ks_new), tail_blocks(vs_new)
    ks_pool = cache_slc_k.reshape(-1, bpp, SLC_BLOCK, G, HEAD_DIM)
    vs_pool = cache_slc_v.reshape(-1, bpp, SLC_BLOCK, G, HEAD_DIM)
    b_i = jnp.arange(B)[:, None, None, None]
    g_i = jnp.arange(G)[None, None, :, None]

    def gather(ix):
        in_past = (ix < n_pb)[..., None, None]
        phys = page_table[b_i, jnp.clip(ix // bpp, 0, n_pages - 1)]
        sub = ix % bpp
        jt = jnp.clip(ix - n_pb, 0, n_tail - 1)
        pick = lambda pool, tail: jnp.where(in_past, pool[phys, sub, :, g_i, :], tail[b_i, jt, :, g_i, :])
        return pick(ks_pool, ks_tail), pick(vs_pool, vs_tail)

    o_s = sel_attend_blocked(q, pos, idx, gather)
    o_w = window_sample(q, pos, kw_new, vw_new, cache_win_k, cache_win_v)
    return combine(gates, o_c, o_s, o_w)


def setup_inputs(seed: int = 0) -> dict:
    key = jax.random.key(seed)
    ks = iter(jax.random.split(key, 64))
    f32 = jnp.float32

    def nrm(shape, scale=1.0):
        return jax.random.normal(next(ks), shape, f32) * scale

    def gain(shape):
        return 1.0 + 0.02 * jax.random.normal(next(ks), shape, f32)

    G, DH = B_KV_GROUPS, HEAD_DIM
    n_pages = PAST_LEN // PAGE_SIZE
    n_used = DEC_BATCH * n_pages
    n_phys = n_used + max(1, n_used // 4)
    w_buf = min(WINDOW, PAST_LEN)
    n_b = DEPTH - N_A_LAYERS
    page_table = jax.random.permutation(next(ks), n_phys)[:n_used].reshape(DEC_BATCH, n_pages).astype(jnp.int32)
    return {
        "x_prompt": nrm((BATCH, SEQ, D_MODEL)),
        "x_sample": nrm((DEC_BATCH, DEC_SEQ, D_MODEL)),
        "mem_prompt": nrm((BATCH, MEM_TOKENS, D_MODEL)),
        "cache_mem_k": nrm((DEPTH, DEC_BATCH, MEM_TOKENS, MEM_HEADS, DH)),
        "cache_mem_v": nrm((DEPTH, DEC_BATCH, MEM_TOKENS, MEM_HEADS, DH)),
        "state_mlstm_C": nrm((N_A_LAYERS, DEC_BATCH, A_HEADS, DH, DH)),
        "state_mlstm_n": nrm((N_A_LAYERS, DEC_BATCH, A_HEADS, DH)),
        "state_mlstm_m": nrm((N_A_LAYERS, DEC_BATCH, A_HEADS), 0.5),
        "cache_cmp_k": nrm((n_phys, PAGE_SIZE, G, DH)),
        "cache_cmp_v": nrm((n_phys, PAGE_SIZE, G, DH)),
        "cache_slc_k": nrm((n_phys, PAGE_SIZE, G, DH)),
        "cache_slc_v": nrm((n_phys, PAGE_SIZE, G, DH)),
        "cache_win_k": nrm((DEC_BATCH, w_buf, G, DH)),
        "cache_win_v": nrm((DEC_BATCH, w_buf, G, DH)),
        "page_table": page_table,
        "ffn1_norm": gain((DEPTH, D_MODEL)),
        "ffn1_w_in": nrm((DEPTH, D_MODEL, 2 * D_FF), D_MODEL ** -0.5),
        "ffn1_w_out": nrm((DEPTH, D_FF, D_MODEL), D_FF ** -0.5),
        "ffn2_norm": gain((DEPTH, D_MODEL)),
        "ffn2_w_in": nrm((DEPTH, D_MODEL, 2 * D_FF), D_MODEL ** -0.5),
        "ffn2_w_out": nrm((DEPTH, D_FF, D_MODEL), D_FF ** -0.5),
        "mix_norm": gain((DEPTH, D_MODEL)),
        "a_w_in": nrm((N_A_LAYERS, D_MODEL, A_IN), D_MODEL ** -0.5),
        "a_b_i": nrm((N_A_LAYERS, A_HEADS), 0.1),
        "a_b_f": 3.0 + 3.0 * jax.random.uniform(next(ks), (N_A_LAYERS, A_HEADS), f32),
        "a_head_norm": gain((N_A_LAYERS, TOK_WIDTH)),
        "b_w_in": nrm((n_b, D_MODEL, B_IN), D_MODEL ** -0.5),
        "mem_norm": gain((DEPTH, D_MODEL)),
        "w_mem_kv": nrm((DEPTH, D_MODEL, 2 * MEM_WIDTH), D_MODEL ** -0.5),
        "w_out": nrm((DEPTH, D_MODEL, D_MODEL), D_MODEL ** -0.5),
        "kv_norm": gain((D_MODEL,)),
        "w_kv": nrm((D_MODEL, KV_WIDTH), D_MODEL ** -0.5),
        "cmp_pe_k": nrm((CMP_LEN, DH), 0.1),
        "cmp_w1_k": nrm((CMP_LEN * DH, CMP_HIDDEN), (CMP_LEN * DH) ** -0.5),
        "cmp_w2_k": nrm((CMP_HIDDEN, DH), CMP_HIDDEN ** -0.5),
        "cmp_pe_v": nrm((CMP_LEN, DH), 0.1),
        "cmp_w1_v": nrm((CMP_LEN * DH, CMP_HIDDEN), (CMP_LEN * DH) ** -0.5),
        "cmp_w2_v": nrm((CMP_HIDDEN, DH), CMP_HIDDEN ** -0.5),
        "final_norm": gain((D_MODEL,)),
    }


def reference(x_prompt, x_sample, mem_prompt, cache_mem_k, cache_mem_v, state_mlstm_C, state_mlstm_n,
              state_mlstm_m, cache_cmp_k, cache_cmp_v, cache_slc_k, cache_slc_v, cache_win_k, cache_win_v,
              page_table, ffn1_norm, ffn1_w_in, ffn1_w_out, ffn2_norm, ffn2_w_in, ffn2_w_out, mix_norm,
              a_w_in, a_b_i, a_b_f, a_head_norm, b_w_in, mem_norm, w_mem_kv, w_out, kv_norm, w_kv,
              cmp_pe_k, cmp_w1_k, cmp_w2_k, cmp_pe_v, cmp_w1_v, cmp_w2_v, final_norm):
    f32 = jnp.float32
    G, R, DH = B_KV_GROUPS, B_GROUP, HEAD_DIM
    split_a = [TOK_WIDTH, 2 * TOK_WIDTH, 3 * TOK_WIDTH, 4 * TOK_WIDTH, 4 * TOK_WIDTH + A_HEADS,
               4 * TOK_WIDTH + 2 * A_HEADS]
    split_b = [TOK_WIDTH, TOK_WIDTH + 3 * B_HEADS]

    def trunk(x, mem_k, mem_v, C0, n0, m0, nsa_mixer):
        Bn, L, _ = x.shape
        Cs, ns, ms = [], [], []
        kv = None
        for l in range(DEPTH):
            x = x + 0.5 * swiglu(rmsnorm(x, ffn1_norm[l]), ffn1_w_in[l], ffn1_w_out[l])
            h = rmsnorm(x, mix_norm[l])
            if l < N_A_LAYERS:
                q, k, v, o, ig, fg, qm = jnp.split(h @ a_w_in[l], split_a, axis=-1)
                hh, (C, n, m) = mlstm_chunkwise(
                    q.reshape(Bn, L, A_HEADS, DH), k.reshape(Bn, L, A_HEADS, DH) * SCALE,
                    v.reshape(Bn, L, A_HEADS, DH), ig.astype(f32) + a_b_i[l],
                    jax.nn.log_sigmoid(fg.astype(f32) + a_b_f[l]), C0[l], n0[l], m0[l])
                gate_o = jax.nn.sigmoid(o.astype(f32)).reshape(Bn, L, A_HEADS, DH)
                mix = (gate_o * head_rmsnorm(hh, a_head_norm[l])).astype(x.dtype).reshape(Bn, L, TOK_WIDTH)
                Cs.append(C)
                ns.append(n)
                ms.append(m)
            else:
                q, gts, qm = jnp.split(h @ b_w_in[l - N_A_LAYERS], split_b, axis=-1)
                gates = jax.nn.sigmoid(gts.astype(f32)).astype(x.dtype).reshape(Bn, L, G, R, 3)
                mix = nsa_mixer(q.reshape(Bn, L, G, R, DH), gates, kv)
            mo = mem_attend(qm.reshape(Bn, L, MEM_HEADS, DH), mem_k[l], mem_v[l]).reshape(Bn, L, MEM_WIDTH)
            x = x + jnp.concatenate([mix, mo], axis=-1) @ w_out[l]
            x = x + 0.5 * swiglu(rmsnorm(x, ffn2_norm[l]), ffn2_w_in[l], ffn2_w_out[l])
            if l == N_A_LAYERS - 1:
                kvp = (rmsnorm(x, kv_norm) @ w_kv).reshape(Bn, L, 6, G, DH)
                kv = (kvp[:, :, 0], kvp[:, :, 1], kvp[:, :, 2], kvp[:, :, 3], kvp[:, :, 4], kvp[:, :, 5])
        return rmsnorm(x, final_norm), jnp.stack(Cs), jnp.stack(ns), jnp.stack(ms), kv

    Bp, S = x_prompt.shape[:2]
    mk_list, mv_list = [], []
    for l in range(DEPTH):
        mkv = (rmsnorm(mem_prompt, mem_norm[l]) @ w_mem_kv[l]).reshape(Bp, MEM_TOKENS, 2, MEM_HEADS, DH)
        mk_list.append(mkv[:, :, 0])
        mv_list.append(mkv[:, :, 1])
    mem_k_p = jnp.stack(mk_list)
    mem_v_p = jnp.stack(mv_list)
    zC = jnp.zeros((N_A_LAYERS, Bp, A_HEADS, DH, DH), f32)
    zn = jnp.zeros((N_A_LAYERS, Bp, A_HEADS, DH), f32)
    zm = jnp.zeros((N_A_LAYERS, Bp, A_HEADS), f32)
    prompt_mixer = lambda q, g, kv: nsa_prompt(q, g, kv, cmp_pe_k, cmp_w1_k, cmp_w2_k, cmp_pe_v, cmp_w1_v, cmp_w2_v)
    y_prompt, C_p, n_p, m_p, kv_p = trunk(x_prompt, mem_k_p, mem_v_p, zC, zn, zm, prompt_mixer)

    sample_mixer = lambda q, g, kv: nsa_sample(q, g, kv, cache_cmp_k, cache_cmp_v, cache_slc_k, cache_slc_v,
                                               cache_win_k, cache_win_v, page_table, cmp_pe_k, cmp_w1_k,
                                               cmp_w2_k, cmp_pe_v, cmp_w1_v, cmp_w2_v)
    y_sample, C_s, n_s, m_s, kv_s = trunk(x_sample, cache_mem_k, cache_mem_v, state_mlstm_C, state_mlstm_n,
                                          state_mlstm_m, sample_mixer)

    w_p = min(WINDOW, S)
    cmp_k_p, cmp_v_p, slc_k_p, slc_v_p, win_k_all, win_v_all = kv_p
    win_k_p = win_k_all[:, S - w_p:]
    win_v_p = win_v_all[:, S - w_p:]
    cmp_k_s, cmp_v_s, slc_k_s, slc_v_s, win_k_s, win_v_s = kv_s
    return (y_prompt, y_sample, mem_k_p, mem_v_p, C_p, n_p, m_p, C_s, n_s, m_s,
            cmp_k_p, cmp_v_p, slc_k_p, slc_v_p, win_k_p, win_v_p,
            cmp_k_s, cmp_v_s, slc_k_s, slc_v_s, win_k_s, win_v_s)
```

```python
import functools

import jax
import jax.numpy as jnp
from jax import lax
from jax.experimental import pallas as pl
from jax.experimental.pallas import tpu as pltpu

D_MODEL = 2048
DEPTH = 2
PAST_LEN = 2048
PAGE_SIZE = 128
HEAD_DIM = 128
N_A_LAYERS = DEPTH // 2
MEM_TOKENS = 256
MEM_HEADS = 4
MEM_WIDTH = MEM_HEADS * HEAD_DIM
TOK_WIDTH = D_MODEL - MEM_WIDTH
A_HEADS = TOK_WIDTH // HEAD_DIM
B_HEADS = TOK_WIDTH // HEAD_DIM
B_KV_GROUPS = 4
B_GROUP = B_HEADS // B_KV_GROUPS
D_FF = ((8 * D_MODEL // 3 + 255) // 256) * 256
MLSTM_CHUNK = 128
CMP_STRIDE = 16
CMP_LEN = 2 * CMP_STRIDE
SLC_BLOCK = 64
N_SELECT = 16
WINDOW = 512
WIN_QBLK = 128
SEL_QBLK = 64
FORCED_SCORE = 1e4
NEG = -1e30
EPS = 1e-6
SCALE = HEAD_DIM ** -0.5

F32 = jnp.float32
BF16 = jnp.bfloat16
VMEM_LIMIT = 56 * 1024 * 1024


def _cparams(*sem):
    return pltpu.CompilerParams(dimension_semantics=sem, vmem_limit_bytes=VMEM_LIMIT)


def _rms_bf16(x, g):
    ms = jnp.mean(x * x, axis=-1, keepdims=True)
    return (x * lax.rsqrt(ms + EPS) * g).astype(BF16)


def _norm_matmul_kernel(x_ref, g_ref, w_ref, o_ref, h_ref):
    @pl.when(pl.program_id(1) == 0)
    def _():
        h_ref[...] = _rms_bf16(x_ref[...], g_ref[...])

    o_ref[...] = jnp.dot(h_ref[...], w_ref[...], preferred_element_type=F32).astype(o_ref.dtype)


def norm_matmul(x, g, w, *, tm, tn, out_dtype=F32):
    M, D = x.shape
    N = w.shape[1]
    return pl.pallas_call(
        _norm_matmul_kernel,
        grid=(M // tm, N // tn),
        in_specs=[pl.BlockSpec((tm, D), lambda i, j: (i, 0)),
                  pl.BlockSpec((1, D), lambda i, j: (0, 0)),
                  pl.BlockSpec((D, tn), lambda i, j: (0, j))],
        out_specs=pl.BlockSpec((tm, tn), lambda i, j: (i, j)),
        out_shape=jax.ShapeDtypeStruct((M, N), out_dtype),
        scratch_shapes=[pltpu.VMEM((tm, D), BF16)],
        compiler_params=_cparams("parallel", "arbitrary"),
        name="norm_matmul",
    )(x, g.reshape(1, D), w)


def _ffn_kernel(x_ref, g_ref, wg_ref, wu_ref, wo_ref, fg_ref, o_ref, h_ref, acc_ref, *, final_norm):
    f = pl.program_id(1)

    @pl.when(f == 0)
    def _():
        h_ref[...] = _rms_bf16(x_ref[...], g_ref[...])
        acc_ref[...] = jnp.zeros_like(acc_ref)

    h = h_ref[...]
    gate = jnp.dot(h, wg_ref[...], preferred_element_type=F32)
    up = jnp.dot(h, wu_ref[...], preferred_element_type=F32)
    act = (gate * jax.nn.sigmoid(gate) * up).astype(BF16)
    acc_ref[...] += jnp.dot(act, wo_ref[...], preferred_element_type=F32)

    @pl.when(f == pl.num_programs(1) - 1)
    def _():
        y = x_ref[...] + 0.5 * acc_ref[...]
        if final_norm:
            ms = jnp.mean(y * y, axis=-1, keepdims=True)
            y = y * lax.rsqrt(ms + EPS) * fg_ref[...]
        o_ref[...] = y


def ffn(x, g, w_in, w_out, final_gain=None, *, tm, tf):
    M, D = x.shape
    F = w_out.shape[0]
    nf = F // tf
    fg = jnp.ones((D,), F32) if final_gain is None else final_gain
    return pl.pallas_call(
        functools.partial(_ffn_kernel, final_norm=final_gain is not None),
        grid=(M // tm, nf),
        in_specs=[pl.BlockSpec((tm, D), lambda i, f: (i, 0)),
                  pl.BlockSpec((1, D), lambda i, f: (0, 0)),
                  pl.BlockSpec((D, tf), lambda i, f: (0, f)),
                  pl.BlockSpec((D, tf), lambda i, f: (0, f + nf)),
                  pl.BlockSpec((tf, D), lambda i, f: (f, 0)),
                  pl.BlockSpec((1, D), lambda i, f: (0, 0))],
        out_specs=pl.BlockSpec((tm, D), lambda i, f: (i, 0)),
        out_shape=jax.ShapeDtypeStruct((M, D), F32),
        scratch_shapes=[pltpu.VMEM((tm, D), BF16), pltpu.VMEM((tm, D), F32)],
        compiler_params=_cparams("parallel", "arbitrary"),
        name="ffn",
    )(x, g.reshape(1, D), w_in, w_in, w_out, fg.reshape(1, D))


def _outproj_kernel(x_ref, mix_ref, mo_ref, wa_ref, wb_ref, o_ref):
    y = jnp.dot(mix_ref[...].astype(BF16), wa_ref[...], preferred_element_type=F32)
    y += jnp.dot(mo_ref[...].astype(BF16), wb_ref[...], preferred_element_type=F32)
    o_ref[...] = x_ref[...] + y


def outproj(x, mix, mo, w, *, tm):
    M, D = x.shape
    Ka, Kb = mix.shape[1], mo.shape[1]
    return pl.pallas_call(
        _outproj_kernel,
        grid=(M // tm,),
        in_specs=[pl.BlockSpec((tm, D), lambda i: (i, 0)),
                  pl.BlockSpec((tm, Ka), lambda i: (i, 0)),
                  pl.BlockSpec((tm, Kb), lambda i: (i, 0)),
                  pl.BlockSpec((Ka, D), lambda i: (0, 0)),
                  pl.BlockSpec((Kb, D), lambda i: (0, 0))],
        out_specs=pl.BlockSpec((tm, D), lambda i: (i, 0)),
        out_shape=jax.ShapeDtypeStruct((M, D), F32),
        compiler_params=_cparams("parallel"),
        name="outproj",
    )(x, mix, mo, w[:Ka], w[Ka:])


def _j_masked_probs(s, mask):
    p = jax.nn.softmax(jnp.where(mask, s, NEG), axis=-1)
    return jnp.where(mask, p, 0.0)


def _j_mem_attend(q, k, v):
    s = jnp.einsum('blhd,bmhd->bhlm', q, k).astype(F32) * SCALE
    p = jax.nn.softmax(s, axis=-1).astype(v.dtype)
    return jnp.einsum('bhlm,bmhd->blhd', p, v)


def _j_head_rmsnorm(h, g):
    h = h * lax.rsqrt(jnp.mean(h * h, axis=-1, keepdims=True) + EPS)
    return h * g.reshape(h.shape[-2:]).astype(F32)


def _j_mlstm(q, k, v, ig, lf, C0, n0, m0):
    B, L, H, D = q.shape
    c = min(MLSTM_CHUNK, L)
    nc = -(-L // c)
    pad = nc * c - L

    def blocks(a, fill):
        a = jnp.pad(a.astype(F32), [(0, 0), (0, pad)] + [(0, 0)] * (a.ndim - 2), constant_values=fill)
        a = a.reshape((B, nc, c) + a.shape[2:])
        return jnp.swapaxes(jnp.moveaxis(a, 1, 0), 2, 3)

    xs = (blocks(q, 0.0), blocks(k, 0.0), blocks(v, 0.0), blocks(ig, NEG), blocks(lf, 0.0))
    tril = jnp.tril(jnp.ones((c, c), dtype=bool))

    def step(carry, xs_c):
        C, n, m = carry
        qc, kc, vc, ic, fc = xs_c
        b = jnp.cumsum(fc, axis=-1)
        dlog = jnp.where(tril, b[..., :, None] - b[..., None, :] + ic[..., None, :], -jnp.inf)
        inter = b + m[..., None]
        mt = jnp.maximum(inter, jnp.max(dlog, axis=-1))
        a = jnp.exp(inter - mt)
        w = jnp.exp(dlog - mt[..., None]) * jnp.einsum('bhtd,bhsd->bhts', qc, kc)
        num = a[..., None] * jnp.einsum('bhtd,bhde->bhte', qc, C) + jnp.einsum('bhts,bhse->bhte', w, vc)
        den = a * jnp.einsum('bhtd,bhd->bht', qc, n) + jnp.sum(w, axis=-1)
        h = num / jnp.maximum(jnp.abs(den), jnp.exp(-mt))[..., None]
        bl = b[..., -1]
        g = bl[..., None] - b + ic
        m_new = jnp.maximum(bl + m, jnp.max(g, axis=-1))
        decay = jnp.exp(bl + m - m_new)
        wk = jnp.exp(g - m_new[..., None])
        C_new = decay[..., None, None] * C + jnp.einsum('bhs,bhsd,bhse->bhde', wk, kc, vc)
        n_new = decay[..., None] * n + jnp.einsum('bhs,bhsd->bhd', wk, kc)
        return (C_new, n_new, m_new), h

    (Cf, nf, mf), hs = lax.scan(step, (C0.astype(F32), n0.astype(F32), m0.astype(F32)), xs)
    hs = jnp.moveaxis(jnp.swapaxes(hs, 2, 3), 0, 1).reshape(B, nc * c, H, D)[:, :L]
    return hs, (Cf, nf, mf)


def _j_compress(x, pe, w1, w2):
    B, T, G, D = x.shape
    nch = T // CMP_STRIDE
    xs = x[:, :nch * CMP_STRIDE].reshape(B, nch, CMP_STRIDE, G, D)
    pe2 = pe.reshape(2, CMP_STRIDE, D)
    w12 = w1.reshape(2, CMP_STRIDE, D, -1)
    first = jnp.einsum('bncgd,cde->bnge', xs + pe2[0][:, None, :], w12[0])
    second = jnp.einsum('bncgd,cde->bnge', xs + pe2[1][:, None, :], w12[1])
    hid = jax.nn.gelu(first[:, :-1] + second[:, 1:])
    return hid @ w2


def _j_cmp_select(q, pos, kc, vc, n_slc):
    s = jnp.einsum('blgrd,bngd->bgrln', q, kc).astype(F32) * SCALE
    n_cmp = kc.shape[1]
    start = jnp.arange(n_cmp, dtype=jnp.int32) * CMP_STRIDE
    mask = (start + CMP_LEN - 1)[None, :] <= pos[:, None]
    p = _j_masked_probs(s, mask)
    o_c = jnp.einsum('bgrln,bngd->blgrd', p.astype(vc.dtype), vc)
    j = jnp.arange(n_slc, dtype=jnp.int32)
    overlap = ((start[:, None] < (j[None, :] + 1) * SLC_BLOCK)
               & (start[:, None] + CMP_LEN > j[None, :] * SLC_BLOCK)).astype(F32)
    imp = jnp.einsum('bgrln,nj->bglj', p, overlap)
    cur = (pos // SLC_BLOCK)[:, None]
    forced = (j[None, :] == 0) | (j[None, :] == cur) | (j[None, :] == cur - 1)
    valid = j[None, :] * SLC_BLOCK <= pos[:, None]
    score = jnp.where(forced, FORCED_SCORE, jnp.where(valid, imp, -1.0))
    _, idx = lax.top_k(score, min(N_SELECT, n_slc))
    return o_c, jnp.transpose(idx, (0, 2, 1, 3))


def _j_sel_attend(q, pos, idx, gather):
    kb, vb = gather(idx)
    B, L, G, K, C, D = kb.shape
    R = q.shape[3]
    s = jnp.einsum('blgrd,blgkcd->blgrkc', q, kb).astype(F32) * SCALE
    kpos = idx[..., None] * SLC_BLOCK + jnp.arange(SLC_BLOCK, dtype=jnp.int32)
    mask = (kpos <= pos[None, :, None, None, None]).reshape(B, L, G, 1, K * C)
    p = _j_masked_probs(s.reshape(B, L, G, R, K * C), mask).astype(vb.dtype)
    return jnp.einsum('blgrn,blgnd->blgrd', p, vb.reshape(B, L, G, K * C, D))


def _j_sel_attend_blocked(q, pos, idx, gather):
    B, L = q.shape[:2]
    qb = SEL_QBLK if L % SEL_QBLK == 0 else L
    nb = L // qb
    if nb == 1:
        return _j_sel_attend(q, pos, idx, gather)
    qs = jnp.swapaxes(q.reshape((B, nb, qb) + q.shape[2:]), 0, 1)
    ps = pos.reshape(nb, qb)
    ids = jnp.swapaxes(idx.reshape((B, nb, qb) + idx.shape[2:]), 0, 1)
    out = lax.map(lambda a: _j_sel_attend(a[0], a[1], a[2], gather), (qs, ps, ids))
    return jnp.swapaxes(out, 0, 1).reshape((B, L) + out.shape[3:])


def _j_window_prompt(q, kw, vw):
    B, S, G, R, D = q.shape
    nb = S // WIN_QBLK
    span = WINDOW + WIN_QBLK
    padw = [(0, 0), (WINDOW, 0), (0, 0), (0, 0)]
    kp, vp = jnp.pad(kw, padw), jnp.pad(vw, padw)
    idx = jnp.arange(nb, dtype=jnp.int32)[:, None] * WIN_QBLK + jnp.arange(span, dtype=jnp.int32)[None, :]
    kb, vb = kp[:, idx], vp[:, idx]
    qb = q.reshape(B, nb, WIN_QBLK, G, R, D)
    s = jnp.einsum('bnqgrd,bnkgd->bngrqk', qb, kb).astype(F32) * SCALE
    kpos = idx - WINDOW
    qpos = jnp.arange(nb, dtype=jnp.int32)[:, None] * WIN_QBLK + jnp.arange(WIN_QBLK, dtype=jnp.int32)[None, :]
    mask = ((kpos[:, None, :] <= qpos[:, :, None]) & (kpos[:, None, :] > qpos[:, :, None] - WINDOW)
            & (kpos[:, None, :] >= 0))
    p = _j_masked_probs(s, mask[None, :, None, None]).astype(vb.dtype)
    return jnp.einsum('bngrqk,bnkgd->bnqgrd', p, vb).reshape(B, S, G, R, D)


def _j_window_sample(q, pos, kw_new, vw_new, kw_buf, vw_buf):
    w_buf = kw_buf.shape[1]
    kk = jnp.concatenate([kw_buf, kw_new], axis=1)
    vv = jnp.concatenate([vw_buf, vw_new], axis=1)
    kpos = PAST_LEN - w_buf + jnp.arange(kk.shape[1], dtype=jnp.int32)
    s = jnp.einsum('blgrd,bkgd->bgrlk', q, kk).astype(F32) * SCALE
    mask = (kpos[None, :] <= pos[:, None]) & (kpos[None, :] > pos[:, None] - WINDOW)
    p = _j_masked_probs(s, mask).astype(vv.dtype)
    return jnp.einsum('bgrlk,bkgd->blgrd', p, vv)


def _j_combine(gates, o_c, o_s, o_w):
    out = gates[..., 0:1] * o_c + gates[..., 1:2] * o_s + gates[..., 2:3] * o_w
    return out.reshape(out.shape[:2] + (-1,))


def _j_nsa_prompt(q, gates, kv, pe_k, w1_k, w2_k, pe_v, w1_v, w2_v):
    kc_raw, vc_raw, ks, vs, kw, vw = kv
    B, L, G = ks.shape[:3]
    pos = jnp.arange(L, dtype=jnp.int32)
    kc = _j_compress(kc_raw, pe_k, w1_k, w2_k)
    vc = _j_compress(vc_raw, pe_v, w1_v, w2_v)
    n_slc = -(-L // SLC_BLOCK)
    o_c, idx = _j_cmp_select(q, pos, kc, vc, n_slc)
    ksb = ks.reshape(B, n_slc, SLC_BLOCK, G, HEAD_DIM)
    vsb = vs.reshape(B, n_slc, SLC_BLOCK, G, HEAD_DIM)
    b_i = jnp.arange(B)[:, None, None, None]
    g_i = jnp.arange(G)[None, None, :, None]
    gather = lambda ix: (ksb[b_i, ix, :, g_i, :], vsb[b_i, ix, :, g_i, :])
    o_s = _j_sel_attend_blocked(q, pos, idx, gather)
    o_w = _j_window_prompt(q, kw, vw)
    return _j_combine(gates, o_c, o_s, o_w)


def _j_nsa_sample(q, gates, kv, cache_cmp_k, cache_cmp_v, cache_slc_k, cache_slc_v, cache_win_k, cache_win_v,
                  page_table, pe_k, w1_k, w2_k, pe_v, w1_v, w2_v):
    kc_new, vc_new, ks_new, vs_new, kw_new, vw_new = kv
    B, L, G = ks_new.shape[:3]
    n_pages = PAST_LEN // PAGE_SIZE
    pos = PAST_LEN + jnp.arange(L, dtype=jnp.int32)

    def paged_rows(pool):
        return pool[page_table].reshape(B, n_pages * PAGE_SIZE, G, HEAD_DIM)

    kc = _j_compress(jnp.concatenate([paged_rows(cache_cmp_k), kc_new], axis=1), pe_k, w1_k, w2_k)
    vc = _j_compress(jnp.concatenate([paged_rows(cache_cmp_v), vc_new], axis=1), pe_v, w1_v, w2_v)
    n_slc = -(-(PAST_LEN + L) // SLC_BLOCK)
    o_c, idx = _j_cmp_select(q, pos, kc, vc, n_slc)
    n_pb = PAST_LEN // SLC_BLOCK
    n_tail = n_slc - n_pb
    bpp = PAGE_SIZE // SLC_BLOCK

    def tail_blocks(x):
        x = jnp.pad(x, ((0, 0), (0, n_tail * SLC_BLOCK - L), (0, 0), (0, 0)))
        return x.reshape(B, n_tail, SLC_BLOCK, G, HEAD_DIM)

    ks_tail, vs_tail = tail_blocks(ks_new), tail_blocks(vs_new)
    ks_pool = cache_slc_k.reshape(-1, bpp, SLC_BLOCK, G, HEAD_DIM)
    vs_pool = cache_slc_v.reshape(-1, bpp, SLC_BLOCK, G, HEAD_DIM)
    b_i = jnp.arange(B)[:, None, None, None]
    g_i = jnp.arange(G)[None, None, :, None]

    def gather(ix):
        in_past = (ix < n_pb)[..., None, None]
        phys = page_table[b_i, jnp.clip(ix // bpp, 0, n_pages - 1)]
        sub = ix % bpp
        jt = jnp.clip(ix - n_pb, 0, n_tail - 1)
        pick = lambda pool, tail: jnp.where(in_past, pool[phys, sub, :, g_i, :], tail[b_i, jt, :, g_i, :])
        return pick(ks_pool, ks_tail), pick(vs_pool, vs_tail)

    o_s = _j_sel_attend_blocked(q, pos, idx, gather)
    o_w = _j_window_sample(q, pos, kw_new, vw_new, cache_win_k, cache_win_v)
    return _j_combine(gates, o_c, o_s, o_w)


def kernel(x_prompt, x_sample, mem_prompt, cache_mem_k, cache_mem_v, state_mlstm_C, state_mlstm_n, state_mlstm_m, cache_cmp_k, cache_cmp_v, cache_slc_k, cache_slc_v, cache_win_k, cache_win_v, page_table, ffn1_norm, ffn1_w_in, ffn1_w_out, ffn2_norm, ffn2_w_in, ffn2_w_out, mix_norm, a_w_in, a_b_i, a_b_f, a_head_norm, b_w_in, mem_norm, w_mem_kv, w_out, kv_norm, w_kv, cmp_pe_k, cmp_w1_k, cmp_w2_k, cmp_pe_v, cmp_w1_v, cmp_w2_v, final_norm):
    G, R, DH = B_KV_GROUPS, B_GROUP, HEAD_DIM
    split_a = [TOK_WIDTH, 2 * TOK_WIDTH, 3 * TOK_WIDTH, 4 * TOK_WIDTH, 4 * TOK_WIDTH + A_HEADS,
               4 * TOK_WIDTH + 2 * A_HEADS]
    split_b = [TOK_WIDTH, TOK_WIDTH + 3 * B_HEADS]

    def pad_cols(w, mult):
        n = w.shape[-1]
        return jnp.pad(w, ((0, 0), (0, -n % mult)))

    ffn1_wi, ffn1_wo = ffn1_w_in.astype(BF16), ffn1_w_out.astype(BF16)
    ffn2_wi, ffn2_wo = ffn2_w_in.astype(BF16), ffn2_w_out.astype(BF16)
    a_w = pad_cols(a_w_in[0], 256).astype(BF16)
    b_w = pad_cols(b_w_in[0], 256).astype(BF16)
    w_o = w_out.astype(BF16)
    w_kv_b = w_kv.astype(BF16)
    w_mkv = w_mem_kv.astype(BF16)

    def trunk(x, mem_k, mem_v, C0, n0, m0, nsa_mixer, tm):
        Bn, L, _ = x.shape
        M = Bn * L
        x = x.reshape(M, D_MODEL)
        Cs, ns, ms = [], [], []
        kv = None
        for l in range(DEPTH):
            x = ffn(x, ffn1_norm[l], ffn1_wi[l], ffn1_wo[l], tm=tm, tf=512)
            if l < N_A_LAYERS:
                proj = norm_matmul(x, mix_norm[l], a_w, tm=tm, tn=256)[:, :a_w_in.shape[-1]].reshape(Bn, L, -1)
                q, k, v, o, ig, fg, qm = jnp.split(proj, split_a, axis=-1)
                hh, (C, n, m) = _j_mlstm(
                    q.reshape(Bn, L, A_HEADS, DH), k.reshape(Bn, L, A_HEADS, DH) * SCALE,
                    v.reshape(Bn, L, A_HEADS, DH), ig.astype(F32) + a_b_i[l],
                    jax.nn.log_sigmoid(fg.astype(F32) + a_b_f[l]), C0[l], n0[l], m0[l])
                gate_o = jax.nn.sigmoid(o.astype(F32)).reshape(Bn, L, A_HEADS, DH)
                mix = (gate_o * _j_head_rmsnorm(hh, a_head_norm[l])).reshape(Bn, L, TOK_WIDTH)
                Cs.append(C)
                ns.append(n)
                ms.append(m)
            else:
                proj = norm_matmul(x, mix_norm[l], b_w, tm=tm, tn=256)[:, :b_w_in.shape[-1]].reshape(Bn, L, -1)
                q, gts, qm = jnp.split(proj, split_b, axis=-1)
                gates = jax.nn.sigmoid(gts.astype(F32)).reshape(Bn, L, G, R, 3)
                mix = nsa_mixer(q.reshape(Bn, L, G, R, DH), gates, kv)
            mo = _j_mem_attend(qm.reshape(Bn, L, MEM_HEADS, DH), mem_k[l], mem_v[l]).reshape(Bn, L, MEM_WIDTH)
            x = outproj(x, mix.reshape(M, TOK_WIDTH), mo.reshape(M, MEM_WIDTH), w_o[l], tm=tm)
            last = l == DEPTH - 1
            x = ffn(x, ffn2_norm[l], ffn2_wi[l], ffn2_wo[l], final_norm if last else None, tm=tm, tf=512)
            if l == N_A_LAYERS - 1:
                kvp = norm_matmul(x, kv_norm, w_kv_b, tm=tm, tn=512).reshape(Bn, L, 6, G, DH)
                kv = tuple(kvp[:, :, i] for i in range(6))
        return x.reshape(Bn, L, D_MODEL), jnp.stack(Cs), jnp.stack(ns), jnp.stack(ms), kv

    Bp, S = x_prompt.shape[:2]
    mk_list, mv_list = [], []
    memx = mem_prompt.reshape(Bp * MEM_TOKENS, D_MODEL)
    for l in range(DEPTH):
        mkv = norm_matmul(memx, mem_norm[l], w_mkv[l], tm=512, tn=512).reshape(Bp, MEM_TOKENS, 2, MEM_HEADS, DH)
        mk_list.append(mkv[:, :, 0])
        mv_list.append(mkv[:, :, 1])
    mem_k_p = jnp.stack(mk_list)
    mem_v_p = jnp.stack(mv_list)
    zC = jnp.zeros((N_A_LAYERS, Bp, A_HEADS, DH, DH), F32)
    zn = jnp.zeros((N_A_LAYERS, Bp, A_HEADS, DH), F32)
    zm = jnp.zeros((N_A_LAYERS, Bp, A_HEADS), F32)
    prompt_mixer = lambda q, g, kv: _j_nsa_prompt(q, g, kv, cmp_pe_k, cmp_w1_k, cmp_w2_k, cmp_pe_v, cmp_w1_v, cmp_w2_v)
    y_prompt, C_p, n_p, m_p, kv_p = trunk(x_prompt, mem_k_p, mem_v_p, zC, zn, zm, prompt_mixer, 512)

    sample_mixer = lambda q, g, kv: _j_nsa_sample(q, g, kv, cache_cmp_k, cache_cmp_v, cache_slc_k, cache_slc_v,
                                                  cache_win_k, cache_win_v, page_table, cmp_pe_k, cmp_w1_k,
                                                  cmp_w2_k, cmp_pe_v, cmp_w1_v, cmp_w2_v)
    y_sample, C_s, n_s, m_s, kv_s = trunk(x_sample, cache_mem_k, cache_mem_v, state_mlstm_C, state_mlstm_n,
                                          state_mlstm_m, sample_mixer, 128)

    w_p = min(WINDOW, S)
    cmp_k_p, cmp_v_p, slc_k_p, slc_v_p, win_k_all, win_v_all = kv_p
    win_k_p = win_k_all[:, S - w_p:]
    win_v_p = win_v_all[:, S - w_p:]
    cmp_k_s, cmp_v_s, slc_k_s, slc_v_s, win_k_s, win_v_s = kv_s
    return (y_prompt, y_sample, mem_k_p, mem_v_p, C_p, n_p, m_p, C_s, n_s, m_s,
            cmp_k_p, cmp_v_p, slc_k_p, slc_v_p, win_k_p, win_v_p,
            cmp_k_s, cmp_v_s, slc_k_s, slc_v_s, win_k_s, win_v_s)
```

```python
import functools

import jax
import jax.numpy as jnp
from jax import lax
from jax.experimental import pallas as pl
from jax.experimental.pallas import tpu as pltpu

D_MODEL = 2048
DEPTH = 2
PAST_LEN = 2048
PAGE_SIZE = 128
HEAD_DIM = 128
N_A_LAYERS = DEPTH // 2
MEM_TOKENS = 256
MEM_HEADS = 4
MEM_WIDTH = MEM_HEADS * HEAD_DIM
TOK_WIDTH = D_MODEL - MEM_WIDTH
A_HEADS = TOK_WIDTH // HEAD_DIM
B_HEADS = TOK_WIDTH // HEAD_DIM
B_KV_GROUPS = 4
B_GROUP = B_HEADS // B_KV_GROUPS
D_FF = ((8 * D_MODEL // 3 + 255) // 256) * 256
MLSTM_CHUNK = 128
CMP_STRIDE = 16
CMP_LEN = 2 * CMP_STRIDE
SLC_BLOCK = 64
N_SELECT = 16
WINDOW = 512
WIN_QBLK = 128
SEL_QBLK = 64
FORCED_SCORE = 1e4
NEG = -1e30
EPS = 1e-6
SCALE = HEAD_DIM ** -0.5

F32 = jnp.float32
BF16 = jnp.bfloat16
VMEM_LIMIT = 56 * 1024 * 1024


def _cparams(*sem):
    return pltpu.CompilerParams(dimension_semantics=sem, vmem_limit_bytes=VMEM_LIMIT)


def _rms_bf16(x, g):
    ms = jnp.mean(x * x, axis=-1, keepdims=True)
    return (x * lax.rsqrt(ms + EPS) * g).astype(BF16)


def _norm_matmul_kernel(x_ref, g_ref, w_ref, o_ref, h_ref):
    @pl.when(pl.program_id(1) == 0)
    def _():
        h_ref[...] = _rms_bf16(x_ref[...], g_ref[...])

    o_ref[...] = jnp.dot(h_ref[...], w_ref[...], preferred_element_type=F32).astype(o_ref.dtype)


def norm_matmul(x, g, w, *, tm, tn, out_dtype=F32):
    M, D = x.shape
    N = w.shape[1]
    return pl.pallas_call(
        _norm_matmul_kernel,
        grid=(M // tm, N // tn),
        in_specs=[pl.BlockSpec((tm, D), lambda i, j: (i, 0)),
                  pl.BlockSpec((1, D), lambda i, j: (0, 0)),
                  pl.BlockSpec((D, tn), lambda i, j: (0, j))],
        out_specs=pl.BlockSpec((tm, tn), lambda i, j: (i, j)),
        out_shape=jax.ShapeDtypeStruct((M, N), out_dtype),
        scratch_shapes=[pltpu.VMEM((tm, D), BF16)],
        compiler_params=_cparams("parallel", "arbitrary"),
        name="norm_matmul",
    )(x, g.reshape(1, D), w)


def _ffn_kernel(x_ref, g_ref, wg_ref, wu_ref, wo_ref, fg_ref, o_ref, h_ref, acc_ref, *, final_norm):
    f = pl.program_id(1)

    @pl.when(f == 0)
    def _():
        h_ref[...] = _rms_bf16(x_ref[...], g_ref[...])
        acc_ref[...] = jnp.zeros_like(acc_ref)

    h = h_ref[...]
    gate = jnp.dot(h, wg_ref[...], preferred_element_type=F32)
    up = jnp.dot(h, wu_ref[...], preferred_element_type=F32)
    act = (gate * jax.nn.sigmoid(gate) * up).astype(BF16)
    acc_ref[...] += jnp.dot(act, wo_ref[...], preferred_element_type=F32)

    @pl.when(f == pl.num_programs(1) - 1)
    def _():
        y = x_ref[...] + 0.5 * acc_ref[...]
        if final_norm:
            ms = jnp.mean(y * y, axis=-1, keepdims=True)
            y = y * lax.rsqrt(ms + EPS) * fg_ref[...]
        o_ref[...] = y


def ffn(x, g, w_in, w_out, final_gain=None, *, tm, tf):
    M, D = x.shape
    F = w_out.shape[0]
    nf = F // tf
    fg = jnp.ones((D,), F32) if final_gain is None else final_gain
    return pl.pallas_call(
        functools.partial(_ffn_kernel, final_norm=final_gain is not None),
        grid=(M // tm, nf),
        in_specs=[pl.BlockSpec((tm, D), lambda i, f: (i, 0)),
                  pl.BlockSpec((1, D), lambda i, f: (0, 0)),
                  pl.BlockSpec((D, tf), lambda i, f: (0, f)),
                  pl.BlockSpec((D, tf), lambda i, f: (0, f + nf)),
                  pl.BlockSpec((tf, D), lambda i, f: (f, 0)),
                  pl.BlockSpec((1, D), lambda i, f: (0, 0))],
        out_specs=pl.BlockSpec((tm, D), lambda i, f: (i, 0)),
        out_shape=jax.ShapeDtypeStruct((M, D), F32),
        scratch_shapes=[pltpu.VMEM((tm, D), BF16), pltpu.VMEM((tm, D), F32)],
        compiler_params=_cparams("parallel", "arbitrary"),
        name="ffn",
    )(x, g.reshape(1, D), w_in, w_in, w_out, fg.reshape(1, D))


def _outproj_kernel(x_ref, mix_ref, mo_ref, wa_ref, wb_ref, o_ref):
    y = jnp.dot(mix_ref[...].astype(BF16), wa_ref[...], preferred_element_type=F32)
    y += jnp.dot(mo_ref[...].astype(BF16), wb_ref[...], preferred_element_type=F32)
    o_ref[...] = x_ref[...] + y


def outproj(x, mix, mo, w, *, tm):
    M, D = x.shape
    Ka, Kb = mix.shape[1], mo.shape[1]
    return pl.pallas_call(
        _outproj_kernel,
        grid=(M // tm,),
        in_specs=[pl.BlockSpec((tm, D), lambda i: (i, 0)),
                  pl.BlockSpec((tm, Ka), lambda i: (i, 0)),
                  pl.BlockSpec((tm, Kb), lambda i: (i, 0)),
                  pl.BlockSpec((Ka, D), lambda i: (0, 0)),
                  pl.BlockSpec((Kb, D), lambda i: (0, 0))],
        out_specs=pl.BlockSpec((tm, D), lambda i: (i, 0)),
        out_shape=jax.ShapeDtypeStruct((M, D), F32),
        compiler_params=_cparams("parallel"),
        name="outproj",
    )(x, mix, mo, w[:Ka], w[Ka:])


def _compress_kernel(*refs, n_pages, cpp):
    page_refs = refs[:n_pages]
    pe_ref, w1_ref, w2_ref, o_ref = refs[n_pages:]
    G, DH = B_KV_GROUPS, HEAD_DIM
    nch = n_pages * cpp
    acc = jnp.zeros((G * nch + 8, 2 * DH), F32)
    for c in range(CMP_STRIDE):
        rows = [page_refs[p][:, c, :] for p in range(n_pages)]
        xc = rows[0] if n_pages == 1 else jnp.concatenate(rows, axis=0)
        lhs = jnp.concatenate([xc[:, g * DH:(g + 1) * DH] for g in range(G)] + [pe_ref[c]], axis=0)
        acc = acc + jnp.dot(lhs.astype(BF16), w1_ref[c], preferred_element_type=F32)
    bias_first = acc[G * nch:G * nch + 1, :DH]
    bias_second = acc[G * nch + 1:G * nch + 2, DH:]
    for g in range(G):
        first = acc[g * nch:(g + 1) * nch, :DH] + bias_first
        second = acc[g * nch:(g + 1) * nch, DH:] + bias_second
        hid = jax.nn.gelu(first + pltpu.roll(second, nch - 1, 0))
        o_ref[g] = jnp.dot(hid.astype(BF16), w2_ref[...], preferred_element_type=F32)


def _compress_weights(pe, w1, w2):
    DH = HEAD_DIM
    pe2 = pe.reshape(2, CMP_STRIDE, DH)
    pe_rows = jnp.concatenate([pe2[0][:, None], pe2[1][:, None], jnp.zeros((CMP_STRIDE, 6, DH), F32)], axis=1)
    w12 = w1.reshape(2, CMP_STRIDE, DH, -1)
    w1cat = jnp.concatenate([w12[0], w12[1]], axis=-1).astype(BF16)
    return pe_rows, w1cat, w2.astype(BF16)


def compress_prompt(kvp, col_block, n_batch, pe, w1, w2):
    M = kvp.shape[0]
    nch = M // n_batch // CMP_STRIDE
    pe_rows, w1cat, w2b = _compress_weights(pe, w1, w2)
    x = kvp.reshape(M // CMP_STRIDE, CMP_STRIDE, kvp.shape[1])
    return pl.pallas_call(
        functools.partial(_compress_kernel, n_pages=1, cpp=nch),
        grid=(n_batch,),
        in_specs=[pl.BlockSpec((nch, CMP_STRIDE, 512), lambda b: (b, 0, col_block)),
                  pl.BlockSpec((CMP_STRIDE, 8, HEAD_DIM), lambda b: (0, 0, 0)),
                  pl.BlockSpec((CMP_STRIDE, HEAD_DIM, 2 * HEAD_DIM), lambda b: (0, 0, 0)),
                  pl.BlockSpec((HEAD_DIM, HEAD_DIM), lambda b: (0, 0))],
        out_specs=pl.BlockSpec((None, B_KV_GROUPS, nch, HEAD_DIM), lambda b: (b, 0, 0, 0)),
        out_shape=jax.ShapeDtypeStruct((n_batch, B_KV_GROUPS, nch, HEAD_DIM), F32),
        compiler_params=_cparams("parallel"),
        name="compress_prompt",
    )(x, pe_rows, w1cat, w2b)


_NT = (((1,), (1,)), ((), ()))


def _split2(x):
    hi = x.astype(BF16)
    return hi, (x - hi.astype(F32)).astype(BF16)


def _cmp_select_kernel(q_ref, kc_ref, vc_ref, oc_ref, mem_ref, sc_ref, *, T, nch, nsl, pos0):
    G, R, DH = B_KV_GROUPS, B_GROUP, HEAD_DIM
    base = pos0 + pl.program_id(1) * T
    pos_r = base + lax.broadcasted_iota(jnp.int32, (T, nch), 0)
    n_c = lax.broadcasted_iota(jnp.int32, (T, nch), 1)
    cmask = (n_c * CMP_STRIDE + CMP_LEN - 1) <= pos_r
    j_o = lax.broadcasted_iota(jnp.int32, (nsl, nch), 0)
    st_o = lax.broadcasted_iota(jnp.int32, (nsl, nch), 1) * CMP_STRIDE
    ovl_t = jnp.where((st_o < (j_o + 1) * SLC_BLOCK) & (st_o + CMP_LEN > j_o * SLC_BLOCK), 1.0, 0.0).astype(BF16)
    j_s = lax.broadcasted_iota(jnp.int32, (nsl, T), 0)
    pos_s = base + lax.broadcasted_iota(jnp.int32, (nsl, T), 1)
    cur = pos_s // SLC_BLOCK
    forced = (j_s == 0) | (j_s == cur) | (j_s == cur - 1)
    valid = j_s * SLC_BLOCK <= pos_s
    for g in range(G):
        kcg = kc_ref[g].astype(BF16)
        vcg = vc_ref[g].astype(BF16)
        psum = jnp.zeros((T, nch), F32)
        for r in range(R):
            h = g * R + r
            qh = q_ref[:, h * DH:(h + 1) * DH].astype(BF16)
            s = lax.dot_general(qh, kcg, _NT, preferred_element_type=F32) * SCALE
            sm = jnp.where(cmask, s, NEG)
            e = jnp.exp(sm - jnp.max(sm, axis=-1, keepdims=True))
            p = jnp.where(cmask, e / jnp.sum(e, axis=-1, keepdims=True), 0.0)
            oc_ref[:, h * DH:(h + 1) * DH] = jnp.dot(p.astype(BF16), vcg, preferred_element_type=F32).astype(oc_ref.dtype)
            psum = psum + p
        p_hi, p_lo = _split2(psum)
        imp_t = (lax.dot_general(ovl_t, p_hi, _NT, preferred_element_type=F32)
                 + lax.dot_general(ovl_t, p_lo, _NT, preferred_element_type=F32))
        score = jnp.where(forced, FORCED_SCORE, jnp.where(valid, imp_t, -1.0))
        sc_ref[...] = score
        rank = jnp.zeros((nsl, T), F32)
        for jp in range(nsl):
            row = sc_ref[jp:jp + 1, :]
            beats = (row > score) | ((row == score) & (j_s > jp))
            rank = rank + jnp.where(beats, 1.0, 0.0)
        mem_t = jnp.where(rank < min(N_SELECT, nsl), 1.0, 0.0)
        if nsl < 128:
            mem_t = jnp.concatenate([mem_t, jnp.zeros((128 - nsl, T), F32)], axis=0)
        mem_ref[g] = mem_t.T.astype(mem_ref.dtype)


def cmp_select_prompt(projb, kc, vc, n_batch, *, T=256):
    M = projb.shape[0]
    L = M // n_batch
    nt, nch, nsl = L // T, L // CMP_STRIDE, L // SLC_BLOCK
    return pl.pallas_call(
        functools.partial(_cmp_select_kernel, T=T, nch=nch, nsl=nsl, pos0=0),
        grid=(n_batch, nt),
        in_specs=[pl.BlockSpec((T, TOK_WIDTH), lambda b, i: (b * nt + i, 0)),
                  pl.BlockSpec((None, B_KV_GROUPS, nch, HEAD_DIM), lambda b, i: (b, 0, 0, 0)),
                  pl.BlockSpec((None, B_KV_GROUPS, nch, HEAD_DIM), lambda b, i: (b, 0, 0, 0))],
        out_specs=[pl.BlockSpec((T, TOK_WIDTH), lambda b, i: (b * nt + i, 0)),
                   pl.BlockSpec((None, B_KV_GROUPS, T, 128), lambda b, i: (b, 0, i, 0))],
        out_shape=[jax.ShapeDtypeStruct((M, TOK_WIDTH), BF16),
                   jax.ShapeDtypeStruct((n_batch, B_KV_GROUPS, L, 128), BF16)],
        scratch_shapes=[pltpu.VMEM((nsl, T), F32)],
        compiler_params=_cparams("parallel", "parallel"),
        name="cmp_select_prompt",
    )(projb, kc, vc)


def _sel_win_kernel(q_ref, ks_ref, vs_ref, kw0_ref, kw1_ref, kw2_ref, vw0_ref, vw1_ref, vw2_ref, oc_ref, mem_ref,
                    gt_ref, mix_ref, *, T, CK):
    R, DH = B_GROUP, HEAD_DIM
    g = pl.program_id(1)
    i = pl.program_id(2)
    q3 = jnp.concatenate([q_ref[:, r * DH:(r + 1) * DH] for r in range(R)], axis=0)
    q3 = (q3 * SCALE).astype(BF16)
    qpos = i * T + lax.broadcasted_iota(jnp.int32, (T, CK), 0)
    memb = mem_ref[...]
    j_e = lax.broadcasted_iota(jnp.int32, (128, CK), 0)
    k_e = lax.broadcasted_iota(jnp.int32, (128, CK), 1)
    n_chunks = (i * T + T + CK - 1) // CK

    def chunk(c, carry):
        m, l, acc = carry
        k0 = pl.multiple_of(c * CK, CK)
        kc = ks_ref[pl.ds(k0, CK), :].astype(BF16)
        vc = vs_ref[pl.ds(k0, CK), :].astype(BF16)
        expand = jnp.where((k0 + k_e) // SLC_BLOCK == j_e, 1.0, 0.0).astype(BF16)
        sel = jnp.dot(memb, expand, preferred_element_type=F32) > 0.5
        ok = sel & ((k0 + lax.broadcasted_iota(jnp.int32, (T, CK), 1)) <= qpos)
        bias = jnp.where(ok, 0.0, NEG)
        s = lax.dot_general(q3, kc, _NT, preferred_element_type=F32) + jnp.concatenate([bias] * R, axis=0)
        m_new = jnp.maximum(m, jnp.max(s, axis=-1, keepdims=True))
        alpha = jnp.exp(m - m_new)
        p = jnp.exp(s - m_new)
        l = alpha * l + jnp.sum(p, axis=-1, keepdims=True)
        acc = alpha * acc + jnp.dot(p.astype(BF16), vc, preferred_element_type=F32)
        return m_new, l, acc

    m0 = jnp.full((R * T, 1), NEG, F32)
    _, l, acc = lax.fori_loop(0, n_chunks, chunk, (m0, jnp.zeros((R * T, 1), F32), jnp.zeros((R * T, DH), F32)))
    o_s = acc / l

    kw = jnp.concatenate([kw0_ref[...], kw1_ref[...], kw2_ref[...]], axis=0).astype(BF16)
    vw = jnp.concatenate([vw0_ref[...], vw1_ref[...], vw2_ref[...]], axis=0).astype(BF16)
    qp = i * T + lax.broadcasted_iota(jnp.int32, (T, 3 * T), 0)
    kp = (i - 2) * T + lax.broadcasted_iota(jnp.int32, (T, 3 * T), 1)
    biasw = jnp.where((kp <= qp) & (kp > qp - WINDOW) & (kp >= 0), 0.0, NEG)
    sw = lax.dot_general(q3, kw, _NT, preferred_element_type=F32) + jnp.concatenate([biasw] * R, axis=0)
    ew = jnp.exp(sw - jnp.max(sw, axis=-1, keepdims=True))
    pw = ew / jnp.sum(ew, axis=-1, keepdims=True)
    o_w = jnp.dot(pw.astype(BF16), vw, preferred_element_type=F32)

    gates = jax.nn.sigmoid(gt_ref[...])
    for r in range(R):
        c0 = r * 3
        gsel = _gate_cols(gates, g, c0)
        out = (gsel[0] * oc_ref[:, r * DH:(r + 1) * DH].astype(F32)
               + gsel[1] * o_s[r * T:(r + 1) * T] + gsel[2] * o_w[r * T:(r + 1) * T])
        mix_ref[:, r * DH:(r + 1) * DH] = out.astype(mix_ref.dtype)


def _gate_cols(gates, g, c0):
    lane = lax.broadcasted_iota(jnp.int32, gates.shape, 1)
    tgt = g * (B_GROUP * 3) + c0
    return [jnp.sum(jnp.where(lane == tgt + k, gates, 0.0), axis=-1, keepdims=True) for k in range(3)]


def sel_win_combine(projb, kvp, oc, mem, n_batch, *, T=256, CK=512):
    M = projb.shape[0]
    L = M // n_batch
    nt = L // T
    GW = B_GROUP * HEAD_DIM
    kcol = lambda base: (lambda b, g, i: (b, base + g))
    wspec = lambda base, d: pl.BlockSpec((T, HEAD_DIM), lambda b, g, i: (b * nt + jnp.maximum(i - d, 0), base + g))
    return pl.pallas_call(
        functools.partial(_sel_win_kernel, T=T, CK=CK),
        grid=(n_batch, B_KV_GROUPS, nt),
        in_specs=[pl.BlockSpec((T, GW), lambda b, g, i: (b * nt + i, g)),
                  pl.BlockSpec((L, HEAD_DIM), kcol(8)),
                  pl.BlockSpec((L, HEAD_DIM), kcol(12)),
                  wspec(16, 2), wspec(16, 1), wspec(16, 0),
                  wspec(20, 2), wspec(20, 1), wspec(20, 0),
                  pl.BlockSpec((T, GW), lambda b, g, i: (b * nt + i, g)),
                  pl.BlockSpec((None, None, T, 128), lambda b, g, i: (b, g, i, 0)),
                  pl.BlockSpec((T, 128), lambda b, g, i: (b * nt + i, 16))],
        out_specs=pl.BlockSpec((T, GW), lambda b, g, i: (b * nt + i, g)),
        out_shape=jax.ShapeDtypeStruct((M, TOK_WIDTH), BF16),
        compiler_params=_cparams("parallel", "parallel", "arbitrary"),
        name="sel_win_combine",
    )(projb, kvp, kvp, kvp, kvp, kvp, kvp, kvp, kvp, oc, mem, projb)


def _mem_attn_kernel(q_ref, k_ref, v_ref, o_ref):
    DH = HEAD_DIM
    for h in range(MEM_HEADS):
        qh = (q_ref[:, h * DH:(h + 1) * DH] * SCALE).astype(BF16)
        kh = k_ref[:, h * DH:(h + 1) * DH].astype(BF16)
        vh = v_ref[:, h * DH:(h + 1) * DH].astype(BF16)
        s = lax.dot_general(qh, kh, _NT, preferred_element_type=F32)
        e = jnp.exp(s - jnp.max(s, axis=-1, keepdims=True))
        p = e / jnp.sum(e, axis=-1, keepdims=True)
        o_ref[:, h * DH:(h + 1) * DH] = jnp.dot(p.astype(BF16), vh, preferred_element_type=F32).astype(o_ref.dtype)


def mem_attn_prompt(proj, q_block, memkv, n_batch, *, T=512):
    M = proj.shape[0]
    nt = M // n_batch // T
    return pl.pallas_call(
        _mem_attn_kernel,
        grid=(n_batch, nt),
        in_specs=[pl.BlockSpec((T, MEM_WIDTH), lambda b, i: (b * nt + i, q_block)),
                  pl.BlockSpec((MEM_TOKENS, MEM_WIDTH), lambda b, i: (b, 0)),
                  pl.BlockSpec((MEM_TOKENS, MEM_WIDTH), lambda b, i: (b, 1))],
        out_specs=pl.BlockSpec((T, MEM_WIDTH), lambda b, i: (b * nt + i, 0)),
        out_shape=jax.ShapeDtypeStruct((M, MEM_WIDTH), BF16),
        compiler_params=_cparams("parallel", "parallel"),
        name="mem_attn_prompt",
    )(proj, memkv, memkv)


def _log_sigmoid(x):
    return jnp.minimum(x, 0.0) - jnp.log1p(jnp.exp(-jnp.abs(x)))


def _mlstm_kernel(q_ref, k_ref, v_ref, o_ref, gt_ref, gb_ref, hg_ref, mix_ref, c_ref, n_ref, m_ref):
    H, DH, CL = A_HEADS, HEAD_DIM, MLSTM_CHUNK

    @pl.when(pl.program_id(1) == 0)
    def _():
        c_ref[...] = jnp.zeros_like(c_ref)
        n_ref[...] = jnp.zeros_like(n_ref)
        m_ref[...] = jnp.zeros_like(m_ref)

    gt = (gt_ref[...] + gb_ref[...]).T
    ig = gt[0:16]
    lf = _log_sigmoid(gt[16:32])
    s_r = lax.broadcasted_iota(jnp.int32, (CL, CL), 0)
    s_c = lax.broadcasted_iota(jnp.int32, (CL, CL), 1)
    upper = jnp.where(s_r <= s_c, 1.0, 0.0).astype(BF16)
    lf_hi = lf.astype(BF16)
    r1 = lf - lf_hi.astype(F32)
    lf_mid = r1.astype(BF16)
    lf_lo = (r1 - lf_mid.astype(F32)).astype(BF16)
    b_rows = (jnp.dot(lf_hi, upper, preferred_element_type=F32) + jnp.dot(lf_mid, upper, preferred_element_type=F32)
              + jnp.dot(lf_lo, upper, preferred_element_type=F32))
    u_rows = ig - b_rows
    bl_all = jnp.sum(lf, axis=-1, keepdims=True)
    tril = s_c <= s_r
    for h in range(H):
        b_row = b_rows[h:h + 1]
        u_row = u_rows[h:h + 1]
        bl = bl_all[h:h + 1]
        m_h = m_ref[h:h + 1, :]
        b_col = jnp.broadcast_to(b_row, (CL, CL)).T
        dlog = jnp.where(tril, b_col + u_row, -jnp.inf)
        inter = b_col[:, 0:1] + m_h[:, 0:1]
        mt = jnp.maximum(inter, jnp.max(dlog, axis=-1, keepdims=True))
        a = jnp.exp(inter - mt)
        qh = q_ref[:, h * DH:(h + 1) * DH]
        kh = k_ref[:, h * DH:(h + 1) * DH] * SCALE
        qb, kb, vb = qh.astype(BF16), kh.astype(BF16), v_ref[:, h * DH:(h + 1) * DH].astype(BF16)
        w = jnp.exp(dlog - mt) * lax.dot_general(qb, kb, _NT, preferred_element_type=F32)
        c_old = c_ref[h]
        num = a * jnp.dot(qb, c_old.astype(BF16), preferred_element_type=F32) + jnp.dot(w.astype(BF16), vb, preferred_element_type=F32)
        n_old = n_ref[h:h + 1, :]
        den = a * jnp.sum(qh * n_old, axis=-1, keepdims=True) + jnp.sum(w, axis=-1, keepdims=True)
        hval = num / jnp.maximum(jnp.abs(den), jnp.exp(-mt))
        g_row = bl + u_row
        m_new = jnp.maximum(bl + m_h, jnp.max(g_row, axis=-1, keepdims=True))
        decay = jnp.exp(bl + m_h - m_new)
        wk_row = jnp.exp(g_row - m_new)
        k_t = kh.T
        c_ref[h] = decay[:, 0:1] * c_old + jnp.dot((k_t * wk_row).astype(BF16), vb, preferred_element_type=F32)
        wk8 = jnp.broadcast_to(wk_row, (8, CL)).astype(BF16)
        n_ref[h:h + 1, :] = decay * n_old + jnp.dot(wk8, kb, preferred_element_type=F32)[0:1]
        m_ref[h:h + 1, :] = m_new
        hn = hval * lax.rsqrt(jnp.mean(hval * hval, axis=-1, keepdims=True) + EPS) * hg_ref[:, h * DH:(h + 1) * DH]
        mix_ref[:, h * DH:(h + 1) * DH] = (jax.nn.sigmoid(o_ref[:, h * DH:(h + 1) * DH]) * hn).astype(mix_ref.dtype)


def mlstm_prompt(proja, gate_bias, head_gain, n_batch):
    M = proja.shape[0]
    nc = M // n_batch // MLSTM_CHUNK
    blk = lambda j: pl.BlockSpec((MLSTM_CHUNK, TOK_WIDTH), lambda b, c: (b * nc + c, j))
    return pl.pallas_call(
        _mlstm_kernel,
        grid=(n_batch, nc),
        in_specs=[blk(0), blk(1), blk(2), blk(3),
                  pl.BlockSpec((MLSTM_CHUNK, 128), lambda b, c: (b * nc + c, (4 * TOK_WIDTH + MEM_WIDTH) // 128)),
                  pl.BlockSpec((1, 128), lambda b, c: (0, 0)),
                  pl.BlockSpec((1, TOK_WIDTH), lambda b, c: (0, 0))],
        out_specs=[pl.BlockSpec((MLSTM_CHUNK, TOK_WIDTH), lambda b, c: (b * nc + c, 0)),
                   pl.BlockSpec((None, A_HEADS, HEAD_DIM, HEAD_DIM), lambda b, c: (b, 0, 0, 0)),
                   pl.BlockSpec((None, 16, HEAD_DIM), lambda b, c: (b, 0, 0)),
                   pl.BlockSpec((None, 16, 128), lambda b, c: (b, 0, 0))],
        out_shape=[jax.ShapeDtypeStruct((M, TOK_WIDTH), BF16),
                   jax.ShapeDtypeStruct((n_batch, A_HEADS, HEAD_DIM, HEAD_DIM), F32),
                   jax.ShapeDtypeStruct((n_batch, 16, HEAD_DIM), F32),
                   jax.ShapeDtypeStruct((n_batch, 16, 128), F32)],
        compiler_params=_cparams("parallel", "arbitrary"),
        name="mlstm_prompt",
    )(proja, proja, proja, proja, proja, gate_bias, head_gain.reshape(1, TOK_WIDTH))


def _j_masked_probs(s, mask):
    p = jax.nn.softmax(jnp.where(mask, s, NEG), axis=-1)
    return jnp.where(mask, p, 0.0)


def _j_mem_attend(q, k, v):
    s = jnp.einsum('blhd,bmhd->bhlm', q, k).astype(F32) * SCALE
    p = jax.nn.softmax(s, axis=-1).astype(v.dtype)
    return jnp.einsum('bhlm,bmhd->blhd', p, v)


def _j_head_rmsnorm(h, g):
    h = h * lax.rsqrt(jnp.mean(h * h, axis=-1, keepdims=True) + EPS)
    return h * g.reshape(h.shape[-2:]).astype(F32)


def _j_mlstm(q, k, v, ig, lf, C0, n0, m0):
    B, L, H, D = q.shape
    c = min(MLSTM_CHUNK, L)
    nc = -(-L // c)
    pad = nc * c - L

    def blocks(a, fill):
        a = jnp.pad(a.astype(F32), [(0, 0), (0, pad)] + [(0, 0)] * (a.ndim - 2), constant_values=fill)
        a = a.reshape((B, nc, c) + a.shape[2:])
        return jnp.swapaxes(jnp.moveaxis(a, 1, 0), 2, 3)

    xs = (blocks(q, 0.0), blocks(k, 0.0), blocks(v, 0.0), blocks(ig, NEG), blocks(lf, 0.0))
    tril = jnp.tril(jnp.ones((c, c), dtype=bool))

    def step(carry, xs_c):
        C, n, m = carry
        qc, kc, vc, ic, fc = xs_c
        b = jnp.cumsum(fc, axis=-1)
        dlog = jnp.where(tril, b[..., :, None] - b[..., None, :] + ic[..., None, :], -jnp.inf)
        inter = b + m[..., None]
        mt = jnp.maximum(inter, jnp.max(dlog, axis=-1))
        a = jnp.exp(inter - mt)
        w = jnp.exp(dlog - mt[..., None]) * jnp.einsum('bhtd,bhsd->bhts', qc, kc)
        num = a[..., None] * jnp.einsum('bhtd,bhde->bhte', qc, C) + jnp.einsum('bhts,bhse->bhte', w, vc)
        den = a * jnp.einsum('bhtd,bhd->bht', qc, n) + jnp.sum(w, axis=-1)
        h = num / jnp.maximum(jnp.abs(den), jnp.exp(-mt))[..., None]
        bl = b[..., -1]
        g = bl[..., None] - b + ic
        m_new = jnp.maximum(bl + m, jnp.max(g, axis=-1))
        decay = jnp.exp(bl + m - m_new)
        wk = jnp.exp(g - m_new[..., None])
        C_new = decay[..., None, None] * C + jnp.einsum('bhs,bhsd,bhse->bhde', wk, kc, vc)
        n_new = decay[..., None] * n + jnp.einsum('bhs,bhsd->bhd', wk, kc)
        return (C_new, n_new, m_new), h

    (Cf, nf, mf), hs = lax.scan(step, (C0.astype(F32), n0.astype(F32), m0.astype(F32)), xs)
    hs = jnp.moveaxis(jnp.swapaxes(hs, 2, 3), 0, 1).reshape(B, nc * c, H, D)[:, :L]
    return hs, (Cf, nf, mf)


def _j_compress(x, pe, w1, w2):
    B, T, G, D = x.shape
    nch = T // CMP_STRIDE
    xs = x[:, :nch * CMP_STRIDE].reshape(B, nch, CMP_STRIDE, G, D)
    pe2 = pe.reshape(2, CMP_STRIDE, D)
    w12 = w1.reshape(2, CMP_STRIDE, D, -1)
    first = jnp.einsum('bncgd,cde->bnge', xs + pe2[0][:, None, :], w12[0])
    second = jnp.einsum('bncgd,cde->bnge', xs + pe2[1][:, None, :], w12[1])
    hid = jax.nn.gelu(first[:, :-1] + second[:, 1:])
    return hid @ w2


def _j_cmp_select(q, pos, kc, vc, n_slc):
    s = jnp.einsum('blgrd,bngd->bgrln', q, kc).astype(F32) * SCALE
    n_cmp = kc.shape[1]
    start = jnp.arange(n_cmp, dtype=jnp.int32) * CMP_STRIDE
    mask = (start + CMP_LEN - 1)[None, :] <= pos[:, None]
    p = _j_masked_probs(s, mask)
    o_c = jnp.einsum('bgrln,bngd->blgrd', p.astype(vc.dtype), vc)
    j = jnp.arange(n_slc, dtype=jnp.int32)
    overlap = ((start[:, None] < (j[None, :] + 1) * SLC_BLOCK)
               & (start[:, None] + CMP_LEN > j[None, :] * SLC_BLOCK)).astype(F32)
    imp = jnp.einsum('bgrln,nj->bglj', p, overlap)
    cur = (pos // SLC_BLOCK)[:, None]
    forced = (j[None, :] == 0) | (j[None, :] == cur) | (j[None, :] == cur - 1)
    valid = j[None, :] * SLC_BLOCK <= pos[:, None]
    score = jnp.where(forced, FORCED_SCORE, jnp.where(valid, imp, -1.0))
    _, idx = lax.top_k(score, min(N_SELECT, n_slc))
    return o_c, jnp.transpose(idx, (0, 2, 1, 3))


def _j_sel_attend(q, pos, idx, gather):
    kb, vb = gather(idx)
    B, L, G, K, C, D = kb.shape
    R = q.shape[3]
    s = jnp.einsum('blgrd,blgkcd->blgrkc', q, kb).astype(F32) * SCALE
    kpos = idx[..., None] * SLC_BLOCK + jnp.arange(SLC_BLOCK, dtype=jnp.int32)
    mask = (kpos <= pos[None, :, None, None, None]).reshape(B, L, G, 1, K * C)
    p = _j_masked_probs(s.reshape(B, L, G, R, K * C), mask).astype(vb.dtype)
    return jnp.einsum('blgrn,blgnd->blgrd', p, vb.reshape(B, L, G, K * C, D))


def _j_sel_attend_blocked(q, pos, idx, gather):
    B, L = q.shape[:2]
    qb = SEL_QBLK if L % SEL_QBLK == 0 else L
    nb = L // qb
    if nb == 1:
        return _j_sel_attend(q, pos, idx, gather)
    qs = jnp.swapaxes(q.reshape((B, nb, qb) + q.shape[2:]), 0, 1)
    ps = pos.reshape(nb, qb)
    ids = jnp.swapaxes(idx.reshape((B, nb, qb) + idx.shape[2:]), 0, 1)
    out = lax.map(lambda a: _j_sel_attend(a[0], a[1], a[2], gather), (qs, ps, ids))
    return jnp.swapaxes(out, 0, 1).reshape((B, L) + out.shape[3:])


def _j_window_prompt(q, kw, vw):
    B, S, G, R, D = q.shape
    nb = S // WIN_QBLK
    span = WINDOW + WIN_QBLK
    padw = [(0, 0), (WINDOW, 0), (0, 0), (0, 0)]
    kp, vp = jnp.pad(kw, padw), jnp.pad(vw, padw)
    idx = jnp.arange(nb, dtype=jnp.int32)[:, None] * WIN_QBLK + jnp.arange(span, dtype=jnp.int32)[None, :]
    kb, vb = kp[:, idx], vp[:, idx]
    qb = q.reshape(B, nb, WIN_QBLK, G, R, D)
    s = jnp.einsum('bnqgrd,bnkgd->bngrqk', qb, kb).astype(F32) * SCALE
    kpos = idx - WINDOW
    qpos = jnp.arange(nb, dtype=jnp.int32)[:, None] * WIN_QBLK + jnp.arange(WIN_QBLK, dtype=jnp.int32)[None, :]
    mask = ((kpos[:, None, :] <= qpos[:, :, None]) & (kpos[:, None, :] > qpos[:, :, None] - WINDOW)
            & (kpos[:, None, :] >= 0))
    p = _j_masked_probs(s, mask[None, :, None, None]).astype(vb.dtype)
    return jnp.einsum('bngrqk,bnkgd->bnqgrd', p, vb).reshape(B, S, G, R, D)


def _j_window_sample(q, pos, kw_new, vw_new, kw_buf, vw_buf):
    w_buf = kw_buf.shape[1]
    kk = jnp.concatenate([kw_buf, kw_new], axis=1)
    vv = jnp.concatenate([vw_buf, vw_new], axis=1)
    kpos = PAST_LEN - w_buf + jnp.arange(kk.shape[1], dtype=jnp.int32)
    s = jnp.einsum('blgrd,bkgd->bgrlk', q, kk).astype(F32) * SCALE
    mask = (kpos[None, :] <= pos[:, None]) & (kpos[None, :] > pos[:, None] - WINDOW)
    p = _j_masked_probs(s, mask).astype(vv.dtype)
    return jnp.einsum('bgrlk,bkgd->blgrd', p, vv)


def _j_combine(gates, o_c, o_s, o_w):
    out = gates[..., 0:1] * o_c + gates[..., 1:2] * o_s + gates[..., 2:3] * o_w
    return out.reshape(out.shape[:2] + (-1,))


def _j_nsa_prompt(q, gates, kv, pe_k, w1_k, w2_k, pe_v, w1_v, w2_v):
    kc_raw, vc_raw, ks, vs, kw, vw = kv
    B, L, G = ks.shape[:3]
    pos = jnp.arange(L, dtype=jnp.int32)
    kc = _j_compress(kc_raw, pe_k, w1_k, w2_k)
    vc = _j_compress(vc_raw, pe_v, w1_v, w2_v)
    n_slc = -(-L // SLC_BLOCK)
    o_c, idx = _j_cmp_select(q, pos, kc, vc, n_slc)
    ksb = ks.reshape(B, n_slc, SLC_BLOCK, G, HEAD_DIM)
    vsb = vs.reshape(B, n_slc, SLC_BLOCK, G, HEAD_DIM)
    b_i = jnp.arange(B)[:, None, None, None]
    g_i = jnp.arange(G)[None, None, :, None]
    gather = lambda ix: (ksb[b_i, ix, :, g_i, :], vsb[b_i, ix, :, g_i, :])
    o_s = _j_sel_attend_blocked(q, pos, idx, gather)
    o_w = _j_window_prompt(q, kw, vw)
    return _j_combine(gates, o_c, o_s, o_w)


def _j_nsa_sample(q, gates, kv, cache_cmp_k, cache_cmp_v, cache_slc_k, cache_slc_v, cache_win_k, cache_win_v,
                  page_table, pe_k, w1_k, w2_k, pe_v, w1_v, w2_v):
    kc_new, vc_new, ks_new, vs_new, kw_new, vw_new = kv
    B, L, G = ks_new.shape[:3]
    n_pages = PAST_LEN // PAGE_SIZE
    pos = PAST_LEN + jnp.arange(L, dtype=jnp.int32)

    def paged_rows(pool):
        return pool[page_table].reshape(B, n_pages * PAGE_SIZE, G, HEAD_DIM)

    kc = _j_compress(jnp.concatenate([paged_rows(cache_cmp_k), kc_new], axis=1), pe_k, w1_k, w2_k)
    vc = _j_compress(jnp.concatenate([paged_rows(cache_cmp_v), vc_new], axis=1), pe_v, w1_v, w2_v)
    n_slc = -(-(PAST_LEN + L) // SLC_BLOCK)
    o_c, idx = _j_cmp_select(q, pos, kc, vc, n_slc)
    n_pb = PAST_LEN // SLC_BLOCK
    n_tail = n_slc - n_pb
    bpp = PAGE_SIZE // SLC_BLOCK

    def tail_blocks(x):
        x = jnp.pad(x, ((0, 0), (0, n_tail * SLC_BLOCK - L), (0, 0), (0, 0)))
        return x.reshape(B, n_tail, SLC_BLOCK, G, HEAD_DIM)

    ks_tail, vs_tail = tail_blocks(ks_new), tail_blocks(vs_new)
    ks_pool = cache_slc_k.reshape(-1, bpp, SLC_BLOCK, G, HEAD_DIM)
    vs_pool = cache_slc_v.reshape(-1, bpp, SLC_BLOCK, G, HEAD_DIM)
    b_i = jnp.arange(B)[:, None, None, None]
    g_i = jnp.arange(G)[None, None, :, None]

    def gather(ix):
        in_past = (ix < n_pb)[..., None, None]
        phys = page_table[b_i, jnp.clip(ix // bpp, 0, n_pages - 1)]
        sub = ix % bpp
        jt = jnp.clip(ix - n_pb, 0, n_tail - 1)
        pick = lambda pool, tail: jnp.where(in_past, pool[phys, sub, :, g_i, :], tail[b_i, jt, :, g_i, :])
        return pick(ks_pool, ks_tail), pick(vs_pool, vs_tail)

    o_s = _j_sel_attend_blocked(q, pos, idx, gather)
    o_w = _j_window_sample(q, pos, kw_new, vw_new, cache_win_k, cache_win_v)
    return _j_combine(gates, o_c, o_s, o_w)


def kernel(x_prompt, x_sample, mem_prompt, cache_mem_k, cache_mem_v, state_mlstm_C, state_mlstm_n, state_mlstm_m, cache_cmp_k, cache_cmp_v, cache_slc_k, cache_slc_v, cache_win_k, cache_win_v, page_table, ffn1_norm, ffn1_w_in, ffn1_w_out, ffn2_norm, ffn2_w_in, ffn2_w_out, mix_norm, a_w_in, a_b_i, a_b_f, a_head_norm, b_w_in, mem_norm, w_mem_kv, w_out, kv_norm, w_kv, cmp_pe_k, cmp_w1_k, cmp_w2_k, cmp_pe_v, cmp_w1_v, cmp_w2_v, final_norm):
    G, R, DH = B_KV_GROUPS, B_GROUP, HEAD_DIM
    split_a = [TOK_WIDTH, 2 * TOK_WIDTH, 3 * TOK_WIDTH, 4 * TOK_WIDTH, 4 * TOK_WIDTH + A_HEADS,
               4 * TOK_WIDTH + 2 * A_HEADS]
    split_b = [TOK_WIDTH, TOK_WIDTH + 3 * B_HEADS]

    def pad_cols(w, mult):
        n = w.shape[-1]
        return jnp.pad(w, ((0, 0), (0, -n % mult)))

    ffn1_wi, ffn1_wo = ffn1_w_in.astype(BF16), ffn1_w_out.astype(BF16)
    ffn2_wi, ffn2_wo = ffn2_w_in.astype(BF16), ffn2_w_out.astype(BF16)
    a_w = pad_cols(a_w_in[0], 256).astype(BF16)
    b_w = pad_cols(b_w_in[0], 256).astype(BF16)
    w_o = w_out.astype(BF16)
    w_kv_b = w_kv.astype(BF16)
    w_mkv = w_mem_kv.astype(BF16)

    def trunk(x, mem_k, mem_v, C0, n0, m0, nsa_mixer, tm):
        Bn, L, _ = x.shape
        M = Bn * L
        x = x.reshape(M, D_MODEL)
        Cs, ns, ms = [], [], []
        kv = None
        for l in range(DEPTH):
            x = ffn(x, ffn1_norm[l], ffn1_wi[l], ffn1_wo[l], tm=tm, tf=512)
            if l < N_A_LAYERS:
                proj = norm_matmul(x, mix_norm[l], a_w, tm=tm, tn=256)[:, :a_w_in.shape[-1]].reshape(Bn, L, -1)
                q, k, v, o, ig, fg, qm = jnp.split(proj, split_a, axis=-1)
                hh, (C, n, m) = _j_mlstm(
                    q.reshape(Bn, L, A_HEADS, DH), k.reshape(Bn, L, A_HEADS, DH) * SCALE,
                    v.reshape(Bn, L, A_HEADS, DH), ig.astype(F32) + a_b_i[l],
                    jax.nn.log_sigmoid(fg.astype(F32) + a_b_f[l]), C0[l], n0[l], m0[l])
                gate_o = jax.nn.sigmoid(o.astype(F32)).reshape(Bn, L, A_HEADS, DH)
                mix = (gate_o * _j_head_rmsnorm(hh, a_head_norm[l])).reshape(Bn, L, TOK_WIDTH)
                Cs.append(C)
                ns.append(n)
                ms.append(m)
            else:
                proj = norm_matmul(x, mix_norm[l], b_w, tm=tm, tn=256)[:, :b_w_in.shape[-1]].reshape(Bn, L, -1)
                q, gts, qm = jnp.split(proj, split_b, axis=-1)
                gates = jax.nn.sigmoid(gts.astype(F32)).reshape(Bn, L, G, R, 3)
                mix = nsa_mixer(q.reshape(Bn, L, G, R, DH), gates, kv)
            mo = _j_mem_attend(qm.reshape(Bn, L, MEM_HEADS, DH), mem_k[l], mem_v[l]).reshape(Bn, L, MEM_WIDTH)
            x = outproj(x, mix.reshape(M, TOK_WIDTH), mo.reshape(M, MEM_WIDTH), w_o[l], tm=tm)
            last = l == DEPTH - 1
            x = ffn(x, ffn2_norm[l], ffn2_wi[l], ffn2_wo[l], final_norm if last else None, tm=tm, tf=512)
            if l == N_A_LAYERS - 1:
                kvp = norm_matmul(x, kv_norm, w_kv_b, tm=tm, tn=512).reshape(Bn, L, 6, G, DH)
                kv = tuple(kvp[:, :, i] for i in range(6))
        return x.reshape(Bn, L, D_MODEL), jnp.stack(Cs), jnp.stack(ns), jnp.stack(ms), kv

    wa = a_w_in[0]
    zc = lambda n: jnp.zeros((D_MODEL, n), F32)
    a_wr = jnp.concatenate([wa[:, :4 * TOK_WIDTH], wa[:, 4 * TOK_WIDTH + 2 * A_HEADS:],
                            wa[:, 4 * TOK_WIDTH:4 * TOK_WIDTH + A_HEADS], zc(16 - A_HEADS),
                            wa[:, 4 * TOK_WIDTH + A_HEADS:4 * TOK_WIDTH + 2 * A_HEADS], zc(128 - 16 - A_HEADS),
                            zc(128)], axis=1).astype(BF16)
    gate_bias = jnp.concatenate([a_b_i[0], jnp.zeros((16 - A_HEADS,), F32), a_b_f[0],
                                 jnp.zeros((128 - 16 - A_HEADS,), F32)]).reshape(1, 128)
    wb = b_w_in[0]
    b_wr = jnp.concatenate([wb[:, :TOK_WIDTH], wb[:, TOK_WIDTH + 3 * B_HEADS:],
                            wb[:, TOK_WIDTH:TOK_WIDTH + 3 * B_HEADS], zc(256 - 3 * B_HEADS)], axis=1).astype(BF16)

    Bp, S = x_prompt.shape[:2]
    Mp = Bp * S
    memx = mem_prompt.reshape(Bp * MEM_TOKENS, D_MODEL)
    memkv = [norm_matmul(memx, mem_norm[l], w_mkv[l], tm=512, tn=512) for l in range(DEPTH)]
    mem5 = jnp.stack(memkv).reshape(DEPTH, Bp, MEM_TOKENS, 2, MEM_HEADS, DH)
    mem_k_p, mem_v_p = mem5[:, :, :, 0], mem5[:, :, :, 1]

    x = x_prompt.reshape(Mp, D_MODEL)
    x = ffn(x, ffn1_norm[0], ffn1_wi[0], ffn1_wo[0], tm=512, tf=512)
    proja = norm_matmul(x, mix_norm[0], a_wr, tm=512, tn=256)
    mix, C_fin, n_fin, m_fin = mlstm_prompt(proja, gate_bias, a_head_norm[0], Bp)
    mo = mem_attn_prompt(proja, 4 * TOK_WIDTH // MEM_WIDTH, memkv[0], Bp)
    x = outproj(x, mix, mo, w_o[0], tm=512)
    x = ffn(x, ffn2_norm[0], ffn2_wi[0], ffn2_wo[0], tm=512, tf=512)
    kvp = norm_matmul(x, kv_norm, w_kv_b, tm=512, tn=512)
    x = ffn(x, ffn1_norm[1], ffn1_wi[1], ffn1_wo[1], tm=512, tf=512)
    projb = norm_matmul(x, mix_norm[1], b_wr, tm=512, tn=256)
    kc = compress_prompt(kvp, 0, Bp, cmp_pe_k, cmp_w1_k, cmp_w2_k)
    vc = compress_prompt(kvp, 1, Bp, cmp_pe_v, cmp_w1_v, cmp_w2_v)
    oc, member = cmp_select_prompt(projb, kc, vc, Bp)
    mix = sel_win_combine(projb, kvp, oc, member, Bp)
    mo = mem_attn_prompt(projb, TOK_WIDTH // MEM_WIDTH, memkv[1], Bp)
    x = outproj(x, mix, mo, w_o[1], tm=512)
    y_prompt = ffn(x, ffn2_norm[1], ffn2_wi[1], ffn2_wo[1], final_norm, tm=512, tf=512).reshape(Bp, S, D_MODEL)
    C_p = C_fin[None]
    n_p = n_fin[None, :, :A_HEADS]
    m_p = m_fin[None, :, :A_HEADS, 0]
    kvp5 = kvp.reshape(Bp, S, 6, G, DH)
    kv_p = tuple(kvp5[:, :, i] for i in range(6))

    sample_mixer = lambda q, g, kv: _j_nsa_sample(q, g, kv, cache_cmp_k, cache_cmp_v, cache_slc_k, cache_slc_v,
                                                  cache_win_k, cache_win_v, page_table, cmp_pe_k, cmp_w1_k,
                                                  cmp_w2_k, cmp_pe_v, cmp_w1_v, cmp_w2_v)
    y_sample, C_s, n_s, m_s, kv_s = trunk(x_sample, cache_mem_k, cache_mem_v, state_mlstm_C, state_mlstm_n,
                                          state_mlstm_m, sample_mixer, 128)

    w_p = min(WINDOW, S)
    cmp_k_p, cmp_v_p, slc_k_p, slc_v_p, win_k_all, win_v_all = kv_p
    win_k_p = win_k_all[:, S - w_p:]
    win_v_p = win_v_all[:, S - w_p:]
    cmp_k_s, cmp_v_s, slc_k_s, slc_v_s, win_k_s, win_v_s = kv_s
    return (y_prompt, y_sample, mem_k_p, mem_v_p, C_p, n_p, m_p, C_s, n_s, m_s,
            cmp_k_p, cmp_v_p, slc_k_p, slc_v_p, win_k_p, win_v_p,
            cmp_k_s, cmp_v_s, slc_k_s, slc_v_s, win_k_s, win_v_s)
```

```python
import functools

import jax
import jax.numpy as jnp
from jax import lax
from jax.experimental import pallas as pl
from jax.experimental.pallas import tpu as pltpu

D_MODEL = 2048
DEPTH = 2
PAST_LEN = 2048
PAGE_SIZE = 128
HEAD_DIM = 128
N_A_LAYERS = DEPTH // 2
MEM_TOKENS = 256
MEM_HEADS = 4
MEM_WIDTH = MEM_HEADS * HEAD_DIM
TOK_WIDTH = D_MODEL - MEM_WIDTH
A_HEADS = TOK_WIDTH // HEAD_DIM
B_HEADS = TOK_WIDTH // HEAD_DIM
B_KV_GROUPS = 4
B_GROUP = B_HEADS // B_KV_GROUPS
D_FF = ((8 * D_MODEL // 3 + 255) // 256) * 256
MLSTM_CHUNK = 128
CMP_STRIDE = 16
CMP_LEN = 2 * CMP_STRIDE
SLC_BLOCK = 64
N_SELECT = 16
WINDOW = 512
WIN_QBLK = 128
SEL_QBLK = 64
FORCED_SCORE = 1e4
NEG = -1e30
EPS = 1e-6
SCALE = HEAD_DIM ** -0.5

F32 = jnp.float32
BF16 = jnp.bfloat16
VMEM_LIMIT = 56 * 1024 * 1024


def _cparams(*sem):
    return pltpu.CompilerParams(dimension_semantics=sem, vmem_limit_bytes=VMEM_LIMIT)


def _rms_bf16(x, g):
    ms = jnp.mean(x * x, axis=-1, keepdims=True)
    return (x * lax.rsqrt(ms + EPS) * g).astype(BF16)


def _norm_matmul_kernel(x_ref, g_ref, w_ref, o_ref, h_ref):
    @pl.when(pl.program_id(1) == 0)
    def _():
        h_ref[...] = _rms_bf16(x_ref[...], g_ref[...])

    o_ref[...] = jnp.dot(h_ref[...], w_ref[...], preferred_element_type=F32).astype(o_ref.dtype)


def norm_matmul(x, g, w, *, tm, tn, out_dtype=F32):
    M, D = x.shape
    N = w.shape[1]
    return pl.pallas_call(
        _norm_matmul_kernel,
        grid=(M // tm, N // tn),
        in_specs=[pl.BlockSpec((tm, D), lambda i, j: (i, 0)),
                  pl.BlockSpec((1, D), lambda i, j: (0, 0)),
                  pl.BlockSpec((D, tn), lambda i, j: (0, j))],
        out_specs=pl.BlockSpec((tm, tn), lambda i, j: (i, j)),
        out_shape=jax.ShapeDtypeStruct((M, N), out_dtype),
        scratch_shapes=[pltpu.VMEM((tm, D), BF16)],
        compiler_params=_cparams("parallel", "arbitrary"),
        name="norm_matmul",
    )(x, g.reshape(1, D), w)


def _ffn_kernel(x_ref, g_ref, wg_ref, wu_ref, wo_ref, fg_ref, o_ref, h_ref, acc_ref, *, final_norm):
    f = pl.program_id(1)

    @pl.when(f == 0)
    def _():
        h_ref[...] = _rms_bf16(x_ref[...], g_ref[...])
        acc_ref[...] = jnp.zeros_like(acc_ref)

    h = h_ref[...]
    gate = jnp.dot(h, wg_ref[...], preferred_element_type=F32)
    up = jnp.dot(h, wu_ref[...], preferred_element_type=F32)
    act = (gate * jax.nn.sigmoid(gate) * up).astype(BF16)
    acc_ref[...] += jnp.dot(act, wo_ref[...], preferred_element_type=F32)

    @pl.when(f == pl.num_programs(1) - 1)
    def _():
        y = x_ref[...] + 0.5 * acc_ref[...]
        if final_norm:
            ms = jnp.mean(y * y, axis=-1, keepdims=True)
            y = y * lax.rsqrt(ms + EPS) * fg_ref[...]
        o_ref[...] = y


def ffn(x, g, w_in, w_out, final_gain=None, *, tm, tf):
    M, D = x.shape
    F = w_out.shape[0]
    nf = F // tf
    fg = jnp.ones((D,), F32) if final_gain is None else final_gain
    return pl.pallas_call(
        functools.partial(_ffn_kernel, final_norm=final_gain is not None),
        grid=(M // tm, nf),
        in_specs=[pl.BlockSpec((tm, D), lambda i, f: (i, 0)),
                  pl.BlockSpec((1, D), lambda i, f: (0, 0)),
                  pl.BlockSpec((D, tf), lambda i, f: (0, f)),
                  pl.BlockSpec((D, tf), lambda i, f: (0, f + nf)),
                  pl.BlockSpec((tf, D), lambda i, f: (f, 0)),
                  pl.BlockSpec((1, D), lambda i, f: (0, 0))],
        out_specs=pl.BlockSpec((tm, D), lambda i, f: (i, 0)),
        out_shape=jax.ShapeDtypeStruct((M, D), F32),
        scratch_shapes=[pltpu.VMEM((tm, D), BF16), pltpu.VMEM((tm, D), F32)],
        compiler_params=_cparams("parallel", "arbitrary"),
        name="ffn",
    )(x, g.reshape(1, D), w_in, w_in, w_out, fg.reshape(1, D))


def _outproj_kernel(x_ref, mix_ref, mo_ref, wa_ref, wb_ref, o_ref):
    y = jnp.dot(mix_ref[...].astype(BF16), wa_ref[...], preferred_element_type=F32)
    y += jnp.dot(mo_ref[...].astype(BF16), wb_ref[...], preferred_element_type=F32)
    o_ref[...] = x_ref[...] + y


def outproj(x, mix, mo, w, *, tm):
    M, D = x.shape
    Ka, Kb = mix.shape[1], mo.shape[1]
    return pl.pallas_call(
        _outproj_kernel,
        grid=(M // tm,),
        in_specs=[pl.BlockSpec((tm, D), lambda i: (i, 0)),
                  pl.BlockSpec((tm, Ka), lambda i: (i, 0)),
                  pl.BlockSpec((tm, Kb), lambda i: (i, 0)),
                  pl.BlockSpec((Ka, D), lambda i: (0, 0)),
                  pl.BlockSpec((Kb, D), lambda i: (0, 0))],
        out_specs=pl.BlockSpec((tm, D), lambda i: (i, 0)),
        out_shape=jax.ShapeDtypeStruct((M, D), F32),
        compiler_params=_cparams("parallel"),
        name="outproj",
    )(x, mix, mo, w[:Ka], w[Ka:])


def _compress_kernel(*refs, n_pages, cpp):
    page_refs = refs[:n_pages]
    pe_ref, w1_ref, w2_ref, o_ref = refs[n_pages:]
    G, DH = B_KV_GROUPS, HEAD_DIM
    nch = n_pages * cpp
    acc = jnp.zeros((G * nch + 8, 2 * DH), F32)
    for c in range(CMP_STRIDE):
        rows = [page_refs[p][:, c, :] for p in range(n_pages)]
        xc = rows[0] if n_pages == 1 else jnp.concatenate(rows, axis=0)
        lhs = jnp.concatenate([xc[:, g * DH:(g + 1) * DH] for g in range(G)] + [pe_ref[c]], axis=0)
        acc = acc + jnp.dot(lhs.astype(BF16), w1_ref[c], preferred_element_type=F32)
    bias_first = acc[G * nch:G * nch + 1, :DH]
    bias_second = acc[G * nch + 1:G * nch + 2, DH:]
    for g in range(G):
        first = acc[g * nch:(g + 1) * nch, :DH] + bias_first
        second = acc[g * nch:(g + 1) * nch, DH:] + bias_second
        hid = jax.nn.gelu(first + pltpu.roll(second, nch - 1, 0))
        o_ref[g] = jnp.dot(hid.astype(BF16), w2_ref[...], preferred_element_type=F32)


def _compress_weights(pe, w1, w2):
    DH = HEAD_DIM
    pe2 = pe.reshape(2, CMP_STRIDE, DH)
    pe_rows = jnp.concatenate([pe2[0][:, None], pe2[1][:, None], jnp.zeros((CMP_STRIDE, 6, DH), F32)], axis=1)
    w12 = w1.reshape(2, CMP_STRIDE, DH, -1)
    w1cat = jnp.concatenate([w12[0], w12[1]], axis=-1).astype(BF16)
    return pe_rows, w1cat, w2.astype(BF16)


def compress_prompt(kvp, col_block, n_batch, pe, w1, w2):
    M = kvp.shape[0]
    nch = M // n_batch // CMP_STRIDE
    pe_rows, w1cat, w2b = _compress_weights(pe, w1, w2)
    x = kvp.reshape(M // CMP_STRIDE, CMP_STRIDE, kvp.shape[1])
    return pl.pallas_call(
        functools.partial(_compress_kernel, n_pages=1, cpp=nch),
        grid=(n_batch,),
        in_specs=[pl.BlockSpec((nch, CMP_STRIDE, 512), lambda b: (b, 0, col_block)),
                  pl.BlockSpec((CMP_STRIDE, 8, HEAD_DIM), lambda b: (0, 0, 0)),
                  pl.BlockSpec((CMP_STRIDE, HEAD_DIM, 2 * HEAD_DIM), lambda b: (0, 0, 0)),
                  pl.BlockSpec((HEAD_DIM, HEAD_DIM), lambda b: (0, 0))],
        out_specs=pl.BlockSpec((None, B_KV_GROUPS, nch, HEAD_DIM), lambda b: (b, 0, 0, 0)),
        out_shape=jax.ShapeDtypeStruct((n_batch, B_KV_GROUPS, nch, HEAD_DIM), F32),
        compiler_params=_cparams("parallel"),
        name="compress_prompt",
    )(x, pe_rows, w1cat, w2b)


_NT = (((1,), (1,)), ((), ()))


def _split2(x):
    hi = x.astype(BF16)
    return hi, (x - hi.astype(F32)).astype(BF16)


def _cmp_select_kernel(q_ref, kc_ref, vc_ref, oc_ref, mem_ref, sc_ref, *, T, nch, nsl, pos0):
    G, R, DH = B_KV_GROUPS, B_GROUP, HEAD_DIM
    base = pos0 + pl.program_id(1) * T
    pos_r = base + lax.broadcasted_iota(jnp.int32, (T, nch), 0)
    n_c = lax.broadcasted_iota(jnp.int32, (T, nch), 1)
    cmask = (n_c * CMP_STRIDE + CMP_LEN - 1) <= pos_r
    j_o = lax.broadcasted_iota(jnp.int32, (nsl, nch), 0)
    st_o = lax.broadcasted_iota(jnp.int32, (nsl, nch), 1) * CMP_STRIDE
    ovl_t = jnp.where((st_o < (j_o + 1) * SLC_BLOCK) & (st_o + CMP_LEN > j_o * SLC_BLOCK), 1.0, 0.0).astype(BF16)
    j_s = lax.broadcasted_iota(jnp.int32, (nsl, T), 0)
    pos_s = base + lax.broadcasted_iota(jnp.int32, (nsl, T), 1)
    cur = pos_s // SLC_BLOCK
    forced = (j_s == 0) | (j_s == cur) | (j_s == cur - 1)
    valid = j_s * SLC_BLOCK <= pos_s
    for g in range(G):
        kcg = kc_ref[g].astype(BF16)
        vcg = vc_ref[g].astype(BF16)
        psum = jnp.zeros((T, nch), F32)
        for r in range(R):
            h = g * R + r
            qh = q_ref[:, h * DH:(h + 1) * DH].astype(BF16)
            s = lax.dot_general(qh, kcg, _NT, preferred_element_type=F32) * SCALE
            sm = jnp.where(cmask, s, NEG)
            e = jnp.exp(sm - jnp.max(sm, axis=-1, keepdims=True))
            p = jnp.where(cmask, e / jnp.sum(e, axis=-1, keepdims=True), 0.0)
            oc_ref[:, h * DH:(h + 1) * DH] = jnp.dot(p.astype(BF16), vcg, preferred_element_type=F32).astype(oc_ref.dtype)
            psum = psum + p
        p_hi, p_lo = _split2(psum)
        imp_t = (lax.dot_general(ovl_t, p_hi, _NT, preferred_element_type=F32)
                 + lax.dot_general(ovl_t, p_lo, _NT, preferred_element_type=F32))
        score = jnp.where(forced, FORCED_SCORE, jnp.where(valid, imp_t, -1.0))
        sc_ref[...] = score
        rank = jnp.zeros((nsl, T), F32)
        for jp in range(nsl):
            row = sc_ref[jp:jp + 1, :]
            beats = (row > score) | ((row == score) & (j_s > jp))
            rank = rank + jnp.where(beats, 1.0, 0.0)
        mem_t = jnp.where(rank < min(N_SELECT, nsl), 1.0, 0.0)
        if nsl < 128:
            mem_t = jnp.concatenate([mem_t, jnp.zeros((128 - nsl, T), F32)], axis=0)
        mem_ref[g] = mem_t.T.astype(mem_ref.dtype)


def cmp_select_prompt(projb, kc, vc, n_batch, *, T=256):
    M = projb.shape[0]
    L = M // n_batch
    nt, nch, nsl = L // T, L // CMP_STRIDE, L // SLC_BLOCK
    return pl.pallas_call(
        functools.partial(_cmp_select_kernel, T=T, nch=nch, nsl=nsl, pos0=0),
        grid=(n_batch, nt),
        in_specs=[pl.BlockSpec((T, TOK_WIDTH), lambda b, i: (b * nt + i, 0)),
                  pl.BlockSpec((None, B_KV_GROUPS, nch, HEAD_DIM), lambda b, i: (b, 0, 0, 0)),
                  pl.BlockSpec((None, B_KV_GROUPS, nch, HEAD_DIM), lambda b, i: (b, 0, 0, 0))],
        out_specs=[pl.BlockSpec((T, TOK_WIDTH), lambda b, i: (b * nt + i, 0)),
                   pl.BlockSpec((None, B_KV_GROUPS, T, 128), lambda b, i: (b, 0, i, 0))],
        out_shape=[jax.ShapeDtypeStruct((M, TOK_WIDTH), BF16),
                   jax.ShapeDtypeStruct((n_batch, B_KV_GROUPS, L, 128), BF16)],
        scratch_shapes=[pltpu.VMEM((nsl, T), F32)],
        compiler_params=_cparams("parallel", "parallel"),
        name="cmp_select_prompt",
    )(projb, kc, vc)


def _sel_win_kernel(q_ref, ks_ref, vs_ref, kw0_ref, kw1_ref, kw2_ref, vw0_ref, vw1_ref, vw2_ref, oc_ref, mem_ref,
                    gt_ref, mix_ref, *, T, CK):
    R, DH = B_GROUP, HEAD_DIM
    g = pl.program_id(1)
    i = pl.program_id(2)
    q3 = jnp.concatenate([q_ref[:, r * DH:(r + 1) * DH] for r in range(R)], axis=0)
    q3 = (q3 * SCALE).astype(BF16)
    qpos = i * T + lax.broadcasted_iota(jnp.int32, (T, CK), 0)
    memb = mem_ref[...]
    j_e = lax.broadcasted_iota(jnp.int32, (128, CK), 0)
    k_e = lax.broadcasted_iota(jnp.int32, (128, CK), 1)
    n_chunks = (i * T + T + CK - 1) // CK

    def chunk(c, carry):
        m, l, acc = carry
        k0 = pl.multiple_of(c * CK, CK)
        kc = ks_ref[pl.ds(k0, CK), :].astype(BF16)
        vc = vs_ref[pl.ds(k0, CK), :].astype(BF16)
        expand = jnp.where((k0 + k_e) // SLC_BLOCK == j_e, 1.0, 0.0).astype(BF16)
        sel = jnp.dot(memb, expand, preferred_element_type=F32) > 0.5
        ok = sel & ((k0 + lax.broadcasted_iota(jnp.int32, (T, CK), 1)) <= qpos)
        bias = jnp.where(ok, 0.0, NEG)
        s = lax.dot_general(q3, kc, _NT, preferred_element_type=F32) + jnp.concatenate([bias] * R, axis=0)
        m_new = jnp.maximum(m, jnp.max(s, axis=-1, keepdims=True))
        alpha = jnp.exp(m - m_new)
        p = jnp.exp(s - m_new)
        l = alpha * l + jnp.sum(p, axis=-1, keepdims=True)
        acc = alpha * acc + jnp.dot(p.astype(BF16), vc, preferred_element_type=F32)
        return m_new, l, acc

    m0 = jnp.full((R * T, 1), NEG, F32)
    _, l, acc = lax.fori_loop(0, n_chunks, chunk, (m0, jnp.zeros((R * T, 1), F32), jnp.zeros((R * T, DH), F32)))
    o_s = acc / l

    kw = jnp.concatenate([kw0_ref[...], kw1_ref[...], kw2_ref[...]], axis=0).astype(BF16)
    vw = jnp.concatenate([vw0_ref[...], vw1_ref[...], vw2_ref[...]], axis=0).astype(BF16)
    qp = i * T + lax.broadcasted_iota(jnp.int32, (T, 3 * T), 0)
    kp = (i - 2) * T + lax.broadcasted_iota(jnp.int32, (T, 3 * T), 1)
    biasw = jnp.where((kp <= qp) & (kp > qp - WINDOW) & (kp >= 0), 0.0, NEG)
    sw = lax.dot_general(q3, kw, _NT, preferred_element_type=F32) + jnp.concatenate([biasw] * R, axis=0)
    ew = jnp.exp(sw - jnp.max(sw, axis=-1, keepdims=True))
    pw = ew / jnp.sum(ew, axis=-1, keepdims=True)
    o_w = jnp.dot(pw.astype(BF16), vw, preferred_element_type=F32)

    gates = jax.nn.sigmoid(gt_ref[...])
    for r in range(R):
        c0 = r * 3
        gsel = _gate_cols(gates, g, c0)
        out = (gsel[0] * oc_ref[:, r * DH:(r + 1) * DH].astype(F32)
               + gsel[1] * o_s[r * T:(r + 1) * T] + gsel[2] * o_w[r * T:(r + 1) * T])
        mix_ref[:, r * DH:(r + 1) * DH] = out.astype(mix_ref.dtype)


def _gate_cols(gates, g, c0):
    lane = lax.broadcasted_iota(jnp.int32, gates.shape, 1)
    tgt = g * (B_GROUP * 3) + c0
    return [jnp.sum(jnp.where(lane == tgt + k, gates, 0.0), axis=-1, keepdims=True) for k in range(3)]


def sel_win_combine(projb, kvp, oc, mem, n_batch, *, T=256, CK=512):
    M = projb.shape[0]
    L = M // n_batch
    nt = L // T
    GW = B_GROUP * HEAD_DIM
    kcol = lambda base: (lambda b, g, i: (b, base + g))
    wspec = lambda base, d: pl.BlockSpec((T, HEAD_DIM), lambda b, g, i: (b * nt + jnp.maximum(i - d, 0), base + g))
    return pl.pallas_call(
        functools.partial(_sel_win_kernel, T=T, CK=CK),
        grid=(n_batch, B_KV_GROUPS, nt),
        in_specs=[pl.BlockSpec((T, GW), lambda b, g, i: (b * nt + i, g)),
                  pl.BlockSpec((L, HEAD_DIM), kcol(8)),
                  pl.BlockSpec((L, HEAD_DIM), kcol(12)),
                  wspec(16, 2), wspec(16, 1), wspec(16, 0),
                  wspec(20, 2), wspec(20, 1), wspec(20, 0),
                  pl.BlockSpec((T, GW), lambda b, g, i: (b * nt + i, g)),
                  pl.BlockSpec((None, None, T, 128), lambda b, g, i: (b, g, i, 0)),
                  pl.BlockSpec((T, 128), lambda b, g, i: (b * nt + i, 16))],
        out_specs=pl.BlockSpec((T, GW), lambda b, g, i: (b * nt + i, g)),
        out_shape=jax.ShapeDtypeStruct((M, TOK_WIDTH), BF16),
        compiler_params=_cparams("parallel", "parallel", "arbitrary"),
        name="sel_win_combine",
    )(projb, kvp, kvp, kvp, kvp, kvp, kvp, kvp, kvp, oc, mem, projb)


def _mem_attn_kernel(q_ref, k_ref, v_ref, o_ref):
    DH = HEAD_DIM
    for h in range(MEM_HEADS):
        qh = (q_ref[:, h * DH:(h + 1) * DH] * SCALE).astype(BF16)
        kh = k_ref[:, h * DH:(h + 1) * DH].astype(BF16)
        vh = v_ref[:, h * DH:(h + 1) * DH].astype(BF16)
        s = lax.dot_general(qh, kh, _NT, preferred_element_type=F32)
        e = jnp.exp(s - jnp.max(s, axis=-1, keepdims=True))
        p = e / jnp.sum(e, axis=-1, keepdims=True)
        o_ref[:, h * DH:(h + 1) * DH] = jnp.dot(p.astype(BF16), vh, preferred_element_type=F32).astype(o_ref.dtype)


def mem_attn_prompt(proj, q_block, memkv, n_batch, *, T=512):
    M = proj.shape[0]
    nt = M // n_batch // T
    return pl.pallas_call(
        _mem_attn_kernel,
        grid=(n_batch, nt),
        in_specs=[pl.BlockSpec((T, MEM_WIDTH), lambda b, i: (b * nt + i, q_block)),
                  pl.BlockSpec((MEM_TOKENS, MEM_WIDTH), lambda b, i: (b, 0)),
                  pl.BlockSpec((MEM_TOKENS, MEM_WIDTH), lambda b, i: (b, 1))],
        out_specs=pl.BlockSpec((T, MEM_WIDTH), lambda b, i: (b * nt + i, 0)),
        out_shape=jax.ShapeDtypeStruct((M, MEM_WIDTH), BF16),
        compiler_params=_cparams("parallel", "parallel"),
        name="mem_attn_prompt",
    )(proj, memkv, memkv)


def _log_sigmoid(x):
    return jnp.minimum(x, 0.0) - jnp.log1p(jnp.exp(-jnp.abs(x)))


def _mlstm_kernel(q_ref, k_ref, v_ref, o_ref, gt_ref, gb_ref, hg_ref, mix_ref, c_ref, n_ref, m_ref):
    H, DH, CL = A_HEADS, HEAD_DIM, MLSTM_CHUNK

    @pl.when(pl.program_id(1) == 0)
    def _():
        c_ref[...] = jnp.zeros_like(c_ref)
        n_ref[...] = jnp.zeros_like(n_ref)
        m_ref[...] = jnp.zeros_like(m_ref)

    gt = (gt_ref[...] + gb_ref[...]).T
    ig = gt[0:16]
    lf = _log_sigmoid(gt[16:32])
    s_r = lax.broadcasted_iota(jnp.int32, (CL, CL), 0)
    s_c = lax.broadcasted_iota(jnp.int32, (CL, CL), 1)
    upper = jnp.where(s_r <= s_c, 1.0, 0.0).astype(BF16)
    lf_hi = lf.astype(BF16)
    r1 = lf - lf_hi.astype(F32)
    lf_mid = r1.astype(BF16)
    lf_lo = (r1 - lf_mid.astype(F32)).astype(BF16)
    b_rows = (jnp.dot(lf_hi, upper, preferred_element_type=F32) + jnp.dot(lf_mid, upper, preferred_element_type=F32)
              + jnp.dot(lf_lo, upper, preferred_element_type=F32))
    u_rows = ig - b_rows
    bl_all = jnp.sum(lf, axis=-1, keepdims=True)
    tril = s_c <= s_r
    for h in range(H):
        b_row = b_rows[h:h + 1]
        u_row = u_rows[h:h + 1]
        bl = bl_all[h:h + 1]
        m_h = m_ref[h:h + 1, :]
        b_col = jnp.broadcast_to(b_row, (CL, CL)).T
        dlog = jnp.where(tril, b_col + u_row, -jnp.inf)
        inter = b_col[:, 0:1] + m_h[:, 0:1]
        mt = jnp.maximum(inter, jnp.max(dlog, axis=-1, keepdims=True))
        a = jnp.exp(inter - mt)
        qh = q_ref[:, h * DH:(h + 1) * DH]
        kh = k_ref[:, h * DH:(h + 1) * DH] * SCALE
        qb, kb, vb = qh.astype(BF16), kh.astype(BF16), v_ref[:, h * DH:(h + 1) * DH].astype(BF16)
        w = jnp.exp(dlog - mt) * lax.dot_general(qb, kb, _NT, preferred_element_type=F32)
        c_old = c_ref[h]
        num = a * jnp.dot(qb, c_old.astype(BF16), preferred_element_type=F32) + jnp.dot(w.astype(BF16), vb, preferred_element_type=F32)
        n_old = n_ref[h:h + 1, :]
        den = a * jnp.sum(qh * n_old, axis=-1, keepdims=True) + jnp.sum(w, axis=-1, keepdims=True)
        hval = num / jnp.maximum(jnp.abs(den), jnp.exp(-mt))
        g_row = bl + u_row
        m_new = jnp.maximum(bl + m_h, jnp.max(g_row, axis=-1, keepdims=True))
        decay = jnp.exp(bl + m_h - m_new)
        wk_row = jnp.exp(g_row - m_new)
        k_t = kh.T
        c_ref[h] = decay[:, 0:1] * c_old + jnp.dot((k_t * wk_row).astype(BF16), vb, preferred_element_type=F32)
        wk8 = jnp.broadcast_to(wk_row, (8, CL)).astype(BF16)
        n_ref[h:h + 1, :] = decay * n_old + jnp.dot(wk8, kb, preferred_element_type=F32)[0:1]
        m_ref[h:h + 1, :] = m_new
        hn = hval * lax.rsqrt(jnp.mean(hval * hval, axis=-1, keepdims=True) + EPS) * hg_ref[:, h * DH:(h + 1) * DH]
        mix_ref[:, h * DH:(h + 1) * DH] = (jax.nn.sigmoid(o_ref[:, h * DH:(h + 1) * DH]) * hn).astype(mix_ref.dtype)


def mlstm_prompt(proja, gate_bias, head_gain, n_batch):
    M = proja.shape[0]
    nc = M // n_batch // MLSTM_CHUNK
    blk = lambda j: pl.BlockSpec((MLSTM_CHUNK, TOK_WIDTH), lambda b, c: (b * nc + c, j))
    return pl.pallas_call(
        _mlstm_kernel,
        grid=(n_batch, nc),
        in_specs=[blk(0), blk(1), blk(2), blk(3),
                  pl.BlockSpec((MLSTM_CHUNK, 128), lambda b, c: (b * nc + c, (4 * TOK_WIDTH + MEM_WIDTH) // 128)),
                  pl.BlockSpec((1, 128), lambda b, c: (0, 0)),
                  pl.BlockSpec((1, TOK_WIDTH), lambda b, c: (0, 0))],
        out_specs=[pl.BlockSpec((MLSTM_CHUNK, TOK_WIDTH), lambda b, c: (b * nc + c, 0)),
                   pl.BlockSpec((None, A_HEADS, HEAD_DIM, HEAD_DIM), lambda b, c: (b, 0, 0, 0)),
                   pl.BlockSpec((None, 16, HEAD_DIM), lambda b, c: (b, 0, 0)),
                   pl.BlockSpec((None, 16, 128), lambda b, c: (b, 0, 0))],
        out_shape=[jax.ShapeDtypeStruct((M, TOK_WIDTH), BF16),
                   jax.ShapeDtypeStruct((n_batch, A_HEADS, HEAD_DIM, HEAD_DIM), F32),
                   jax.ShapeDtypeStruct((n_batch, 16, HEAD_DIM), F32),
                   jax.ShapeDtypeStruct((n_batch, 16, 128), F32)],
        compiler_params=_cparams("parallel", "arbitrary"),
        name="mlstm_prompt",
    )(proja, proja, proja, proja, proja, gate_bias, head_gain.reshape(1, TOK_WIDTH))


def _lane_bcast_rows(row):
    return jnp.broadcast_to(row, (128, 128)).T


def _mlstm_sample_kernel(p_ref, gb_ref, m0_ref, hg_ref, c0_ref, n0_ref, mix_ref, c_ref, n_ref, m_ref):
    H, DH, W = A_HEADS, HEAD_DIM, TOK_WIDTH
    gcol = 4 * W + MEM_WIDTH
    gates = _lane_bcast_rows(p_ref[:, gcol:gcol + 128] + gb_ref[...])
    m_all = _lane_bcast_rows(m0_ref[...])
    m_ref[...] = jnp.zeros_like(m_ref)
    for h in range(H):
        ig = gates[h:h + 1]
        lf = _log_sigmoid(gates[16 + h:17 + h])
        m_old = m_all[h:h + 1]
        m_new = jnp.maximum(lf + m_old, ig)
        a = jnp.exp(lf + m_old - m_new)
        wgt = jnp.exp(ig - m_new)
        q = p_ref[:, h * DH:(h + 1) * DH]
        k = p_ref[:, W + h * DH:W + (h + 1) * DH] * SCALE
        v = p_ref[:, 2 * W + h * DH:2 * W + (h + 1) * DH]
        o = p_ref[:, 3 * W + h * DH:3 * W + (h + 1) * DH]
        c_old = c0_ref[h]
        n_old = n0_ref[h:h + 1, :]
        q_col = jnp.broadcast_to(q, (DH, DH)).T
        k_col = jnp.broadcast_to(k, (DH, DH)).T
        wqk = wgt * jnp.sum(q * k, axis=-1, keepdims=True)
        num = a * jnp.sum(q_col * c_old, axis=0, keepdims=True) + wqk * v
        den = a * jnp.sum(q * n_old, axis=-1, keepdims=True) + wqk
        hval = num / jnp.maximum(jnp.abs(den), jnp.exp(-m_new))
        c_ref[h] = a * c_old + (wgt * k_col) * v
        n_ref[h:h + 1, :] = a * n_old + wgt * k
        m_ref[h:h + 1, :] = m_new
        hn = hval * lax.rsqrt(jnp.mean(hval * hval, axis=-1, keepdims=True) + EPS) * hg_ref[:, h * DH:(h + 1) * DH]
        mix_ref[:, h * DH:(h + 1) * DH] = (jax.nn.sigmoid(o) * hn).astype(mix_ref.dtype)


def mlstm_sample(proja, gate_bias, m0, head_gain, c0, n0):
    Bs, N = proja.shape
    m0p = jnp.pad(m0, ((0, 0), (0, 128 - A_HEADS))).reshape(Bs, 1, 128)
    row = lambda n: pl.BlockSpec((None, 1, n), lambda b: (b, 0, 0))
    const = lambda n: pl.BlockSpec((1, n), lambda b: (0, 0))
    cspec = pl.BlockSpec((None, A_HEADS, HEAD_DIM, HEAD_DIM), lambda b: (b, 0, 0, 0))
    nspec = pl.BlockSpec((None, A_HEADS, HEAD_DIM), lambda b: (b, 0, 0))
    return pl.pallas_call(
        _mlstm_sample_kernel,
        grid=(Bs,),
        in_specs=[row(N), const(128), row(128), const(TOK_WIDTH), cspec, nspec],
        out_specs=[row(TOK_WIDTH), cspec, nspec, pl.BlockSpec((None, 16, 128), lambda b: (b, 0, 0))],
        out_shape=[jax.ShapeDtypeStruct((Bs, 1, TOK_WIDTH), BF16),
                   jax.ShapeDtypeStruct(c0.shape, F32), jax.ShapeDtypeStruct(n0.shape, F32),
                   jax.ShapeDtypeStruct((Bs, 16, 128), F32)],
        compiler_params=_cparams("parallel"),
        name="mlstm_sample",
    )(proja.reshape(Bs, 1, N), gate_bias, m0p, head_gain.reshape(1, TOK_WIDTH), c0, n0)


def _mem_attn_sample_kernel(q_ref, k_ref, v_ref, o_ref, *, nb):
    DH = HEAD_DIM
    for b in range(nb):
        for h in range(MEM_HEADS):
            q8 = jnp.broadcast_to(q_ref[b:b + 1, h * DH:(h + 1) * DH] * SCALE, (8, DH)).astype(BF16)
            kh = k_ref[b, :, h * DH:(h + 1) * DH].astype(BF16)
            vh = v_ref[b, :, h * DH:(h + 1) * DH].astype(BF16)
            s = lax.dot_general(q8, kh, _NT, preferred_element_type=F32)
            e = jnp.exp(s - jnp.max(s, axis=-1, keepdims=True))
            p = e / jnp.sum(e, axis=-1, keepdims=True)
            o = jnp.dot(p.astype(BF16), vh, preferred_element_type=F32)
            o_ref[b:b + 1, h * DH:(h + 1) * DH] = o[0:1].astype(o_ref.dtype)


def mem_attn_sample(proj, q_block, cache_k, cache_v, layer, *, nb=8):
    Bs, N = proj.shape
    kc = cache_k.reshape(cache_k.shape[0], Bs // nb, nb, MEM_TOKENS, MEM_WIDTH)
    vc = cache_v.reshape(cache_v.shape[0], Bs // nb, nb, MEM_TOKENS, MEM_WIDTH)
    cspec = pl.BlockSpec((None, None, nb, MEM_TOKENS, MEM_WIDTH), lambda i: (layer, i, 0, 0, 0))
    return pl.pallas_call(
        functools.partial(_mem_attn_sample_kernel, nb=nb),
        grid=(Bs // nb,),
        in_specs=[pl.BlockSpec((None, nb, MEM_WIDTH), lambda i: (i, 0, q_block)), cspec, cspec],
        out_specs=pl.BlockSpec((None, nb, MEM_WIDTH), lambda i: (i, 0, 0)),
        out_shape=jax.ShapeDtypeStruct((Bs // nb, nb, MEM_WIDTH), BF16),
        compiler_params=_cparams("parallel"),
        name="mem_attn_sample",
    )(proj.reshape(Bs // nb, nb, N), kc, vc).reshape(Bs, MEM_WIDTH)


def compress_sample(cache, page_table, pe, w1, w2):
    Bs, n_pages = page_table.shape
    cpp = PAGE_SIZE // CMP_STRIDE
    nch = n_pages * cpp
    pe_rows, w1cat, w2b = _compress_weights(pe, w1, w2)
    pages = cache.reshape(cache.shape[0], cpp, CMP_STRIDE, B_KV_GROUPS * HEAD_DIM)
    page_specs = [pl.BlockSpec((None, cpp, CMP_STRIDE, B_KV_GROUPS * HEAD_DIM), lambda b, pt, p=p: (pt[b, p], 0, 0, 0))
                  for p in range(n_pages)]

    def body(pt_ref, *refs):
        _compress_kernel(*refs, n_pages=n_pages, cpp=cpp)

    return pl.pallas_call(
        body,
        grid_spec=pltpu.PrefetchScalarGridSpec(
            num_scalar_prefetch=1, grid=(Bs,),
            in_specs=page_specs + [pl.BlockSpec((CMP_STRIDE, 8, HEAD_DIM), lambda b, pt: (0, 0, 0)),
                                   pl.BlockSpec((CMP_STRIDE, HEAD_DIM, 2 * HEAD_DIM), lambda b, pt: (0, 0, 0)),
                                   pl.BlockSpec((HEAD_DIM, HEAD_DIM), lambda b, pt: (0, 0))],
            out_specs=pl.BlockSpec((None, B_KV_GROUPS, nch, HEAD_DIM), lambda b, pt: (b, 0, 0, 0))),
        out_shape=jax.ShapeDtypeStruct((Bs, B_KV_GROUPS, nch, HEAD_DIM), F32),
        compiler_params=_cparams("parallel"),
        name="compress_sample",
    )(page_table, *([pages] * n_pages), pe_rows, w1cat, w2b)


def _nsa_sample_kernel(pt_ref, *refs, n_pages):
    p_ref, kv_ref, kc_ref, vc_ref = refs[:4]
    sk_refs = refs[4:4 + n_pages]
    sv_refs = refs[4 + n_pages:4 + 2 * n_pages]
    wk_ref, wv_ref, mix_ref = refs[4 + 2 * n_pages:]
    G, R, DH, W = B_KV_GROUPS, B_GROUP, HEAD_DIM, TOK_WIDTH
    pos = PAST_LEN
    nch = kc_ref.shape[1]
    n_keys = n_pages * PAGE_SIZE
    nsl = -(-(pos + 1) // SLC_BLOCK)
    w_buf = wk_ref.shape[0]

    n_l = lax.broadcasted_iota(jnp.int32, (8, nch), 1)
    cvalid = n_l * CMP_STRIDE + CMP_LEN - 1 <= pos
    n_o = lax.broadcasted_iota(jnp.int32, (nch, 128), 0) * CMP_STRIDE
    j_o = lax.broadcasted_iota(jnp.int32, (nch, 128), 1)
    ovl = jnp.where((n_o < (j_o + 1) * SLC_BLOCK) & (n_o + CMP_LEN > j_o * SLC_BLOCK), 1.0, 0.0).astype(BF16)
    j_l = lax.broadcasted_iota(jnp.int32, (1, 128), 1)
    cur = pos // SLC_BLOCK
    forced = (j_l == 0) | (j_l == cur) | (j_l == cur - 1)
    jr = lax.broadcasted_iota(jnp.int32, (128, 128), 0)
    jc = lax.broadcasted_iota(jnp.int32, (128, 128), 1)
    j_e = lax.broadcasted_iota(jnp.int32, (128, n_keys), 0)
    k_e = lax.broadcasted_iota(jnp.int32, (128, n_keys), 1)
    expand = jnp.where(k_e // SLC_BLOCK == j_e, 1.0, 0.0).astype(BF16)
    wpos = pos - w_buf + lax.broadcasted_iota(jnp.int32, (8, w_buf), 1)
    wbias = jnp.where(wpos > pos - WINDOW, 0.0, NEG)
    gates = jax.nn.sigmoid(p_ref[:, W + MEM_WIDTH:W + MEM_WIDTH + 128])

    for g in range(G):
        q8 = jnp.concatenate([p_ref[:, (g * R + r) * DH:(g * R + r + 1) * DH] for r in range(R)]
                             + [jnp.zeros((8 - R, DH), F32)], axis=0)
        s = lax.dot_general(q8.astype(BF16), kc_ref[g].astype(BF16), _NT, preferred_element_type=F32) * SCALE
        sm = jnp.where(cvalid, s, NEG)
        e = jnp.exp(sm - jnp.max(sm, axis=-1, keepdims=True))
        p = jnp.where(cvalid, e / jnp.sum(e, axis=-1, keepdims=True), 0.0)
        o_c = jnp.dot(p.astype(BF16), vc_ref[g].astype(BF16), preferred_element_type=F32)
        psum = jnp.broadcast_to(p[0:1] + p[1:2] + p[2:3], (8, nch))
        p_hi, p_lo = _split2(psum)
        imp = (jnp.dot(p_hi, ovl, preferred_element_type=F32) + jnp.dot(p_lo, ovl, preferred_element_type=F32))[0:1]
        score = jnp.where(forced, FORCED_SCORE, jnp.where(j_l * SLC_BLOCK <= pos, imp, -1.0))
        score = jnp.where(j_l < nsl, score, -2.0)
        sc_c = jnp.broadcast_to(score, (128, 128))
        sc_r = sc_c.T
        beats = (sc_r > sc_c) | ((sc_r == sc_c) & (jr < jc))
        rank = jnp.sum(jnp.where(beats, 1.0, 0.0), axis=0, keepdims=True)
        member = jnp.where((rank < min(N_SELECT, nsl)) & (j_l < nsl), 1.0, 0.0)
        mexp = jnp.dot(jnp.broadcast_to(member, (8, 128)).astype(BF16), expand, preferred_element_type=F32)
        sbias = jnp.where(mexp > 0.5, 0.0, NEG)
        q8s = q8 * SCALE
        q8b = q8s.astype(BF16)
        gs = slice(g * DH, (g + 1) * DH)
        ks = jnp.concatenate([r_[:, gs].astype(BF16) for r_ in sk_refs], axis=0)
        vs = jnp.concatenate([r_[:, gs].astype(BF16) for r_ in sv_refs], axis=0)
        ss = lax.dot_general(q8b, ks, _NT, preferred_element_type=F32) + sbias
        s_new = jnp.sum(q8s * kv_ref[:, 2 * G * DH + g * DH:2 * G * DH + (g + 1) * DH], axis=-1, keepdims=True)
        mx = jnp.maximum(jnp.max(ss, axis=-1, keepdims=True), s_new)
        ps = jnp.exp(ss - mx)
        p_new = jnp.exp(s_new - mx)
        o_s = (jnp.dot(ps.astype(BF16), vs, preferred_element_type=F32)
               + p_new * kv_ref[:, 3 * G * DH + g * DH:3 * G * DH + (g + 1) * DH]) / (jnp.sum(ps, axis=-1, keepdims=True) + p_new)
        sw = lax.dot_general(q8b, wk_ref[:, gs].astype(BF16), _NT, preferred_element_type=F32) + wbias
        sw_new = jnp.sum(q8s * kv_ref[:, 4 * G * DH + g * DH:4 * G * DH + (g + 1) * DH], axis=-1, keepdims=True)
        mxw = jnp.maximum(jnp.max(sw, axis=-1, keepdims=True), sw_new)
        pw = jnp.exp(sw - mxw)
        pw_new = jnp.exp(sw_new - mxw)
        o_w = (jnp.dot(pw.astype(BF16), wv_ref[:, gs].astype(BF16), preferred_element_type=F32)
               + pw_new * kv_ref[:, 5 * G * DH + g * DH:5 * G * DH + (g + 1) * DH]) / (jnp.sum(pw, axis=-1, keepdims=True) + pw_new)
        for r in range(R):
            h = g * R + r
            gsel = _gate_cols(gates, g, r * 3)
            out = gsel[0] * o_c[r:r + 1] + gsel[1] * o_s[r:r + 1] + gsel[2] * o_w[r:r + 1]
            mix_ref[:, h * DH:(h + 1) * DH] = out.astype(mix_ref.dtype)


def nsa_sample(projb, kvp, kc, vc, cache_slc_k, cache_slc_v, cache_win_k, cache_win_v, page_table):
    Bs, n_pages = page_table.shape
    GD = B_KV_GROUPS * HEAD_DIM
    w_buf = cache_win_k.shape[1]
    sk = cache_slc_k.reshape(cache_slc_k.shape[0], PAGE_SIZE, GD)
    sv = cache_slc_v.reshape(cache_slc_v.shape[0], PAGE_SIZE, GD)
    row = lambda n: pl.BlockSpec((None, 1, n), lambda b, pt: (b, 0, 0))
    cspec = pl.BlockSpec((None, B_KV_GROUPS, kc.shape[2], HEAD_DIM), lambda b, pt: (b, 0, 0, 0))
    page_specs = [pl.BlockSpec((None, PAGE_SIZE, GD), lambda b, pt, p=p: (pt[b, p], 0, 0)) for p in range(n_pages)]
    wspec = pl.BlockSpec((None, w_buf, GD), lambda b, pt: (b, 0, 0))
    return pl.pallas_call(
        functools.partial(_nsa_sample_kernel, n_pages=n_pages),
        grid_spec=pltpu.PrefetchScalarGridSpec(
            num_scalar_prefetch=1, grid=(Bs,),
            in_specs=[row(projb.shape[1]), row(kvp.shape[1]), cspec, cspec] + page_specs + page_specs + [wspec, wspec],
            out_specs=row(TOK_WIDTH)),
        out_shape=jax.ShapeDtypeStruct((Bs, 1, TOK_WIDTH), BF16),
        compiler_params=_cparams("parallel"),
        name="nsa_sample",
    )(page_table, projb.reshape(Bs, 1, -1), kvp.reshape(Bs, 1, -1), kc, vc, *([sk] * n_pages), *([sv] * n_pages),
      cache_win_k.reshape(Bs, w_buf, GD), cache_win_v.reshape(Bs, w_buf, GD)).reshape(Bs, TOK_WIDTH)


def _j_masked_probs(s, mask):
    p = jax.nn.softmax(jnp.where(mask, s, NEG), axis=-1)
    return jnp.where(mask, p, 0.0)


def _j_mem_attend(q, k, v):
    s = jnp.einsum('blhd,bmhd->bhlm', q, k).astype(F32) * SCALE
    p = jax.nn.softmax(s, axis=-1).astype(v.dtype)
    return jnp.einsum('bhlm,bmhd->blhd', p, v)


def _j_head_rmsnorm(h, g):
    h = h * lax.rsqrt(jnp.mean(h * h, axis=-1, keepdims=True) + EPS)
    return h * g.reshape(h.shape[-2:]).astype(F32)


def _j_mlstm(q, k, v, ig, lf, C0, n0, m0):
    B, L, H, D = q.shape
    c = min(MLSTM_CHUNK, L)
    nc = -(-L // c)
    pad = nc * c - L

    def blocks(a, fill):
        a = jnp.pad(a.astype(F32), [(0, 0), (0, pad)] + [(0, 0)] * (a.ndim - 2), constant_values=fill)
        a = a.reshape((B, nc, c) + a.shape[2:])
        return jnp.swapaxes(jnp.moveaxis(a, 1, 0), 2, 3)

    xs = (blocks(q, 0.0), blocks(k, 0.0), blocks(v, 0.0), blocks(ig, NEG), blocks(lf, 0.0))
    tril = jnp.tril(jnp.ones((c, c), dtype=bool))

    def step(carry, xs_c):
        C, n, m = carry
        qc, kc, vc, ic, fc = xs_c
        b = jnp.cumsum(fc, axis=-1)
        dlog = jnp.where(tril, b[..., :, None] - b[..., None, :] + ic[..., None, :], -jnp.inf)
        inter = b + m[..., None]
        mt = jnp.maximum(inter, jnp.max(dlog, axis=-1))
        a = jnp.exp(inter - mt)
        w = jnp.exp(dlog - mt[..., None]) * jnp.einsum('bhtd,bhsd->bhts', qc, kc)
        num = a[..., None] * jnp.einsum('bhtd,bhde->bhte', qc, C) + jnp.einsum('bhts,bhse->bhte', w, vc)
        den = a * jnp.einsum('bhtd,bhd->bht', qc, n) + jnp.sum(w, axis=-1)
        h = num / jnp.maximum(jnp.abs(den), jnp.exp(-mt))[..., None]
        bl = b[..., -1]
        g = bl[..., None] - b + ic
        m_new = jnp.maximum(bl + m, jnp.max(g, axis=-1))
        decay = jnp.exp(bl + m - m_new)
        wk = jnp.exp(g - m_new[..., None])
        C_new = decay[..., None, None] * C + jnp.einsum('bhs,bhsd,bhse->bhde', wk, kc, vc)
        n_new = decay[..., None] * n + jnp.einsum('bhs,bhsd->bhd', wk, kc)
        return (C_new, n_new, m_new), h

    (Cf, nf, mf), hs = lax.scan(step, (C0.astype(F32), n0.astype(F32), m0.astype(F32)), xs)
    hs = jnp.moveaxis(jnp.swapaxes(hs, 2, 3), 0, 1).reshape(B, nc * c, H, D)[:, :L]
    return hs, (Cf, nf, mf)


def _j_compress(x, pe, w1, w2):
    B, T, G, D = x.shape
    nch = T // CMP_STRIDE
    xs = x[:, :nch * CMP_STRIDE].reshape(B, nch, CMP_STRIDE, G, D)
    pe2 = pe.reshape(2, CMP_STRIDE, D)
    w12 = w1.reshape(2, CMP_STRIDE, D, -1)
    first = jnp.einsum('bncgd,cde->bnge', xs + pe2[0][:, None, :], w12[0])
    second = jnp.einsum('bncgd,cde->bnge', xs + pe2[1][:, None, :], w12[1])
    hid = jax.nn.gelu(first[:, :-1] + second[:, 1:])
    return hid @ w2


def _j_cmp_select(q, pos, kc, vc, n_slc):
    s = jnp.einsum('blgrd,bngd->bgrln', q, kc).astype(F32) * SCALE
    n_cmp = kc.shape[1]
    start = jnp.arange(n_cmp, dtype=jnp.int32) * CMP_STRIDE
    mask = (start + CMP_LEN - 1)[None, :] <= pos[:, None]
    p = _j_masked_probs(s, mask)
    o_c = jnp.einsum('bgrln,bngd->blgrd', p.astype(vc.dtype), vc)
    j = jnp.arange(n_slc, dtype=jnp.int32)
    overlap = ((start[:, None] < (j[None, :] + 1) * SLC_BLOCK)
               & (start[:, None] + CMP_LEN > j[None, :] * SLC_BLOCK)).astype(F32)
    imp = jnp.einsum('bgrln,nj->bglj', p, overlap)
    cur = (pos // SLC_BLOCK)[:, None]
    forced = (j[None, :] == 0) | (j[None, :] == cur) | (j[None, :] == cur - 1)
    valid = j[None, :] * SLC_BLOCK <= pos[:, None]
    score = jnp.where(forced, FORCED_SCORE, jnp.where(valid, imp, -1.0))
    _, idx = lax.top_k(score, min(N_SELECT, n_slc))
    return o_c, jnp.transpose(idx, (0, 2, 1, 3))


def _j_sel_attend(q, pos, idx, gather):
    kb, vb = gather(idx)
    B, L, G, K, C, D = kb.shape
    R = q.shape[3]
    s = jnp.einsum('blgrd,blgkcd->blgrkc', q, kb).astype(F32) * SCALE
    kpos = idx[..., None] * SLC_BLOCK + jnp.arange(SLC_BLOCK, dtype=jnp.int32)
    mask = (kpos <= pos[None, :, None, None, None]).reshape(B, L, G, 1, K * C)
    p = _j_masked_probs(s.reshape(B, L, G, R, K * C), mask).astype(vb.dtype)
    return jnp.einsum('blgrn,blgnd->blgrd', p, vb.reshape(B, L, G, K * C, D))


def _j_sel_attend_blocked(q, pos, idx, gather):
    B, L = q.shape[:2]
    qb = SEL_QBLK if L % SEL_QBLK == 0 else L
    nb = L // qb
    if nb == 1:
        return _j_sel_attend(q, pos, idx, gather)
    qs = jnp.swapaxes(q.reshape((B, nb, qb) + q.shape[2:]), 0, 1)
    ps = pos.reshape(nb, qb)
    ids = jnp.swapaxes(idx.reshape((B, nb, qb) + idx.shape[2:]), 0, 1)
    out = lax.map(lambda a: _j_sel_attend(a[0], a[1], a[2], gather), (qs, ps, ids))
    return jnp.swapaxes(out, 0, 1).reshape((B, L) + out.shape[3:])


def _j_window_prompt(q, kw, vw):
    B, S, G, R, D = q.shape
    nb = S // WIN_QBLK
    span = WINDOW + WIN_QBLK
    padw = [(0, 0), (WINDOW, 0), (0, 0), (0, 0)]
    kp, vp = jnp.pad(kw, padw), jnp.pad(vw, padw)
    idx = jnp.arange(nb, dtype=jnp.int32)[:, None] * WIN_QBLK + jnp.arange(span, dtype=jnp.int32)[None, :]
    kb, vb = kp[:, idx], vp[:, idx]
    qb = q.reshape(B, nb, WIN_QBLK, G, R, D)
    s = jnp.einsum('bnqgrd,bnkgd->bngrqk', qb, kb).astype(F32) * SCALE
    kpos = idx - WINDOW
    qpos = jnp.arange(nb, dtype=jnp.int32)[:, None] * WIN_QBLK + jnp.arange(WIN_QBLK, dtype=jnp.int32)[None, :]
    mask = ((kpos[:, None, :] <= qpos[:, :, None]) & (kpos[:, None, :] > qpos[:, :, None] - WINDOW)
            & (kpos[:, None, :] >= 0))
    p = _j_masked_probs(s, mask[None, :, None, None]).astype(vb.dtype)
    return jnp.einsum('bngrqk,bnkgd->bnqgrd', p, vb).reshape(B, S, G, R, D)


def _j_window_sample(q, pos, kw_new, vw_new, kw_buf, vw_buf):
    w_buf = kw_buf.shape[1]
    kk = jnp.concatenate([kw_buf, kw_new], axis=1)
    vv = jnp.concatenate([vw_buf, vw_new], axis=1)
    kpos = PAST_LEN - w_buf + jnp.arange(kk.shape[1], dtype=jnp.int32)
    s = jnp.einsum('blgrd,bkgd->bgrlk', q, kk).astype(F32) * SCALE
    mask = (kpos[None, :] <= pos[:, None]) & (kpos[None, :] > pos[:, None] - WINDOW)
    p = _j_masked_probs(s, mask).astype(vv.dtype)
    return jnp.einsum('bgrlk,bkgd->blgrd', p, vv)


def _j_combine(gates, o_c, o_s, o_w):
    out = gates[..., 0:1] * o_c + gates[..., 1:2] * o_s + gates[..., 2:3] * o_w
    return out.reshape(out.shape[:2] + (-1,))


def _j_nsa_prompt(q, gates, kv, pe_k, w1_k, w2_k, pe_v, w1_v, w2_v):
    kc_raw, vc_raw, ks, vs, kw, vw = kv
    B, L, G = ks.shape[:3]
    pos = jnp.arange(L, dtype=jnp.int32)
    kc = _j_compress(kc_raw, pe_k, w1_k, w2_k)
    vc = _j_compress(vc_raw, pe_v, w1_v, w2_v)
    n_slc = -(-L // SLC_BLOCK)
    o_c, idx = _j_cmp_select(q, pos, kc, vc, n_slc)
    ksb = ks.reshape(B, n_slc, SLC_BLOCK, G, HEAD_DIM)
    vsb = vs.reshape(B, n_slc, SLC_BLOCK, G, HEAD_DIM)
    b_i = jnp.arange(B)[:, None, None, None]
    g_i = jnp.arange(G)[None, None, :, None]
    gather = lambda ix: (ksb[b_i, ix, :, g_i, :], vsb[b_i, ix, :, g_i, :])
    o_s = _j_sel_attend_blocked(q, pos, idx, gather)
    o_w = _j_window_prompt(q, kw, vw)
    return _j_combine(gates, o_c, o_s, o_w)


def _j_nsa_sample(q, gates, kv, cache_cmp_k, cache_cmp_v, cache_slc_k, cache_slc_v, cache_win_k, cache_win_v,
                  page_table, pe_k, w1_k, w2_k, pe_v, w1_v, w2_v):
    kc_new, vc_new, ks_new, vs_new, kw_new, vw_new = kv
    B, L, G = ks_new.shape[:3]
    n_pages = PAST_LEN // PAGE_SIZE
    pos = PAST_LEN + jnp.arange(L, dtype=jnp.int32)

    def paged_rows(pool):
        return pool[page_table].reshape(B, n_pages * PAGE_SIZE, G, HEAD_DIM)

    kc = _j_compress(jnp.concatenate([paged_rows(cache_cmp_k), kc_new], axis=1), pe_k, w1_k, w2_k)
    vc = _j_compress(jnp.concatenate([paged_rows(cache_cmp_v), vc_new], axis=1), pe_v, w1_v, w2_v)
    n_slc = -(-(PAST_LEN + L) // SLC_BLOCK)
    o_c, idx = _j_cmp_select(q, pos, kc, vc, n_slc)
    n_pb = PAST_LEN // SLC_BLOCK
    n_tail = n_slc - n_pb
    bpp = PAGE_SIZE // SLC_BLOCK

    def tail_blocks(x):
        x = jnp.pad(x, ((0, 0), (0, n_tail * SLC_BLOCK - L), (0, 0), (0, 0)))
        return x.reshape(B, n_tail, SLC_BLOCK, G, HEAD_DIM)

    ks_tail, vs_tail = tail_blocks(ks_new), tail_blocks(vs_new)
    ks_pool = cache_slc_k.reshape(-1, bpp, SLC_BLOCK, G, HEAD_DIM)
    vs_pool = cache_slc_v.reshape(-1, bpp, SLC_BLOCK, G, HEAD_DIM)
    b_i = jnp.arange(B)[:, None, None, None]
    g_i = jnp.arange(G)[None, None, :, None]

    def gather(ix):
        in_past = (ix < n_pb)[..., None, None]
        phys = page_table[b_i, jnp.clip(ix // bpp, 0, n_pages - 1)]
        sub = ix % bpp
        jt = jnp.clip(ix - n_pb, 0, n_tail - 1)
        pick = lambda pool, tail: jnp.where(in_past, pool[phys, sub, :, g_i, :], tail[b_i, jt, :, g_i, :])
        return pick(ks_pool, ks_tail), pick(vs_pool, vs_tail)

    o_s = _j_sel_attend_blocked(q, pos, idx, gather)
    o_w = _j_window_sample(q, pos, kw_new, vw_new, cache_win_k, cache_win_v)
    return _j_combine(gates, o_c, o_s, o_w)


def kernel(x_prompt, x_sample, mem_prompt, cache_mem_k, cache_mem_v, state_mlstm_C, state_mlstm_n, state_mlstm_m, cache_cmp_k, cache_cmp_v, cache_slc_k, cache_slc_v, cache_win_k, cache_win_v, page_table, ffn1_norm, ffn1_w_in, ffn1_w_out, ffn2_norm, ffn2_w_in, ffn2_w_out, mix_norm, a_w_in, a_b_i, a_b_f, a_head_norm, b_w_in, mem_norm, w_mem_kv, w_out, kv_norm, w_kv, cmp_pe_k, cmp_w1_k, cmp_w2_k, cmp_pe_v, cmp_w1_v, cmp_w2_v, final_norm):
    G, R, DH = B_KV_GROUPS, B_GROUP, HEAD_DIM
    ffn1_wi, ffn1_wo = ffn1_w_in.astype(BF16), ffn1_w_out.astype(BF16)
    ffn2_wi, ffn2_wo = ffn2_w_in.astype(BF16), ffn2_w_out.astype(BF16)
    w_o = w_out.astype(BF16)
    w_kv_b = w_kv.astype(BF16)
    w_mkv = w_mem_kv.astype(BF16)

    wa = a_w_in[0]
    zc = lambda n: jnp.zeros((D_MODEL, n), F32)
    a_wr = jnp.concatenate([wa[:, :4 * TOK_WIDTH], wa[:, 4 * TOK_WIDTH + 2 * A_HEADS:],
                            wa[:, 4 * TOK_WIDTH:4 * TOK_WIDTH + A_HEADS], zc(16 - A_HEADS),
                            wa[:, 4 * TOK_WIDTH + A_HEADS:4 * TOK_WIDTH + 2 * A_HEADS], zc(128 - 16 - A_HEADS),
                            zc(128)], axis=1).astype(BF16)
    gate_bias = jnp.concatenate([a_b_i[0], jnp.zeros((16 - A_HEADS,), F32), a_b_f[0],
                                 jnp.zeros((128 - 16 - A_HEADS,), F32)]).reshape(1, 128)
    wb = b_w_in[0]
    b_wr = jnp.concatenate([wb[:, :TOK_WIDTH], wb[:, TOK_WIDTH + 3 * B_HEADS:],
                            wb[:, TOK_WIDTH:TOK_WIDTH + 3 * B_HEADS], zc(256 - 3 * B_HEADS)], axis=1).astype(BF16)

    Bp, S = x_prompt.shape[:2]
    Mp = Bp * S
    memx = mem_prompt.reshape(Bp * MEM_TOKENS, D_MODEL)
    memkv = [norm_matmul(memx, mem_norm[l], w_mkv[l], tm=512, tn=512) for l in range(DEPTH)]
    mem5 = jnp.stack(memkv).reshape(DEPTH, Bp, MEM_TOKENS, 2, MEM_HEADS, DH)
    mem_k_p, mem_v_p = mem5[:, :, :, 0], mem5[:, :, :, 1]

    x = x_prompt.reshape(Mp, D_MODEL)
    x = ffn(x, ffn1_norm[0], ffn1_wi[0], ffn1_wo[0], tm=512, tf=512)
    proja = norm_matmul(x, mix_norm[0], a_wr, tm=512, tn=256)
    mix, C_fin, n_fin, m_fin = mlstm_prompt(proja, gate_bias, a_head_norm[0], Bp)
    mo = mem_attn_prompt(proja, 4 * TOK_WIDTH // MEM_WIDTH, memkv[0], Bp)
    x = outproj(x, mix, mo, w_o[0], tm=512)
    x = ffn(x, ffn2_norm[0], ffn2_wi[0], ffn2_wo[0], tm=512, tf=512)
    kvp = norm_matmul(x, kv_norm, w_kv_b, tm=512, tn=512)
    x = ffn(x, ffn1_norm[1], ffn1_wi[1], ffn1_wo[1], tm=512, tf=512)
    projb = norm_matmul(x, mix_norm[1], b_wr, tm=512, tn=256)
    kc = compress_prompt(kvp, 0, Bp, cmp_pe_k, cmp_w1_k, cmp_w2_k)
    vc = compress_prompt(kvp, 1, Bp, cmp_pe_v, cmp_w1_v, cmp_w2_v)
    oc, member = cmp_select_prompt(projb, kc, vc, Bp)
    mix = sel_win_combine(projb, kvp, oc, member, Bp)
    mo = mem_attn_prompt(projb, TOK_WIDTH // MEM_WIDTH, memkv[1], Bp)
    x = outproj(x, mix, mo, w_o[1], tm=512)
    y_prompt = ffn(x, ffn2_norm[1], ffn2_wi[1], ffn2_wo[1], final_norm, tm=512, tf=512).reshape(Bp, S, D_MODEL)
    C_p = C_fin[None]
    n_p = n_fin[None, :, :A_HEADS]
    m_p = m_fin[None, :, :A_HEADS, 0]
    kvp5 = kvp.reshape(Bp, S, 6, G, DH)
    kv_p = tuple(kvp5[:, :, i] for i in range(6))

    Bs = x_sample.shape[0]
    xs = x_sample.reshape(Bs, D_MODEL)
    xs = ffn(xs, ffn1_norm[0], ffn1_wi[0], ffn1_wo[0], tm=Bs, tf=512)
    proja_s = norm_matmul(xs, mix_norm[0], a_wr, tm=Bs, tn=256)
    mix_s, C_new, n_new, m_new = mlstm_sample(proja_s, gate_bias, state_mlstm_m[0], a_head_norm[0],
                                              state_mlstm_C[0], state_mlstm_n[0])
    mo_s = mem_attn_sample(proja_s, 4 * TOK_WIDTH // MEM_WIDTH, cache_mem_k, cache_mem_v, 0)
    xs = outproj(xs, mix_s.reshape(Bs, TOK_WIDTH), mo_s, w_o[0], tm=Bs)
    xs = ffn(xs, ffn2_norm[0], ffn2_wi[0], ffn2_wo[0], tm=Bs, tf=512)
    kvp_s = norm_matmul(xs, kv_norm, w_kv_b, tm=Bs, tn=512)
    xs = ffn(xs, ffn1_norm[1], ffn1_wi[1], ffn1_wo[1], tm=Bs, tf=512)
    projb_s = norm_matmul(xs, mix_norm[1], b_wr, tm=Bs, tn=256)
    kc_s = compress_sample(cache_cmp_k, page_table, cmp_pe_k, cmp_w1_k, cmp_w2_k)
    vc_s = compress_sample(cache_cmp_v, page_table, cmp_pe_v, cmp_w1_v, cmp_w2_v)
    mix_s = nsa_sample(projb_s, kvp_s, kc_s, vc_s, cache_slc_k, cache_slc_v, cache_win_k, cache_win_v, page_table)
    mo_s = mem_attn_sample(projb_s, TOK_WIDTH // MEM_WIDTH, cache_mem_k, cache_mem_v, 1)
    xs = outproj(xs, mix_s, mo_s, w_o[1], tm=Bs)
    y_sample = ffn(xs, ffn2_norm[1], ffn2_wi[1], ffn2_wo[1], final_norm, tm=Bs, tf=512).reshape(Bs, 1, D_MODEL)
    C_s = C_new[None]
    n_s = n_new[None]
    m_s = m_new[None, :, :A_HEADS, 0]
    kvs5 = kvp_s.reshape(Bs, 1, 6, G, DH)
    kv_s = tuple(kvs5[:, :, i] for i in range(6))

    w_p = min(WINDOW, S)
    cmp_k_p, cmp_v_p, slc_k_p, slc_v_p, win_k_all, win_v_all = kv_p
    win_k_p = win_k_all[:, S - w_p:]
    win_v_p = win_v_all[:, S - w_p:]
    cmp_k_s, cmp_v_s, slc_k_s, slc_v_s, win_k_s, win_v_s = kv_s
    return (y_prompt, y_sample, mem_k_p, mem_v_p, C_p, n_p, m_p, C_s, n_s, m_s,
            cmp_k_p, cmp_v_p, slc_k_p, slc_v_p, win_k_p, win_v_p,
            cmp_k_s, cmp_v_s, slc_k_s, slc_v_s, win_k_s, win_v_s)
```

```python
import functools

import jax
import jax.numpy as jnp
from jax import lax
from jax.experimental import pallas as pl
from jax.experimental.pallas import tpu as pltpu

D_MODEL = 2048
DEPTH = 2
PAST_LEN = 2048
PAGE_SIZE = 128
HEAD_DIM = 128
N_A_LAYERS = DEPTH // 2
MEM_TOKENS = 256
MEM_HEADS = 4
MEM_WIDTH = MEM_HEADS * HEAD_DIM
TOK_WIDTH = D_MODEL - MEM_WIDTH
A_HEADS = TOK_WIDTH // HEAD_DIM
B_HEADS = TOK_WIDTH // HEAD_DIM
B_KV_GROUPS = 4
B_GROUP = B_HEADS // B_KV_GROUPS
D_FF = ((8 * D_MODEL // 3 + 255) // 256) * 256
MLSTM_CHUNK = 128
CMP_STRIDE = 16
CMP_LEN = 2 * CMP_STRIDE
SLC_BLOCK = 64
N_SELECT = 16
WINDOW = 512
WIN_QBLK = 128
SEL_QBLK = 64
FORCED_SCORE = 1e4
NEG = -1e30
EPS = 1e-6
SCALE = HEAD_DIM ** -0.5

F32 = jnp.float32
BF16 = jnp.bfloat16
VMEM_LIMIT = 56 * 1024 * 1024


def _cparams(*sem):
    return pltpu.CompilerParams(dimension_semantics=sem, vmem_limit_bytes=VMEM_LIMIT)


def _rms_bf16(x, g):
    ms = jnp.mean(x * x, axis=-1, keepdims=True)
    return (x * lax.rsqrt(ms + EPS) * g).astype(BF16)


def _norm_matmul_kernel(x_ref, g_ref, w_ref, o_ref, h_ref):
    @pl.when(pl.program_id(1) == 0)
    def _():
        h_ref[...] = _rms_bf16(x_ref[...], g_ref[...])

    o_ref[...] = jnp.dot(h_ref[...], w_ref[...], preferred_element_type=F32).astype(o_ref.dtype)


def norm_matmul(x, g, w, *, tm, tn, out_dtype=F32):
    M, D = x.shape
    N = w.shape[1]
    return pl.pallas_call(
        _norm_matmul_kernel,
        grid=(M // tm, N // tn),
        in_specs=[pl.BlockSpec((tm, D), lambda i, j: (i, 0)),
                  pl.BlockSpec((1, D), lambda i, j: (0, 0)),
                  pl.BlockSpec((D, tn), lambda i, j: (0, j))],
        out_specs=pl.BlockSpec((tm, tn), lambda i, j: (i, j)),
        out_shape=jax.ShapeDtypeStruct((M, N), out_dtype),
        scratch_shapes=[pltpu.VMEM((tm, D), BF16)],
        compiler_params=_cparams("parallel", "arbitrary"),
        name="norm_matmul",
    )(x, g.reshape(1, D), w)


def _kv_proj_kernel(x_ref, g_ref, w_ref, o_ref, o4_ref, h_ref):
    @pl.when(pl.program_id(1) == 0)
    def _():
        h_ref[...] = _rms_bf16(x_ref[...], g_ref[...])

    y = jnp.dot(h_ref[...], w_ref[...], preferred_element_type=F32)
    o_ref[...] = y
    tm = y.shape[0]
    for g in range(B_KV_GROUPS):
        o4_ref[pl.ds(g, tm, stride=B_KV_GROUPS), :] = y[:, g * HEAD_DIM:(g + 1) * HEAD_DIM]


def kv_proj(x, g, w, *, tm):
    M, D = x.shape
    GD = B_KV_GROUPS * HEAD_DIM
    nbr = w.shape[1] // GD
    return pl.pallas_call(
        _kv_proj_kernel,
        grid=(M // tm, nbr),
        in_specs=[pl.BlockSpec((tm, D), lambda i, j: (i, 0)),
                  pl.BlockSpec((1, D), lambda i, j: (0, 0)),
                  pl.BlockSpec((D, GD), lambda i, j: (0, j))],
        out_specs=[pl.BlockSpec((tm, GD), lambda i, j: (i, j)),
                   pl.BlockSpec((None, tm * B_KV_GROUPS, HEAD_DIM), lambda i, j: (j, i, 0))],
        out_shape=[jax.ShapeDtypeStruct((M, nbr * GD), F32),
                   jax.ShapeDtypeStruct((nbr, M * B_KV_GROUPS, HEAD_DIM), F32)],
        scratch_shapes=[pltpu.VMEM((tm, D), BF16)],
        compiler_params=_cparams("parallel", "arbitrary"),
        name="kv_proj",
    )(x, g.reshape(1, D), w)


def _ffn_kernel(x_ref, g_ref, wg_ref, wu_ref, wo_ref, fg_ref, o_ref, h_ref, acc_ref, *, final_norm):
    f = pl.program_id(1)

    @pl.when(f == 0)
    def _():
        h_ref[...] = _rms_bf16(x_ref[...], g_ref[...])
        acc_ref[...] = jnp.zeros_like(acc_ref)

    h = h_ref[...]
    gate = jnp.dot(h, wg_ref[...], preferred_element_type=F32)
    up = jnp.dot(h, wu_ref[...], preferred_element_type=F32)
    act = (gate * jax.nn.sigmoid(gate) * up).astype(BF16)
    acc_ref[...] += jnp.dot(act, wo_ref[...], preferred_element_type=F32)

    @pl.when(f == pl.num_programs(1) - 1)
    def _():
        y = x_ref[...] + 0.5 * acc_ref[...]
        if final_norm:
            ms = jnp.mean(y * y, axis=-1, keepdims=True)
            y = y * lax.rsqrt(ms + EPS) * fg_ref[...]
        o_ref[...] = y


def ffn(x, g, w_in, w_out, final_gain=None, *, tm, tf):
    M, D = x.shape
    F = w_out.shape[0]
    nf = F // tf
    fg = jnp.ones((D,), F32) if final_gain is None else final_gain
    return pl.pallas_call(
        functools.partial(_ffn_kernel, final_norm=final_gain is not None),
        grid=(M // tm, nf),
        in_specs=[pl.BlockSpec((tm, D), lambda i, f: (i, 0)),
                  pl.BlockSpec((1, D), lambda i, f: (0, 0)),
                  pl.BlockSpec((D, tf), lambda i, f: (0, f)),
                  pl.BlockSpec((D, tf), lambda i, f: (0, f + nf)),
                  pl.BlockSpec((tf, D), lambda i, f: (f, 0)),
                  pl.BlockSpec((1, D), lambda i, f: (0, 0))],
        out_specs=pl.BlockSpec((tm, D), lambda i, f: (i, 0)),
        out_shape=jax.ShapeDtypeStruct((M, D), F32),
        scratch_shapes=[pltpu.VMEM((tm, D), BF16), pltpu.VMEM((tm, D), F32)],
        compiler_params=_cparams("parallel", "arbitrary"),
        name="ffn",
    )(x, g.reshape(1, D), w_in, w_in, w_out, fg.reshape(1, D))


def _outproj_kernel(x_ref, mix_ref, mo_ref, wa_ref, wb_ref, o_ref):
    y = jnp.dot(mix_ref[...].astype(BF16), wa_ref[...], preferred_element_type=F32)
    y += jnp.dot(mo_ref[...].astype(BF16), wb_ref[...], preferred_element_type=F32)
    o_ref[...] = x_ref[...] + y


def outproj(x, mix, mo, w, *, tm):
    M, D = x.shape
    Ka, Kb = mix.shape[1], mo.shape[1]
    return pl.pallas_call(
        _outproj_kernel,
        grid=(M // tm,),
        in_specs=[pl.BlockSpec((tm, D), lambda i: (i, 0)),
                  pl.BlockSpec((tm, Ka), lambda i: (i, 0)),
                  pl.BlockSpec((tm, Kb), lambda i: (i, 0)),
                  pl.BlockSpec((Ka, D), lambda i: (0, 0)),
                  pl.BlockSpec((Kb, D), lambda i: (0, 0))],
        out_specs=pl.BlockSpec((tm, D), lambda i: (i, 0)),
        out_shape=jax.ShapeDtypeStruct((M, D), F32),
        compiler_params=_cparams("parallel"),
        name="outproj",
    )(x, mix, mo, w[:Ka], w[Ka:])


def _compress_kernel(*refs, n_pages, cpp, paged):
    page_refs = refs[:n_pages]
    pe_ref, w1_ref, w2_ref, o_ref = refs[n_pages:]
    G, DH = B_KV_GROUPS, HEAD_DIM
    nch = n_pages * cpp

    def piece(p, c, g):
        if paged:
            return page_refs[p][pl.ds(c, cpp, stride=CMP_STRIDE), g, :]
        return page_refs[p][:, c, g * DH:(g + 1) * DH]

    acc = jnp.zeros((G * nch + 8, 2 * DH), F32)
    for c in range(CMP_STRIDE):
        lhs = jnp.concatenate([piece(p, c, g) for g in range(G) for p in range(n_pages)] + [pe_ref[c]], axis=0)
        acc = acc + jnp.dot(lhs.astype(BF16), w1_ref[c], preferred_element_type=F32)
    bias_first = acc[G * nch:G * nch + 1, :DH]
    bias_second = acc[G * nch + 1:G * nch + 2, DH:]
    for g in range(G):
        first = acc[g * nch:(g + 1) * nch, :DH] + bias_first
        second = acc[g * nch:(g + 1) * nch, DH:] + bias_second
        hid = jax.nn.gelu(first + pltpu.roll(second, nch - 1, 0))
        o_ref[g] = jnp.dot(hid.astype(BF16), w2_ref[...], preferred_element_type=F32)


def _compress_weights(pe, w1, w2):
    DH = HEAD_DIM
    pe2 = pe.reshape(2, CMP_STRIDE, DH)
    pe_rows = jnp.concatenate([pe2[0][:, None], pe2[1][:, None], jnp.zeros((CMP_STRIDE, 6, DH), F32)], axis=1)
    w12 = w1.reshape(2, CMP_STRIDE, DH, -1)
    w1cat = jnp.concatenate([w12[0], w12[1]], axis=-1).astype(BF16)
    return pe_rows, w1cat, w2.astype(BF16)


def compress_prompt(kvp, col_block, n_batch, pe, w1, w2):
    M = kvp.shape[0]
    nch = M // n_batch // CMP_STRIDE
    pe_rows, w1cat, w2b = _compress_weights(pe, w1, w2)
    x = kvp.reshape(M // CMP_STRIDE, CMP_STRIDE, kvp.shape[1])
    return pl.pallas_call(
        functools.partial(_compress_kernel, n_pages=1, cpp=nch, paged=False),
        grid=(n_batch,),
        in_specs=[pl.BlockSpec((nch, CMP_STRIDE, 512), lambda b: (b, 0, col_block)),
                  pl.BlockSpec((CMP_STRIDE, 8, HEAD_DIM), lambda b: (0, 0, 0)),
                  pl.BlockSpec((CMP_STRIDE, HEAD_DIM, 2 * HEAD_DIM), lambda b: (0, 0, 0)),
                  pl.BlockSpec((HEAD_DIM, HEAD_DIM), lambda b: (0, 0))],
        out_specs=pl.BlockSpec((None, B_KV_GROUPS, nch, HEAD_DIM), lambda b: (b, 0, 0, 0)),
        out_shape=jax.ShapeDtypeStruct((n_batch, B_KV_GROUPS, nch, HEAD_DIM), F32),
        compiler_params=_cparams("parallel"),
        name="compress_prompt",
    )(x, pe_rows, w1cat, w2b)


_NT = (((1,), (1,)), ((), ()))


def _split2(x):
    hi = x.astype(BF16)
    return hi, (x - hi.astype(F32)).astype(BF16)


def _cmp_select_kernel(q_ref, kc_ref, vc_ref, oc_ref, mem_ref, sc_ref, *, T, nch, nsl, pos0):
    G, R, DH = B_KV_GROUPS, B_GROUP, HEAD_DIM
    base = pos0 + pl.program_id(1) * T
    pos_r = base + lax.broadcasted_iota(jnp.int32, (T, nch), 0)
    n_c = lax.broadcasted_iota(jnp.int32, (T, nch), 1)
    cmask = (n_c * CMP_STRIDE + CMP_LEN - 1) <= pos_r
    j_o = lax.broadcasted_iota(jnp.int32, (nsl, nch), 0)
    st_o = lax.broadcasted_iota(jnp.int32, (nsl, nch), 1) * CMP_STRIDE
    ovl_t = jnp.where((st_o < (j_o + 1) * SLC_BLOCK) & (st_o + CMP_LEN > j_o * SLC_BLOCK), 1.0, 0.0).astype(BF16)
    j_s = lax.broadcasted_iota(jnp.int32, (nsl, T), 0)
    pos_s = base + lax.broadcasted_iota(jnp.int32, (nsl, T), 1)
    cur = pos_s // SLC_BLOCK
    forced = (j_s == 0) | (j_s == cur) | (j_s == cur - 1)
    valid = j_s * SLC_BLOCK <= pos_s
    for g in range(G):
        kcg = kc_ref[g].astype(BF16)
        vcg = vc_ref[g].astype(BF16)
        psum = jnp.zeros((T, nch), F32)
        for r in range(R):
            h = g * R + r
            qh = q_ref[:, h * DH:(h + 1) * DH].astype(BF16)
            s = lax.dot_general(qh, kcg, _NT, preferred_element_type=F32) * SCALE
            sm = jnp.where(cmask, s, NEG)
            e = jnp.exp(sm - jnp.max(sm, axis=-1, keepdims=True))
            p = jnp.where(cmask, e / jnp.sum(e, axis=-1, keepdims=True), 0.0)
            oc_ref[:, h * DH:(h + 1) * DH] = jnp.dot(p.astype(BF16), vcg, preferred_element_type=F32).astype(oc_ref.dtype)
            psum = psum + p
        p_hi, p_lo = _split2(psum)
        imp_t = (lax.dot_general(ovl_t, p_hi, _NT, preferred_element_type=F32)
                 + lax.dot_general(ovl_t, p_lo, _NT, preferred_element_type=F32))
        score = jnp.where(forced, FORCED_SCORE, jnp.where(valid, imp_t, -1.0))
        sc_ref[...] = score
        rank = jnp.zeros((nsl, T), F32)
        for jp in range(nsl):
            row = sc_ref[jp:jp + 1, :]
            beats = (row > score) | ((row == score) & (j_s > jp))
            rank = rank + jnp.where(beats, 1.0, 0.0)
        mem_t = jnp.where(rank < min(N_SELECT, nsl), 1.0, 0.0)
        if nsl < 128:
            mem_t = jnp.concatenate([mem_t, jnp.zeros((128 - nsl, T), F32)], axis=0)
        mem_ref[g] = mem_t.T.astype(mem_ref.dtype)


def cmp_select_prompt(projb, kc, vc, n_batch, *, T=256):
    M = projb.shape[0]
    L = M // n_batch
    nt, nch, nsl = L // T, L // CMP_STRIDE, L // SLC_BLOCK
    return pl.pallas_call(
        functools.partial(_cmp_select_kernel, T=T, nch=nch, nsl=nsl, pos0=0),
        grid=(n_batch, nt),
        in_specs=[pl.BlockSpec((T, TOK_WIDTH), lambda b, i: (b * nt + i, 0)),
                  pl.BlockSpec((None, B_KV_GROUPS, nch, HEAD_DIM), lambda b, i: (b, 0, 0, 0)),
                  pl.BlockSpec((None, B_KV_GROUPS, nch, HEAD_DIM), lambda b, i: (b, 0, 0, 0))],
        out_specs=[pl.BlockSpec((T, TOK_WIDTH), lambda b, i: (b * nt + i, 0)),
                   pl.BlockSpec((None, B_KV_GROUPS, T, 128), lambda b, i: (b, 0, i, 0))],
        out_shape=[jax.ShapeDtypeStruct((M, TOK_WIDTH), BF16),
                   jax.ShapeDtypeStruct((n_batch, B_KV_GROUPS, L, 128), BF16)],
        scratch_shapes=[pltpu.VMEM((nsl, T), F32)],
        compiler_params=_cparams("parallel", "parallel"),
        name="cmp_select_prompt",
    )(projb, kc, vc)


def _sel_win_kernel(q_ref, ks_ref, vs_ref, kw0_ref, kw1_ref, kw2_ref, vw0_ref, vw1_ref, vw2_ref, oc_ref, mem_ref,
                    gt_ref, mix_ref, *, T, CK):
    R, DH = B_GROUP, HEAD_DIM
    g = pl.program_id(1)
    i = pl.program_id(2)
    q3 = jnp.concatenate([q_ref[:, r * DH:(r + 1) * DH] for r in range(R)], axis=0)
    q3 = (q3 * SCALE).astype(BF16)
    qpos = i * T + lax.broadcasted_iota(jnp.int32, (T, CK), 0)
    memb = mem_ref[...]
    j_e = lax.broadcasted_iota(jnp.int32, (128, CK), 0)
    k_e = lax.broadcasted_iota(jnp.int32, (128, CK), 1)
    n_chunks = (i * T + T + CK - 1) // CK

    def chunk(c, carry):
        m, l, acc = carry
        k0 = pl.multiple_of(c * CK, CK)
        kc = ks_ref[pl.ds(k0, CK), :].astype(BF16)
        vc = vs_ref[pl.ds(k0, CK), :].astype(BF16)
        expand = jnp.where((k0 + k_e) // SLC_BLOCK == j_e, 1.0, 0.0).astype(BF16)
        sel = jnp.dot(memb, expand, preferred_element_type=F32) > 0.5
        ok = sel & ((k0 + lax.broadcasted_iota(jnp.int32, (T, CK), 1)) <= qpos)
        bias = jnp.where(ok, 0.0, NEG)
        s = lax.dot_general(q3, kc, _NT, preferred_element_type=F32) + jnp.concatenate([bias] * R, axis=0)
        m_new = jnp.maximum(m, jnp.max(s, axis=-1, keepdims=True))
        alpha = jnp.exp(m - m_new)
        p = jnp.exp(s - m_new)
        l = alpha * l + jnp.sum(p, axis=-1, keepdims=True)
        acc = alpha * acc + jnp.dot(p.astype(BF16), vc, preferred_element_type=F32)
        return m_new, l, acc

    m0 = jnp.full((R * T, 1), NEG, F32)
    _, l, acc = lax.fori_loop(0, n_chunks, chunk, (m0, jnp.zeros((R * T, 1), F32), jnp.zeros((R * T, DH), F32)))
    o_s = acc / l

    kw = jnp.concatenate([kw0_ref[...], kw1_ref[...], kw2_ref[...]], axis=0).astype(BF16)
    vw = jnp.concatenate([vw0_ref[...], vw1_ref[...], vw2_ref[...]], axis=0).astype(BF16)
    qp = i * T + lax.broadcasted_iota(jnp.int32, (T, 3 * T), 0)
    kp = (i - 2) * T + lax.broadcasted_iota(jnp.int32, (T, 3 * T), 1)
    biasw = jnp.where((kp <= qp) & (kp > qp - WINDOW) & (kp >= 0), 0.0, NEG)
    sw = lax.dot_general(q3, kw, _NT, preferred_element_type=F32) + jnp.concatenate([biasw] * R, axis=0)
    ew = jnp.exp(sw - jnp.max(sw, axis=-1, keepdims=True))
    pw = ew / jnp.sum(ew, axis=-1, keepdims=True)
    o_w = jnp.dot(pw.astype(BF16), vw, preferred_element_type=F32)

    gates = jax.nn.sigmoid(gt_ref[...])
    for r in range(R):
        c0 = r * 3
        gsel = _gate_cols(gates, g, c0)
        out = (gsel[0] * oc_ref[:, r * DH:(r + 1) * DH].astype(F32)
               + gsel[1] * o_s[r * T:(r + 1) * T] + gsel[2] * o_w[r * T:(r + 1) * T])
        mix_ref[:, r * DH:(r + 1) * DH] = out.astype(mix_ref.dtype)


def _gate_cols(gates, g, c0):
    lane = lax.broadcasted_iota(jnp.int32, gates.shape, 1)
    tgt = g * (B_GROUP * 3) + c0
    return [jnp.sum(jnp.where(lane == tgt + k, gates, 0.0), axis=-1, keepdims=True) for k in range(3)]


def sel_win_combine(projb, kvp, oc, mem, n_batch, *, T=256, CK=512):
    M = projb.shape[0]
    L = M // n_batch
    nt = L // T
    GW = B_GROUP * HEAD_DIM
    kcol = lambda base: (lambda b, g, i: (b, base + g))
    wspec = lambda base, d: pl.BlockSpec((T, HEAD_DIM), lambda b, g, i: (b * nt + jnp.maximum(i - d, 0), base + g))
    return pl.pallas_call(
        functools.partial(_sel_win_kernel, T=T, CK=CK),
        grid=(n_batch, B_KV_GROUPS, nt),
        in_specs=[pl.BlockSpec((T, GW), lambda b, g, i: (b * nt + i, g)),
                  pl.BlockSpec((L, HEAD_DIM), kcol(8)),
                  pl.BlockSpec((L, HEAD_DIM), kcol(12)),
                  wspec(16, 2), wspec(16, 1), wspec(16, 0),
                  wspec(20, 2), wspec(20, 1), wspec(20, 0),
                  pl.BlockSpec((T, GW), lambda b, g, i: (b * nt + i, g)),
                  pl.BlockSpec((None, None, T, 128), lambda b, g, i: (b, g, i, 0)),
                  pl.BlockSpec((T, 128), lambda b, g, i: (b * nt + i, 16))],
        out_specs=pl.BlockSpec((T, GW), lambda b, g, i: (b * nt + i, g)),
        out_shape=jax.ShapeDtypeStruct((M, TOK_WIDTH), BF16),
        compiler_params=_cparams("parallel", "parallel", "arbitrary"),
        name="sel_win_combine",
    )(projb, kvp, kvp, kvp, kvp, kvp, kvp, kvp, kvp, oc, mem, projb)


def _mem_attn_kernel(q_ref, k_ref, v_ref, o_ref):
    DH = HEAD_DIM
    for h in range(MEM_HEADS):
        qh = (q_ref[:, h * DH:(h + 1) * DH] * SCALE).astype(BF16)
        kh = k_ref[:, h * DH:(h + 1) * DH].astype(BF16)
        vh = v_ref[:, h * DH:(h + 1) * DH].astype(BF16)
        s = lax.dot_general(qh, kh, _NT, preferred_element_type=F32)
        e = jnp.exp(s - jnp.max(s, axis=-1, keepdims=True))
        p = e / jnp.sum(e, axis=-1, keepdims=True)
        o_ref[:, h * DH:(h + 1) * DH] = jnp.dot(p.astype(BF16), vh, preferred_element_type=F32).astype(o_ref.dtype)


def mem_attn_prompt(proj, q_block, memkv, n_batch, *, T=512):
    M = proj.shape[0]
    nt = M // n_batch // T
    return pl.pallas_call(
        _mem_attn_kernel,
        grid=(n_batch, nt),
        in_specs=[pl.BlockSpec((T, MEM_WIDTH), lambda b, i: (b * nt + i, q_block)),
                  pl.BlockSpec((MEM_TOKENS, MEM_WIDTH), lambda b, i: (b, 0)),
                  pl.BlockSpec((MEM_TOKENS, MEM_WIDTH), lambda b, i: (b, 1))],
        out_specs=pl.BlockSpec((T, MEM_WIDTH), lambda b, i: (b * nt + i, 0)),
        out_shape=jax.ShapeDtypeStruct((M, MEM_WIDTH), BF16),
        compiler_params=_cparams("parallel", "parallel"),
        name="mem_attn_prompt",
    )(proj, memkv, memkv)


def _log_sigmoid(x):
    return jnp.minimum(x, 0.0) - jnp.log1p(jnp.exp(-jnp.abs(x)))


def _mlstm_kernel(q_ref, k_ref, v_ref, o_ref, gt_ref, gb_ref, hg_ref, mix_ref, c_ref, n_ref, m_ref):
    H, DH, CL = A_HEADS, HEAD_DIM, MLSTM_CHUNK

    @pl.when(pl.program_id(1) == 0)
    def _():
        c_ref[...] = jnp.zeros_like(c_ref)
        n_ref[...] = jnp.zeros_like(n_ref)
        m_ref[...] = jnp.zeros_like(m_ref)

    gt = (gt_ref[...] + gb_ref[...]).T
    ig = gt[0:16]
    lf = _log_sigmoid(gt[16:32])
    s_r = lax.broadcasted_iota(jnp.int32, (CL, CL), 0)
    s_c = lax.broadcasted_iota(jnp.int32, (CL, CL), 1)
    upper = jnp.where(s_r <= s_c, 1.0, 0.0).astype(BF16)
    lf_hi = lf.astype(BF16)
    r1 = lf - lf_hi.astype(F32)
    lf_mid = r1.astype(BF16)
    lf_lo = (r1 - lf_mid.astype(F32)).astype(BF16)
    b_rows = (jnp.dot(lf_hi, upper, preferred_element_type=F32) + jnp.dot(lf_mid, upper, preferred_element_type=F32)
              + jnp.dot(lf_lo, upper, preferred_element_type=F32))
    u_rows = ig - b_rows
    bl_all = jnp.sum(lf, axis=-1, keepdims=True)
    tril = s_c <= s_r
    for h in range(H):
        b_row = b_rows[h:h + 1]
        u_row = u_rows[h:h + 1]
        bl = bl_all[h:h + 1]
        m_h = m_ref[h:h + 1, :]
        b_col = jnp.broadcast_to(b_row, (CL, CL)).T
        dlog = jnp.where(tril, b_col + u_row, -jnp.inf)
        inter = b_col[:, 0:1] + m_h[:, 0:1]
        mt = jnp.maximum(inter, jnp.max(dlog, axis=-1, keepdims=True))
        a = jnp.exp(inter - mt)
        qh = q_ref[:, h * DH:(h + 1) * DH]
        kh = k_ref[:, h * DH:(h + 1) * DH] * SCALE
        qb, kb, vb = qh.astype(BF16), kh.astype(BF16), v_ref[:, h * DH:(h + 1) * DH].astype(BF16)
        w = jnp.exp(dlog - mt) * lax.dot_general(qb, kb, _NT, preferred_element_type=F32)
        c_old = c_ref[h]
        num = a * jnp.dot(qb, c_old.astype(BF16), preferred_element_type=F32) + jnp.dot(w.astype(BF16), vb, preferred_element_type=F32)
        n_old = n_ref[h:h + 1, :]
        den = a * jnp.sum(qh * n_old, axis=-1, keepdims=True) + jnp.sum(w, axis=-1, keepdims=True)
        hval = num / jnp.maximum(jnp.abs(den), jnp.exp(-mt))
        g_row = bl + u_row
        m_new = jnp.maximum(bl + m_h, jnp.max(g_row, axis=-1, keepdims=True))
        decay = jnp.exp(bl + m_h - m_new)
        wk_row = jnp.exp(g_row - m_new)
        k_t = kh.T
        c_ref[h] = decay[:, 0:1] * c_old + jnp.dot((k_t * wk_row).astype(BF16), vb, preferred_element_type=F32)
        wk8 = jnp.broadcast_to(wk_row, (8, CL)).astype(BF16)
        n_ref[h:h + 1, :] = decay * n_old + jnp.dot(wk8, kb, preferred_element_type=F32)[0:1]
        m_ref[h:h + 1, :] = m_new
        hn = hval * lax.rsqrt(jnp.mean(hval * hval, axis=-1, keepdims=True) + EPS) * hg_ref[:, h * DH:(h + 1) * DH]
        mix_ref[:, h * DH:(h + 1) * DH] = (jax.nn.sigmoid(o_ref[:, h * DH:(h + 1) * DH]) * hn).astype(mix_ref.dtype)


def mlstm_prompt(proja, gate_bias, head_gain, n_batch):
    M = proja.shape[0]
    nc = M // n_batch // MLSTM_CHUNK
    blk = lambda j: pl.BlockSpec((MLSTM_CHUNK, TOK_WIDTH), lambda b, c: (b * nc + c, j))
    return pl.pallas_call(
        _mlstm_kernel,
        grid=(n_batch, nc),
        in_specs=[blk(0), blk(1), blk(2), blk(3),
                  pl.BlockSpec((MLSTM_CHUNK, 128), lambda b, c: (b * nc + c, (4 * TOK_WIDTH + MEM_WIDTH) // 128)),
                  pl.BlockSpec((1, 128), lambda b, c: (0, 0)),
                  pl.BlockSpec((1, TOK_WIDTH), lambda b, c: (0, 0))],
        out_specs=[pl.BlockSpec((MLSTM_CHUNK, TOK_WIDTH), lambda b, c: (b * nc + c, 0)),
                   pl.BlockSpec((None, A_HEADS, HEAD_DIM, HEAD_DIM), lambda b, c: (b, 0, 0, 0)),
                   pl.BlockSpec((None, 16, HEAD_DIM), lambda b, c: (b, 0, 0)),
                   pl.BlockSpec((None, 16, 128), lambda b, c: (b, 0, 0))],
        out_shape=[jax.ShapeDtypeStruct((M, TOK_WIDTH), BF16),
                   jax.ShapeDtypeStruct((n_batch, A_HEADS, HEAD_DIM, HEAD_DIM), F32),
                   jax.ShapeDtypeStruct((n_batch, 16, HEAD_DIM), F32),
                   jax.ShapeDtypeStruct((n_batch, 16, 128), F32)],
        compiler_params=_cparams("parallel", "arbitrary"),
        name="mlstm_prompt",
    )(proja, proja, proja, proja, proja, gate_bias, head_gain.reshape(1, TOK_WIDTH))


def _lane_bcast_rows(row):
    return jnp.broadcast_to(row, (128, 128)).T


def _mlstm_sample_kernel(p_ref, gb_ref, m0_ref, hg_ref, c0_ref, n0_ref, mix_ref, c_ref, n_ref, m_ref):
    H, DH, W = A_HEADS, HEAD_DIM, TOK_WIDTH
    gcol = 4 * W + MEM_WIDTH
    gates = _lane_bcast_rows(p_ref[:, gcol:gcol + 128] + gb_ref[...])
    m_all = _lane_bcast_rows(m0_ref[...])
    m_ref[...] = jnp.zeros_like(m_ref)
    for h in range(H):
        ig = gates[h:h + 1]
        lf = _log_sigmoid(gates[16 + h:17 + h])
        m_old = m_all[h:h + 1]
        m_new = jnp.maximum(lf + m_old, ig)
        a = jnp.exp(lf + m_old - m_new)
        wgt = jnp.exp(ig - m_new)
        q = p_ref[:, h * DH:(h + 1) * DH]
        k = p_ref[:, W + h * DH:W + (h + 1) * DH] * SCALE
        v = p_ref[:, 2 * W + h * DH:2 * W + (h + 1) * DH]
        o = p_ref[:, 3 * W + h * DH:3 * W + (h + 1) * DH]
        c_old = c0_ref[h]
        n_old = n0_ref[h:h + 1, :]
        q_col = jnp.broadcast_to(q, (DH, DH)).T
        k_col = jnp.broadcast_to(k, (DH, DH)).T
        wqk = wgt * jnp.sum(q * k, axis=-1, keepdims=True)
        num = a * jnp.sum(q_col * c_old, axis=0, keepdims=True) + wqk * v
        den = a * jnp.sum(q * n_old, axis=-1, keepdims=True) + wqk
        hval = num / jnp.maximum(jnp.abs(den), jnp.exp(-m_new))
        c_ref[h] = a * c_old + (wgt * k_col) * v
        n_ref[h:h + 1, :] = a * n_old + wgt * k
        m_ref[h:h + 1, :] = m_new
        hn = hval * lax.rsqrt(jnp.mean(hval * hval, axis=-1, keepdims=True) + EPS) * hg_ref[:, h * DH:(h + 1) * DH]
        mix_ref[:, h * DH:(h + 1) * DH] = (jax.nn.sigmoid(o) * hn).astype(mix_ref.dtype)


def mlstm_sample(proja, gate_bias, m0, head_gain, c0, n0):
    Bs, N = proja.shape
    m0p = jnp.pad(m0, ((0, 0), (0, 128 - A_HEADS))).reshape(Bs, 1, 128)
    row = lambda n: pl.BlockSpec((None, 1, n), lambda b: (b, 0, 0))
    const = lambda n: pl.BlockSpec((1, n), lambda b: (0, 0))
    cspec = pl.BlockSpec((None, A_HEADS, HEAD_DIM, HEAD_DIM), lambda b: (b, 0, 0, 0))
    nspec = pl.BlockSpec((None, A_HEADS, HEAD_DIM), lambda b: (b, 0, 0))
    return pl.pallas_call(
        _mlstm_sample_kernel,
        grid=(Bs,),
        in_specs=[row(N), const(128), row(128), const(TOK_WIDTH), cspec, nspec],
        out_specs=[row(TOK_WIDTH), cspec, nspec, pl.BlockSpec((None, 16, 128), lambda b: (b, 0, 0))],
        out_shape=[jax.ShapeDtypeStruct((Bs, 1, TOK_WIDTH), BF16),
                   jax.ShapeDtypeStruct(c0.shape, F32), jax.ShapeDtypeStruct(n0.shape, F32),
                   jax.ShapeDtypeStruct((Bs, 16, 128), F32)],
        compiler_params=_cparams("parallel"),
        name="mlstm_sample",
    )(proja.reshape(Bs, 1, N), gate_bias, m0p, head_gain.reshape(1, TOK_WIDTH), c0, n0)


def _mem_attn_sample_kernel(q_ref, k_ref, v_ref, o_ref, *, nb):
    H, DH = MEM_HEADS, HEAD_DIM
    nrow = k_ref.shape[1]
    row_h = lax.broadcasted_iota(jnp.int32, (8, nrow), 0)
    col_h = lax.broadcasted_iota(jnp.int32, (8, nrow), 1) % H
    bias = jnp.where(row_h == col_h, 0.0, NEG)
    for b in range(nb):
        q8 = jnp.concatenate([q_ref[b:b + 1, h * DH:(h + 1) * DH] for h in range(H)] + [jnp.zeros((8 - H, DH), F32)], axis=0)
        s = lax.dot_general((q8 * SCALE).astype(BF16), k_ref[b].astype(BF16), _NT, preferred_element_type=F32) + bias
        e = jnp.exp(s - jnp.max(s, axis=-1, keepdims=True))
        p = e / jnp.sum(e, axis=-1, keepdims=True)
        o = jnp.dot(p.astype(BF16), v_ref[b].astype(BF16), preferred_element_type=F32)
        o_ref[b * H:(b + 1) * H, :] = o[0:H]


def mem_attn_sample(proj, q_block, cache_k, cache_v, layer, *, nb=8):
    Bs, N = proj.shape
    nrow = MEM_TOKENS * MEM_HEADS
    k2 = cache_k.reshape(cache_k.shape[0], Bs, nrow, HEAD_DIM)
    v2 = cache_v.reshape(cache_v.shape[0], Bs, nrow, HEAD_DIM)
    cspec = pl.BlockSpec((None, nb, nrow, HEAD_DIM), lambda i: (layer, i, 0, 0))
    return pl.pallas_call(
        functools.partial(_mem_attn_sample_kernel, nb=nb),
        grid=(Bs // nb,),
        in_specs=[pl.BlockSpec((None, nb, MEM_WIDTH), lambda i: (i, 0, q_block)), cspec, cspec],
        out_specs=pl.BlockSpec((nb * MEM_HEADS, HEAD_DIM), lambda i: (i, 0)),
        out_shape=jax.ShapeDtypeStruct((Bs * MEM_HEADS, HEAD_DIM), F32),
        compiler_params=_cparams("parallel"),
        name="mem_attn_sample",
    )(proj.reshape(Bs // nb, nb, N), k2, v2).reshape(Bs, MEM_WIDTH)


def compress_sample(cache, page_table, pe, w1, w2):
    Bs, n_pages = page_table.shape
    cpp = PAGE_SIZE // CMP_STRIDE
    nch = n_pages * cpp
    pe_rows, w1cat, w2b = _compress_weights(pe, w1, w2)
    page_specs = [pl.BlockSpec((None, PAGE_SIZE, B_KV_GROUPS, HEAD_DIM), lambda b, pt, p=p: (pt[b, p], 0, 0, 0))
                  for p in range(n_pages)]

    def body(pt_ref, *refs):
        _compress_kernel(*refs, n_pages=n_pages, cpp=cpp, paged=True)

    return pl.pallas_call(
        body,
        grid_spec=pltpu.PrefetchScalarGridSpec(
            num_scalar_prefetch=1, grid=(Bs,),
            in_specs=page_specs + [pl.BlockSpec((CMP_STRIDE, 8, HEAD_DIM), lambda b, pt: (0, 0, 0)),
                                   pl.BlockSpec((CMP_STRIDE, HEAD_DIM, 2 * HEAD_DIM), lambda b, pt: (0, 0, 0)),
                                   pl.BlockSpec((HEAD_DIM, HEAD_DIM), lambda b, pt: (0, 0))],
            out_specs=pl.BlockSpec((None, B_KV_GROUPS, nch, HEAD_DIM), lambda b, pt: (b, 0, 0, 0))),
        out_shape=jax.ShapeDtypeStruct((Bs, B_KV_GROUPS, nch, HEAD_DIM), F32),
        compiler_params=_cparams("parallel"),
        name="compress_sample",
    )(page_table, *([cache] * n_pages), pe_rows, w1cat, w2b)


def _nsa_sample_kernel(pt_ref, *refs, n_pages):
    p_ref, kv_ref, kc_ref, vc_ref, ex_ref = refs[:5]
    sk_refs = refs[5:5 + n_pages]
    sv_refs = refs[5 + n_pages:5 + 2 * n_pages]
    wk_ref, wv_ref, mix_ref = refs[5 + 2 * n_pages:]
    G, R, DH, W = B_KV_GROUPS, B_GROUP, HEAD_DIM, TOK_WIDTH
    pos = PAST_LEN
    nch = kc_ref.shape[1]
    ncol = n_pages * sk_refs[0].shape[0]
    nsl = -(-(pos + 1) // SLC_BLOCK)
    wrow = wk_ref.shape[0]
    w_buf = wrow // G

    n_l = lax.broadcasted_iota(jnp.int32, (8, nch), 1)
    cvalid = n_l * CMP_STRIDE + CMP_LEN - 1 <= pos
    n_o = lax.broadcasted_iota(jnp.int32, (nch, 128), 0) * CMP_STRIDE
    j_o = lax.broadcasted_iota(jnp.int32, (nch, 128), 1)
    ovl = jnp.where((n_o < (j_o + 1) * SLC_BLOCK) & (n_o + CMP_LEN > j_o * SLC_BLOCK), 1.0, 0.0).astype(BF16)
    j_l = lax.broadcasted_iota(jnp.int32, (1, 128), 1)
    cur = pos // SLC_BLOCK
    forced = (j_l == 0) | (j_l == cur) | (j_l == cur - 1)
    jr = lax.broadcasted_iota(jnp.int32, (128, 128), 0)
    jc = lax.broadcasted_iota(jnp.int32, (128, 128), 1)
    col_g = lax.broadcasted_iota(jnp.int32, (8, ncol), 1) % G
    wcol = lax.broadcasted_iota(jnp.int32, (8, wrow), 1)
    wvalid = (pos - w_buf + wcol // G) > pos - WINDOW
    gates = jax.nn.sigmoid(p_ref[:, W + MEM_WIDTH:W + MEM_WIDTH + 128])

    q_rows, oc_rows, sb_rows, wb_rows = [], [], [], []
    for g in range(G):
        q8 = jnp.concatenate([p_ref[:, (g * R + r) * DH:(g * R + r + 1) * DH] for r in range(R)]
                             + [jnp.zeros((8 - R, DH), F32)], axis=0)
        s = lax.dot_general(q8.astype(BF16), kc_ref[g].astype(BF16), _NT, preferred_element_type=F32) * SCALE
        sm = jnp.where(cvalid, s, NEG)
        e = jnp.exp(sm - jnp.max(sm, axis=-1, keepdims=True))
        p = jnp.where(cvalid, e / jnp.sum(e, axis=-1, keepdims=True), 0.0)
        o_c = jnp.dot(p.astype(BF16), vc_ref[g].astype(BF16), preferred_element_type=F32)
        psum = jnp.broadcast_to(p[0:1] + p[1:2] + p[2:3], (8, nch))
        p_hi, p_lo = _split2(psum)
        imp = (jnp.dot(p_hi, ovl, preferred_element_type=F32) + jnp.dot(p_lo, ovl, preferred_element_type=F32))[0:1]
        score = jnp.where(forced, FORCED_SCORE, jnp.where(j_l * SLC_BLOCK <= pos, imp, -1.0))
        score = jnp.where(j_l < nsl, score, -2.0)
        sc_c = jnp.broadcast_to(score, (128, 128))
        sc_r = sc_c.T
        beats = (sc_r > sc_c) | ((sc_r == sc_c) & (jr < jc))
        rank = jnp.sum(jnp.where(beats, 1.0, 0.0), axis=0, keepdims=True)
        member = jnp.where((rank < min(N_SELECT, nsl)) & (j_l < nsl), 1.0, 0.0)
        mexp = jnp.dot(jnp.broadcast_to(member, (8, 128)).astype(BF16), ex_ref[...], preferred_element_type=F32)
        sb_rows.append(jnp.where((mexp > 0.5) & (col_g == g), 0.0, NEG))
        wb_rows.append(jnp.where(wvalid & (wcol % G == g), 0.0, NEG))
        q_rows.append(q8)
        oc_rows.append(o_c)

    q32 = jnp.concatenate(q_rows, axis=0) * SCALE
    q32b = q32.astype(BF16)
    o_c = jnp.concatenate(oc_rows, axis=0)

    def new_rows(base):
        return jnp.concatenate([jnp.broadcast_to(kv_ref[:, base + g * DH:base + (g + 1) * DH], (8, DH)) for g in range(G)], axis=0)

    def branch(k_rows, v_rows, bias, k_base, v_base):
        s = lax.dot_general(q32b, k_rows, _NT, preferred_element_type=F32) + bias
        s_new = jnp.sum(q32 * new_rows(k_base), axis=-1, keepdims=True)
        mx = jnp.maximum(jnp.max(s, axis=-1, keepdims=True), s_new)
        p = jnp.exp(s - mx)
        p_new = jnp.exp(s_new - mx)
        num = jnp.dot(p.astype(BF16), v_rows, preferred_element_type=F32) + p_new * new_rows(v_base)
        return num / (jnp.sum(p, axis=-1, keepdims=True) + p_new)

    ks = jnp.concatenate([r_[...].astype(BF16) for r_ in sk_refs], axis=0)
    vs = jnp.concatenate([r_[...].astype(BF16) for r_ in sv_refs], axis=0)
    o_s = branch(ks, vs, jnp.concatenate(sb_rows, axis=0), 2 * G * DH, 3 * G * DH)
    o_w = branch(wk_ref[...].astype(BF16), wv_ref[...].astype(BF16), jnp.concatenate(wb_rows, axis=0), 4 * G * DH, 5 * G * DH)
    for g in range(G):
        for r in range(R):
            h, row = g * R + r, g * 8 + r
            gsel = _gate_cols(gates, g, r * 3)
            out = gsel[0] * o_c[row:row + 1] + gsel[1] * o_s[row:row + 1] + gsel[2] * o_w[row:row + 1]
            mix_ref[:, h * DH:(h + 1) * DH] = out.astype(mix_ref.dtype)


def nsa_sample(projb, kvp, kc, vc, cache_slc_k, cache_slc_v, cache_win_k, cache_win_v, page_table):
    Bs, n_pages = page_table.shape
    G = B_KV_GROUPS
    prow = PAGE_SIZE * G
    wrow = cache_win_k.shape[1] * G
    sk = cache_slc_k.reshape(cache_slc_k.shape[0], prow, HEAD_DIM)
    sv = cache_slc_v.reshape(cache_slc_v.shape[0], prow, HEAD_DIM)
    wk = cache_win_k.reshape(Bs, wrow, HEAD_DIM)
    wv = cache_win_v.reshape(Bs, wrow, HEAD_DIM)
    ncol = n_pages * prow
    expand = (jnp.arange(ncol, dtype=jnp.int32)[None, :] // (G * SLC_BLOCK)
              == jnp.arange(128, dtype=jnp.int32)[:, None]).astype(BF16)
    row = lambda n: pl.BlockSpec((None, 1, n), lambda b, pt: (b, 0, 0))
    cspec = pl.BlockSpec((None, G, kc.shape[2], HEAD_DIM), lambda b, pt: (b, 0, 0, 0))
    page_specs = [pl.BlockSpec((None, prow, HEAD_DIM), lambda b, pt, p=p: (pt[b, p], 0, 0)) for p in range(n_pages)]
    wspec = pl.BlockSpec((None, wrow, HEAD_DIM), lambda b, pt: (b, 0, 0))
    return pl.pallas_call(
        functools.partial(_nsa_sample_kernel, n_pages=n_pages),
        grid_spec=pltpu.PrefetchScalarGridSpec(
            num_scalar_prefetch=1, grid=(Bs,),
            in_specs=[row(projb.shape[1]), row(kvp.shape[1]), cspec, cspec,
                      pl.BlockSpec((128, ncol), lambda b, pt: (0, 0))] + page_specs + page_specs + [wspec, wspec],
            out_specs=row(TOK_WIDTH)),
        out_shape=jax.ShapeDtypeStruct((Bs, 1, TOK_WIDTH), BF16),
        compiler_params=_cparams("parallel"),
        name="nsa_sample",
    )(page_table, projb.reshape(Bs, 1, -1), kvp.reshape(Bs, 1, -1), kc, vc, expand, *([sk] * n_pages),
      *([sv] * n_pages), wk, wv).reshape(Bs, TOK_WIDTH)


def _j_masked_probs(s, mask):
    p = jax.nn.softmax(jnp.where(mask, s, NEG), axis=-1)
    return jnp.where(mask, p, 0.0)


def _j_mem_attend(q, k, v):
    s = jnp.einsum('blhd,bmhd->bhlm', q, k).astype(F32) * SCALE
    p = jax.nn.softmax(s, axis=-1).astype(v.dtype)
    return jnp.einsum('bhlm,bmhd->blhd', p, v)


def _j_head_rmsnorm(h, g):
    h = h * lax.rsqrt(jnp.mean(h * h, axis=-1, keepdims=True) + EPS)
    return h * g.reshape(h.shape[-2:]).astype(F32)


def _j_mlstm(q, k, v, ig, lf, C0, n0, m0):
    B, L, H, D = q.shape
    c = min(MLSTM_CHUNK, L)
    nc = -(-L // c)
    pad = nc * c - L

    def blocks(a, fill):
        a = jnp.pad(a.astype(F32), [(0, 0), (0, pad)] + [(0, 0)] * (a.ndim - 2), constant_values=fill)
        a = a.reshape((B, nc, c) + a.shape[2:])
        return jnp.swapaxes(jnp.moveaxis(a, 1, 0), 2, 3)

    xs = (blocks(q, 0.0), blocks(k, 0.0), blocks(v, 0.0), blocks(ig, NEG), blocks(lf, 0.0))
    tril = jnp.tril(jnp.ones((c, c), dtype=bool))

    def step(carry, xs_c):
        C, n, m = carry
        qc, kc, vc, ic, fc = xs_c
        b = jnp.cumsum(fc, axis=-1)
        dlog = jnp.where(tril, b[..., :, None] - b[..., None, :] + ic[..., None, :], -jnp.inf)
        inter = b + m[..., None]
        mt = jnp.maximum(inter, jnp.max(dlog, axis=-1))
        a = jnp.exp(inter - mt)
        w = jnp.exp(dlog - mt[..., None]) * jnp.einsum('bhtd,bhsd->bhts', qc, kc)
        num = a[..., None] * jnp.einsum('bhtd,bhde->bhte', qc, C) + jnp.einsum('bhts,bhse->bhte', w, vc)
        den = a * jnp.einsum('bhtd,bhd->bht', qc, n) + jnp.sum(w, axis=-1)
        h = num / jnp.maximum(jnp.abs(den), jnp.exp(-mt))[..., None]
        bl = b[..., -1]
        g = bl[..., None] - b + ic
        m_new = jnp.maximum(bl + m, jnp.max(g, axis=-1))
        decay = jnp.exp(bl + m - m_new)
        wk = jnp.exp(g - m_new[..., None])
        C_new = decay[..., None, None] * C + jnp.einsum('bhs,bhsd,bhse->bhde', wk, kc, vc)
        n_new = decay[..., None] * n + jnp.einsum('bhs,bhsd->bhd', wk, kc)
        return (C_new, n_new, m_new), h

    (Cf, nf, mf), hs = lax.scan(step, (C0.astype(F32), n0.astype(F32), m0.astype(F32)), xs)
    hs = jnp.moveaxis(jnp.swapaxes(hs, 2, 3), 0, 1).reshape(B, nc * c, H, D)[:, :L]
    return hs, (Cf, nf, mf)


def _j_compress(x, pe, w1, w2):
    B, T, G, D = x.shape
    nch = T // CMP_STRIDE
    xs = x[:, :nch * CMP_STRIDE].reshape(B, nch, CMP_STRIDE, G, D)
    pe2 = pe.reshape(2, CMP_STRIDE, D)
    w12 = w1.reshape(2, CMP_STRIDE, D, -1)
    first = jnp.einsum('bncgd,cde->bnge', xs + pe2[0][:, None, :], w12[0])
    second = jnp.einsum('bncgd,cde->bnge', xs + pe2[1][:, None, :], w12[1])
    hid = jax.nn.gelu(first[:, :-1] + second[:, 1:])
    return hid @ w2


def _j_cmp_select(q, pos, kc, vc, n_slc):
    s = jnp.einsum('blgrd,bngd->bgrln', q, kc).astype(F32) * SCALE
    n_cmp = kc.shape[1]
    start = jnp.arange(n_cmp, dtype=jnp.int32) * CMP_STRIDE
    mask = (start + CMP_LEN - 1)[None, :] <= pos[:, None]
    p = _j_masked_probs(s, mask)
    o_c = jnp.einsum('bgrln,bngd->blgrd', p.astype(vc.dtype), vc)
    j = jnp.arange(n_slc, dtype=jnp.int32)
    overlap = ((start[:, None] < (j[None, :] + 1) * SLC_BLOCK)
               & (start[:, None] + CMP_LEN > j[None, :] * SLC_BLOCK)).astype(F32)
    imp = jnp.einsum('bgrln,nj->bglj', p, overlap)
    cur = (pos // SLC_BLOCK)[:, None]
    forced = (j[None, :] == 0) | (j[None, :] == cur) | (j[None, :] == cur - 1)
    valid = j[None, :] * SLC_BLOCK <= pos[:, None]
    score = jnp.where(forced, FORCED_SCORE, jnp.where(valid, imp, -1.0))
    _, idx = lax.top_k(score, min(N_SELECT, n_slc))
    return o_c, jnp.transpose(idx, (0, 2, 1, 3))


def _j_sel_attend(q, pos, idx, gather):
    kb, vb = gather(idx)
    B, L, G, K, C, D = kb.shape
    R = q.shape[3]
    s = jnp.einsum('blgrd,blgkcd->blgrkc', q, kb).astype(F32) * SCALE
    kpos = idx[..., None] * SLC_BLOCK + jnp.arange(SLC_BLOCK, dtype=jnp.int32)
    mask = (kpos <= pos[None, :, None, None, None]).reshape(B, L, G, 1, K * C)
    p = _j_masked_probs(s.reshape(B, L, G, R, K * C), mask).astype(vb.dtype)
    return jnp.einsum('blgrn,blgnd->blgrd', p, vb.reshape(B, L, G, K * C, D))


def _j_sel_attend_blocked(q, pos, idx, gather):
    B, L = q.shape[:2]
    qb = SEL_QBLK if L % SEL_QBLK == 0 else L
    nb = L // qb
    if nb == 1:
        return _j_sel_attend(q, pos, idx, gather)
    qs = jnp.swapaxes(q.reshape((B, nb, qb) + q.shape[2:]), 0, 1)
    ps = pos.reshape(nb, qb)
    ids = jnp.swapaxes(idx.reshape((B, nb, qb) + idx.shape[2:]), 0, 1)
    out = lax.map(lambda a: _j_sel_attend(a[0], a[1], a[2], gather), (qs, ps, ids))
    return jnp.swapaxes(out, 0, 1).reshape((B, L) + out.shape[3:])


def _j_window_prompt(q, kw, vw):
    B, S, G, R, D = q.shape
    nb = S // WIN_QBLK
    span = WINDOW + WIN_QBLK
    padw = [(0, 0), (WINDOW, 0), (0, 0), (0, 0)]
    kp, vp = jnp.pad(kw, padw), jnp.pad(vw, padw)
    idx = jnp.arange(nb, dtype=jnp.int32)[:, None] * WIN_QBLK + jnp.arange(span, dtype=jnp.int32)[None, :]
    kb, vb = kp[:, idx], vp[:, idx]
    qb = q.reshape(B, nb, WIN_QBLK, G, R, D)
    s = jnp.einsum('bnqgrd,bnkgd->bngrqk', qb, kb).astype(F32) * SCALE
    kpos = idx - WINDOW
    qpos = jnp.arange(nb, dtype=jnp.int32)[:, None] * WIN_QBLK + jnp.arange(WIN_QBLK, dtype=jnp.int32)[None, :]
    mask = ((kpos[:, None, :] <= qpos[:, :, None]) & (kpos[:, None, :] > qpos[:, :, None] - WINDOW)
            & (kpos[:, None, :] >= 0))
    p = _j_masked_probs(s, mask[None, :, None, None]).astype(vb.dtype)
    return jnp.einsum('bngrqk,bnkgd->bnqgrd', p, vb).reshape(B, S, G, R, D)


def _j_window_sample(q, pos, kw_new, vw_new, kw_buf, vw_buf):
    w_buf = kw_buf.shape[1]
    kk = jnp.concatenate([kw_buf, kw_new], axis=1)
    vv = jnp.concatenate([vw_buf, vw_new], axis=1)
    kpos = PAST_LEN - w_buf + jnp.arange(kk.shape[1], dtype=jnp.int32)
    s = jnp.einsum('blgrd,bkgd->bgrlk', q, kk).astype(F32) * SCALE
    mask = (kpos[None, :] <= pos[:, None]) & (kpos[None, :] > pos[:, None] - WINDOW)
    p = _j_masked_probs(s, mask).astype(vv.dtype)
    return jnp.einsum('bgrlk,bkgd->blgrd', p, vv)


def _j_combine(gates, o_c, o_s, o_w):
    out = gates[..., 0:1] * o_c + gates[..., 1:2] * o_s + gates[..., 2:3] * o_w
    return out.reshape(out.shape[:2] + (-1,))


def _j_nsa_prompt(q, gates, kv, pe_k, w1_k, w2_k, pe_v, w1_v, w2_v):
    kc_raw, vc_raw, ks, vs, kw, vw = kv
    B, L, G = ks.shape[:3]
    pos = jnp.arange(L, dtype=jnp.int32)
    kc = _j_compress(kc_raw, pe_k, w1_k, w2_k)
    vc = _j_compress(vc_raw, pe_v, w1_v, w2_v)
    n_slc = -(-L // SLC_BLOCK)
    o_c, idx = _j_cmp_select(q, pos, kc, vc, n_slc)
    ksb = ks.reshape(B, n_slc, SLC_BLOCK, G, HEAD_DIM)
    vsb = vs.reshape(B, n_slc, SLC_BLOCK, G, HEAD_DIM)
    b_i = jnp.arange(B)[:, None, None, None]
    g_i = jnp.arange(G)[None, None, :, None]
    gather = lambda ix: (ksb[b_i, ix, :, g_i, :], vsb[b_i, ix, :, g_i, :])
    o_s = _j_sel_attend_blocked(q, pos, idx, gather)
    o_w = _j_window_prompt(q, kw, vw)
    return _j_combine(gates, o_c, o_s, o_w)


def _j_nsa_sample(q, gates, kv, cache_cmp_k, cache_cmp_v, cache_slc_k, cache_slc_v, cache_win_k, cache_win_v,
                  page_table, pe_k, w1_k, w2_k, pe_v, w1_v, w2_v):
    kc_new, vc_new, ks_new, vs_new, kw_new, vw_new = kv
    B, L, G = ks_new.shape[:3]
    n_pages = PAST_LEN // PAGE_SIZE
    pos = PAST_LEN + jnp.arange(L, dtype=jnp.int32)

    def paged_rows(pool):
        return pool[page_table].reshape(B, n_pages * PAGE_SIZE, G, HEAD_DIM)

    kc = _j_compress(jnp.concatenate([paged_rows(cache_cmp_k), kc_new], axis=1), pe_k, w1_k, w2_k)
    vc = _j_compress(jnp.concatenate([paged_rows(cache_cmp_v), vc_new], axis=1), pe_v, w1_v, w2_v)
    n_slc = -(-(PAST_LEN + L) // SLC_BLOCK)
    o_c, idx = _j_cmp_select(q, pos, kc, vc, n_slc)
    n_pb = PAST_LEN // SLC_BLOCK
    n_tail = n_slc - n_pb
    bpp = PAGE_SIZE // SLC_BLOCK

    def tail_blocks(x):
        x = jnp.pad(x, ((0, 0), (0, n_tail * SLC_BLOCK - L), (0, 0), (0, 0)))
        return x.reshape(B, n_tail, SLC_BLOCK, G, HEAD_DIM)

    ks_tail, vs_tail = tail_blocks(ks_new), tail_blocks(vs_new)
    ks_pool = cache_slc_k.reshape(-1, bpp, SLC_BLOCK, G, HEAD_DIM)
    vs_pool = cache_slc_v.reshape(-1, bpp, SLC_BLOCK, G, HEAD_DIM)
    b_i = jnp.arange(B)[:, None, None, None]
    g_i = jnp.arange(G)[None, None, :, None]

    def gather(ix):
        in_past = (ix < n_pb)[..., None, None]
        phys = page_table[b_i, jnp.clip(ix // bpp, 0, n_pages - 1)]
        sub = ix % bpp
        jt = jnp.clip(ix - n_pb, 0, n_tail - 1)
        pick = lambda pool, tail: jnp.where(in_past, pool[phys, sub, :, g_i, :], tail[b_i, jt, :, g_i, :])
        return pick(ks_pool, ks_tail), pick(vs_pool, vs_tail)

    o_s = _j_sel_attend_blocked(q, pos, idx, gather)
    o_w = _j_window_sample(q, pos, kw_new, vw_new, cache_win_k, cache_win_v)
    return _j_combine(gates, o_c, o_s, o_w)


def kernel(x_prompt, x_sample, mem_prompt, cache_mem_k, cache_mem_v, state_mlstm_C, state_mlstm_n, state_mlstm_m, cache_cmp_k, cache_cmp_v, cache_slc_k, cache_slc_v, cache_win_k, cache_win_v, page_table, ffn1_norm, ffn1_w_in, ffn1_w_out, ffn2_norm, ffn2_w_in, ffn2_w_out, mix_norm, a_w_in, a_b_i, a_b_f, a_head_norm, b_w_in, mem_norm, w_mem_kv, w_out, kv_norm, w_kv, cmp_pe_k, cmp_w1_k, cmp_w2_k, cmp_pe_v, cmp_w1_v, cmp_w2_v, final_norm):
    G, R, DH = B_KV_GROUPS, B_GROUP, HEAD_DIM
    ffn1_wi, ffn1_wo = ffn1_w_in.astype(BF16), ffn1_w_out.astype(BF16)
    ffn2_wi, ffn2_wo = ffn2_w_in.astype(BF16), ffn2_w_out.astype(BF16)
    w_o = w_out.astype(BF16)
    w_kv_b = w_kv.astype(BF16)
    w_mkv = w_mem_kv.astype(BF16)

    wa = a_w_in[0]
    zc = lambda n: jnp.zeros((D_MODEL, n), F32)
    a_wr = jnp.concatenate([wa[:, :4 * TOK_WIDTH], wa[:, 4 * TOK_WIDTH + 2 * A_HEADS:],
                            wa[:, 4 * TOK_WIDTH:4 * TOK_WIDTH + A_HEADS], zc(16 - A_HEADS),
                            wa[:, 4 * TOK_WIDTH + A_HEADS:4 * TOK_WIDTH + 2 * A_HEADS], zc(128 - 16 - A_HEADS),
                            zc(128)], axis=1).astype(BF16)
    gate_bias = jnp.concatenate([a_b_i[0], jnp.zeros((16 - A_HEADS,), F32), a_b_f[0],
                                 jnp.zeros((128 - 16 - A_HEADS,), F32)]).reshape(1, 128)
    wb = b_w_in[0]
    b_wr = jnp.concatenate([wb[:, :TOK_WIDTH], wb[:, TOK_WIDTH + 3 * B_HEADS:],
                            wb[:, TOK_WIDTH:TOK_WIDTH + 3 * B_HEADS], zc(256 - 3 * B_HEADS)], axis=1).astype(BF16)

    Bp, S = x_prompt.shape[:2]
    Mp = Bp * S
    memx = mem_prompt.reshape(Bp * MEM_TOKENS, D_MODEL)
    memkv = [norm_matmul(memx, mem_norm[l], w_mkv[l], tm=512, tn=512) for l in range(DEPTH)]
    mem5 = jnp.stack(memkv).reshape(DEPTH, Bp, MEM_TOKENS, 2, MEM_HEADS, DH)
    mem_k_p, mem_v_p = mem5[:, :, :, 0], mem5[:, :, :, 1]

    x = x_prompt.reshape(Mp, D_MODEL)
    x = ffn(x, ffn1_norm[0], ffn1_wi[0], ffn1_wo[0], tm=512, tf=512)
    proja = norm_matmul(x, mix_norm[0], a_wr, tm=512, tn=256)
    mix, C_fin, n_fin, m_fin = mlstm_prompt(proja, gate_bias, a_head_norm[0], Bp)
    mo = mem_attn_prompt(proja, 4 * TOK_WIDTH // MEM_WIDTH, memkv[0], Bp)
    x = outproj(x, mix, mo, w_o[0], tm=512)
    x = ffn(x, ffn2_norm[0], ffn2_wi[0], ffn2_wo[0], tm=512, tf=512)
    kvp, kv4 = kv_proj(x, kv_norm, w_kv_b, tm=512)
    x = ffn(x, ffn1_norm[1], ffn1_wi[1], ffn1_wo[1], tm=512, tf=512)
    projb = norm_matmul(x, mix_norm[1], b_wr, tm=512, tn=256)
    kc = compress_prompt(kvp, 0, Bp, cmp_pe_k, cmp_w1_k, cmp_w2_k)
    vc = compress_prompt(kvp, 1, Bp, cmp_pe_v, cmp_w1_v, cmp_w2_v)
    oc, member = cmp_select_prompt(projb, kc, vc, Bp)
    mix = sel_win_combine(projb, kvp, oc, member, Bp)
    mo = mem_attn_prompt(projb, TOK_WIDTH // MEM_WIDTH, memkv[1], Bp)
    x = outproj(x, mix, mo, w_o[1], tm=512)
    y_prompt = ffn(x, ffn2_norm[1], ffn2_wi[1], ffn2_wo[1], final_norm, tm=512, tf=512).reshape(Bp, S, D_MODEL)
    C_p = C_fin[None]
    n_p = n_fin[None, :, :A_HEADS]
    m_p = m_fin[None, :, :A_HEADS, 0]
    kv_p = tuple(kv4[i].reshape(Bp, S, G, DH) for i in range(6))

    Bs = x_sample.shape[0]
    xs = x_sample.reshape(Bs, D_MODEL)
    xs = ffn(xs, ffn1_norm[0], ffn1_wi[0], ffn1_wo[0], tm=Bs, tf=512)
    proja_s = norm_matmul(xs, mix_norm[0], a_wr, tm=Bs, tn=256)
    mix_s, C_new, n_new, m_new = mlstm_sample(proja_s, gate_bias, state_mlstm_m[0], a_head_norm[0],
                                              state_mlstm_C[0], state_mlstm_n[0])
    mo_s = mem_attn_sample(proja_s, 4 * TOK_WIDTH // MEM_WIDTH, cache_mem_k, cache_mem_v, 0)
    xs = outproj(xs, mix_s.reshape(Bs, TOK_WIDTH), mo_s, w_o[0], tm=Bs)
    xs = ffn(xs, ffn2_norm[0], ffn2_wi[0], ffn2_wo[0], tm=Bs, tf=512)
    kvp_s = norm_matmul(xs, kv_norm, w_kv_b, tm=Bs, tn=512)
    xs = ffn(xs, ffn1_norm[1], ffn1_wi[1], ffn1_wo[1], tm=Bs, tf=512)
    projb_s = norm_matmul(xs, mix_norm[1], b_wr, tm=Bs, tn=256)
    kc_s = compress_sample(cache_cmp_k, page_table, cmp_pe_k, cmp_w1_k, cmp_w2_k)
    vc_s = compress_sample(cache_cmp_v, page_table, cmp_pe_v, cmp_w1_v, cmp_w2_v)
    mix_s = nsa_sample(projb_s, kvp_s, kc_s, vc_s, cache_slc_k, cache_slc_v, cache_win_k, cache_win_v, page_table)
    mo_s = mem_attn_sample(projb_s, TOK_WIDTH // MEM_WIDTH, cache_mem_k, cache_mem_v, 1)
    xs = outproj(xs, mix_s, mo_s, w_o[1], tm=Bs)
    y_sample = ffn(xs, ffn2_norm[1], ffn2_wi[1], ffn2_wo[1], final_norm, tm=Bs, tf=512).reshape(Bs, 1, D_MODEL)
    C_s = C_new[None]
    n_s = n_new[None]
    m_s = m_new[None, :, :A_HEADS, 0]
    kvs5 = kvp_s.reshape(Bs, 1, 6, G, DH)
    kv_s = tuple(kvs5[:, :, i] for i in range(6))

    w_p = min(WINDOW, S)
    cmp_k_p, cmp_v_p, slc_k_p, slc_v_p, win_k_all, win_v_all = kv_p
    win_k_p = win_k_all[:, S - w_p:]
    win_v_p = win_v_all[:, S - w_p:]
    cmp_k_s, cmp_v_s, slc_k_s, slc_v_s, win_k_s, win_v_s = kv_s
    return (y_prompt, y_sample, mem_k_p, mem_v_p, C_p, n_p, m_p, C_s, n_s, m_s,
            cmp_k_p, cmp_v_p, slc_k_p, slc_v_p, win_k_p, win_v_p,
            cmp_k_s, cmp_v_s, slc_k_s, slc_v_s, win_k_s, win_v_s)
```

```python
import functools

import jax
import jax.numpy as jnp
from jax import lax
from jax.experimental import pallas as pl
from jax.experimental.pallas import tpu as pltpu

D_MODEL = 2048
DEPTH = 2
PAST_LEN = 2048
PAGE_SIZE = 128
HEAD_DIM = 128
N_A_LAYERS = DEPTH // 2
MEM_TOKENS = 256
MEM_HEADS = 4
MEM_WIDTH = MEM_HEADS * HEAD_DIM
TOK_WIDTH = D_MODEL - MEM_WIDTH
A_HEADS = TOK_WIDTH // HEAD_DIM
B_HEADS = TOK_WIDTH // HEAD_DIM
B_KV_GROUPS = 4
B_GROUP = B_HEADS // B_KV_GROUPS
D_FF = ((8 * D_MODEL // 3 + 255) // 256) * 256
MLSTM_CHUNK = 128
CMP_STRIDE = 16
CMP_LEN = 2 * CMP_STRIDE
SLC_BLOCK = 64
N_SELECT = 16
WINDOW = 512
WIN_QBLK = 128
SEL_QBLK = 64
FORCED_SCORE = 1e4
NEG = -1e30
EPS = 1e-6
SCALE = HEAD_DIM ** -0.5

F32 = jnp.float32
BF16 = jnp.bfloat16
VMEM_LIMIT = 56 * 1024 * 1024


def _cparams(*sem):
    return pltpu.CompilerParams(dimension_semantics=sem, vmem_limit_bytes=VMEM_LIMIT)


def _rms_bf16(x, g):
    ms = jnp.mean(x * x, axis=-1, keepdims=True)
    return (x * lax.rsqrt(ms + EPS) * g).astype(BF16)


def _norm_matmul_kernel(x_ref, g_ref, w_ref, o_ref, h_ref):
    @pl.when(pl.program_id(1) == 0)
    def _():
        h_ref[...] = _rms_bf16(x_ref[...], g_ref[...])

    o_ref[...] = jnp.dot(h_ref[...], w_ref[...], preferred_element_type=F32).astype(o_ref.dtype)


def norm_matmul(x, g, w, *, tm, tn, out_dtype=F32):
    M, D = x.shape
    N = w.shape[1]
    return pl.pallas_call(
        _norm_matmul_kernel,
        grid=(M // tm, N // tn),
        in_specs=[pl.BlockSpec((tm, D), lambda i, j: (i, 0)),
                  pl.BlockSpec((1, D), lambda i, j: (0, 0)),
                  pl.BlockSpec((D, tn), lambda i, j: (0, j))],
        out_specs=pl.BlockSpec((tm, tn), lambda i, j: (i, j)),
        out_shape=jax.ShapeDtypeStruct((M, N), out_dtype),
        scratch_shapes=[pltpu.VMEM((tm, D), BF16)],
        compiler_params=_cparams("parallel", "arbitrary"),
        name="norm_matmul",
    )(x, g.reshape(1, D), w)


def _kv_proj_kernel(x_ref, g_ref, w_ref, o_ref, o4_ref, h_ref):
    @pl.when(pl.program_id(1) == 0)
    def _():
        h_ref[...] = _rms_bf16(x_ref[...], g_ref[...])

    y = jnp.dot(h_ref[...], w_ref[...], preferred_element_type=F32)
    o_ref[...] = y
    tm = y.shape[0]
    for g in range(B_KV_GROUPS):
        o4_ref[pl.ds(g, tm, stride=B_KV_GROUPS), :] = y[:, g * HEAD_DIM:(g + 1) * HEAD_DIM]


def kv_proj(x, g, w, *, tm):
    M, D = x.shape
    GD = B_KV_GROUPS * HEAD_DIM
    nbr = w.shape[1] // GD
    return pl.pallas_call(
        _kv_proj_kernel,
        grid=(M // tm, nbr),
        in_specs=[pl.BlockSpec((tm, D), lambda i, j: (i, 0)),
                  pl.BlockSpec((1, D), lambda i, j: (0, 0)),
                  pl.BlockSpec((D, GD), lambda i, j: (0, j))],
        out_specs=[pl.BlockSpec((tm, GD), lambda i, j: (i, j)),
                   pl.BlockSpec((None, tm * B_KV_GROUPS, HEAD_DIM), lambda i, j: (j, i, 0))],
        out_shape=[jax.ShapeDtypeStruct((M, nbr * GD), F32),
                   jax.ShapeDtypeStruct((nbr, M * B_KV_GROUPS, HEAD_DIM), F32)],
        scratch_shapes=[pltpu.VMEM((tm, D), BF16)],
        compiler_params=_cparams("parallel", "arbitrary"),
        name="kv_proj",
    )(x, g.reshape(1, D), w)


def _ffn_kernel(x_ref, g_ref, wg_ref, wu_ref, wo_ref, fg_ref, o_ref, h_ref, acc_ref, *, final_norm):
    f = pl.program_id(1)

    @pl.when(f == 0)
    def _():
        h_ref[...] = _rms_bf16(x_ref[...], g_ref[...])
        acc_ref[...] = jnp.zeros_like(acc_ref)

    h = h_ref[...]
    gate = jnp.dot(h, wg_ref[...], preferred_element_type=F32)
    up = jnp.dot(h, wu_ref[...], preferred_element_type=F32)
    act = (gate * jax.nn.sigmoid(gate) * up).astype(BF16)
    acc_ref[...] += jnp.dot(act, wo_ref[...], preferred_element_type=F32)

    @pl.when(f == pl.num_programs(1) - 1)
    def _():
        y = x_ref[...] + 0.5 * acc_ref[...]
        if final_norm:
            ms = jnp.mean(y * y, axis=-1, keepdims=True)
            y = y * lax.rsqrt(ms + EPS) * fg_ref[...]
        o_ref[...] = y


def ffn(x, g, w_in, w_out, layer, final_gain=None, *, tm, tf):
    M, D = x.shape
    F = w_out.shape[1]
    nf = F // tf
    fg = jnp.ones((D,), F32) if final_gain is None else final_gain
    return pl.pallas_call(
        functools.partial(_ffn_kernel, final_norm=final_gain is not None),
        grid=(M // tm, nf),
        in_specs=[pl.BlockSpec((tm, D), lambda i, f: (i, 0)),
                  pl.BlockSpec((1, D), lambda i, f: (0, 0)),
                  pl.BlockSpec((None, D, tf), lambda i, f: (layer, 0, f)),
                  pl.BlockSpec((None, D, tf), lambda i, f: (layer, 0, f + nf)),
                  pl.BlockSpec((None, tf, D), lambda i, f: (layer, f, 0)),
                  pl.BlockSpec((1, D), lambda i, f: (0, 0))],
        out_specs=pl.BlockSpec((tm, D), lambda i, f: (i, 0)),
        out_shape=jax.ShapeDtypeStruct((M, D), F32),
        scratch_shapes=[pltpu.VMEM((tm, D), BF16), pltpu.VMEM((tm, D), F32)],
        compiler_params=_cparams("parallel", "arbitrary"),
        name="ffn",
    )(x, g.reshape(1, D), w_in, w_in, w_out, fg.reshape(1, D))


def _outproj_kernel(x_ref, mix_ref, mo_ref, wa_ref, wb_ref, o_ref):
    y = jnp.dot(mix_ref[...].astype(BF16), wa_ref[...], preferred_element_type=F32)
    y += jnp.dot(mo_ref[...].astype(BF16), wb_ref[...], preferred_element_type=F32)
    o_ref[...] = x_ref[...] + y


def outproj(x, mix, mo, w, layer, *, tm):
    M, D = x.shape
    Ka, Kb = mix.shape[1], mo.shape[1]
    assert Ka % Kb == 0
    return pl.pallas_call(
        _outproj_kernel,
        grid=(M // tm,),
        in_specs=[pl.BlockSpec((tm, D), lambda i: (i, 0)),
                  pl.BlockSpec((tm, Ka), lambda i: (i, 0)),
                  pl.BlockSpec((tm, Kb), lambda i: (i, 0)),
                  pl.BlockSpec((None, Ka, D), lambda i: (layer, 0, 0)),
                  pl.BlockSpec((None, Kb, D), lambda i: (layer, Ka // Kb, 0))],
        out_specs=pl.BlockSpec((tm, D), lambda i: (i, 0)),
        out_shape=jax.ShapeDtypeStruct((M, D), F32),
        compiler_params=_cparams("parallel"),
        name="outproj",
    )(x, mix, mo, w, w)


def _compress_kernel(*refs, n_pages, cpp, paged):
    page_refs = refs[:n_pages]
    pe_ref, w1_ref, w2_ref, o_ref = refs[n_pages:]
    G, DH = B_KV_GROUPS, HEAD_DIM
    nch = n_pages * cpp

    def piece(p, c, g):
        if paged:
            return page_refs[p][pl.ds(c * G + g, cpp, stride=CMP_STRIDE * G), :]
        return page_refs[p][:, c, g * DH:(g + 1) * DH]

    lhs = jnp.concatenate(
        [jnp.concatenate([piece(p, c, g) for g in range(G) for p in range(n_pages)] + [pe_ref[c]], axis=0).astype(BF16)
         for c in range(CMP_STRIDE)], axis=1)
    acc = jnp.dot(lhs, w1_ref[...], preferred_element_type=F32)
    bias_first = acc[G * nch:G * nch + 1, :DH]
    bias_second = acc[G * nch + 1:G * nch + 2, DH:]
    for g in range(G):
        first = acc[g * nch:(g + 1) * nch, :DH] + bias_first
        second = acc[g * nch:(g + 1) * nch, DH:] + bias_second
        hid = jax.nn.gelu(first + pltpu.roll(second, nch - 1, 0))
        o_ref[g] = jnp.dot(hid.astype(BF16), w2_ref[...], preferred_element_type=F32)


def _compress_weights(pe, w1, w2):
    DH = HEAD_DIM
    pe2 = pe.reshape(2, CMP_STRIDE, DH)
    pe_rows = jnp.concatenate([pe2[0][:, None], pe2[1][:, None], jnp.zeros((CMP_STRIDE, 6, DH), F32)], axis=1)
    w12 = w1.reshape(2, CMP_STRIDE, DH, -1)
    w1cat = jnp.concatenate([w12[0], w12[1]], axis=-1).astype(BF16)
    return pe_rows, w1cat.reshape(CMP_STRIDE * DH, -1), w2.astype(BF16)


def compress_prompt(kvp, col_block, n_batch, pe, w1, w2):
    M = kvp.shape[0]
    nch = M // n_batch // CMP_STRIDE
    pe_rows, w1cat, w2b = _compress_weights(pe, w1, w2)
    x = kvp.reshape(M // CMP_STRIDE, CMP_STRIDE, kvp.shape[1])
    return pl.pallas_call(
        functools.partial(_compress_kernel, n_pages=1, cpp=nch, paged=False),
        grid=(n_batch,),
        in_specs=[pl.BlockSpec((nch, CMP_STRIDE, 512), lambda b: (b, 0, col_block)),
                  pl.BlockSpec((CMP_STRIDE, 8, HEAD_DIM), lambda b: (0, 0, 0)),
                  pl.BlockSpec((CMP_STRIDE * HEAD_DIM, 2 * HEAD_DIM), lambda b: (0, 0)),
                  pl.BlockSpec((HEAD_DIM, HEAD_DIM), lambda b: (0, 0))],
        out_specs=pl.BlockSpec((None, B_KV_GROUPS, nch, HEAD_DIM), lambda b: (b, 0, 0, 0)),
        out_shape=jax.ShapeDtypeStruct((n_batch, B_KV_GROUPS, nch, HEAD_DIM), F32),
        compiler_params=_cparams("parallel"),
        name="compress_prompt",
    )(x, pe_rows, w1cat, w2b)


_NT = (((1,), (1,)), ((), ()))


def _split2(x):
    hi = x.astype(BF16)
    return hi, (x - hi.astype(F32)).astype(BF16)


def _cmp_select_kernel(q_ref, kc_ref, vc_ref, oc_ref, mem_ref, sc_ref, *, T, nch, nsl, pos0):
    G, R, DH = B_KV_GROUPS, B_GROUP, HEAD_DIM
    base = pos0 + pl.program_id(1) * T
    pos_r = base + lax.broadcasted_iota(jnp.int32, (T, nch), 0)
    n_c = lax.broadcasted_iota(jnp.int32, (T, nch), 1)
    cmask = (n_c * CMP_STRIDE + CMP_LEN - 1) <= pos_r
    j_o = lax.broadcasted_iota(jnp.int32, (nsl, nch), 0)
    st_o = lax.broadcasted_iota(jnp.int32, (nsl, nch), 1) * CMP_STRIDE
    ovl_t = jnp.where((st_o < (j_o + 1) * SLC_BLOCK) & (st_o + CMP_LEN > j_o * SLC_BLOCK), 1.0, 0.0).astype(BF16)
    j_s = lax.broadcasted_iota(jnp.int32, (nsl, T), 0)
    pos_s = base + lax.broadcasted_iota(jnp.int32, (nsl, T), 1)
    cur = pos_s // SLC_BLOCK
    forced = (j_s == 0) | (j_s == cur) | (j_s == cur - 1)
    valid = j_s * SLC_BLOCK <= pos_s
    for g in range(G):
        kcg = kc_ref[g].astype(BF16)
        vcg = vc_ref[g].astype(BF16)
        psum = jnp.zeros((T, nch), F32)
        for r in range(R):
            h = g * R + r
            qh = q_ref[:, h * DH:(h + 1) * DH].astype(BF16)
            s = lax.dot_general(qh, kcg, _NT, preferred_element_type=F32) * SCALE
            sm = jnp.where(cmask, s, NEG)
            e = jnp.exp(sm - jnp.max(sm, axis=-1, keepdims=True))
            p = jnp.where(cmask, e / jnp.sum(e, axis=-1, keepdims=True), 0.0)
            oc_ref[:, h * DH:(h + 1) * DH] = jnp.dot(p.astype(BF16), vcg, preferred_element_type=F32).astype(oc_ref.dtype)
            psum = psum + p
        p_hi, p_lo = _split2(psum)
        imp_t = (lax.dot_general(ovl_t, p_hi, _NT, preferred_element_type=F32)
                 + lax.dot_general(ovl_t, p_lo, _NT, preferred_element_type=F32))
        score = jnp.where(forced, FORCED_SCORE, jnp.where(valid, imp_t, -1.0))
        sc_ref[...] = score
        rank = jnp.zeros((nsl, T), F32)
        for jp in range(nsl):
            row = sc_ref[jp:jp + 1, :]
            beats = (row > score) | ((row == score) & (j_s > jp))
            rank = rank + jnp.where(beats, 1.0, 0.0)
        mem_t = jnp.where(rank < min(N_SELECT, nsl), 1.0, 0.0)
        if nsl < 128:
            mem_t = jnp.concatenate([mem_t, jnp.zeros((128 - nsl, T), F32)], axis=0)
        mem_ref[g] = mem_t.T.astype(mem_ref.dtype)


def cmp_select_prompt(projb, kc, vc, n_batch, *, T=256):
    M = projb.shape[0]
    L = M // n_batch
    nt, nch, nsl = L // T, L // CMP_STRIDE, L // SLC_BLOCK
    return pl.pallas_call(
        functools.partial(_cmp_select_kernel, T=T, nch=nch, nsl=nsl, pos0=0),
        grid=(n_batch, nt),
        in_specs=[pl.BlockSpec((T, TOK_WIDTH), lambda b, i: (b * nt + i, 0)),
                  pl.BlockSpec((None, B_KV_GROUPS, nch, HEAD_DIM), lambda b, i: (b, 0, 0, 0)),
                  pl.BlockSpec((None, B_KV_GROUPS, nch, HEAD_DIM), lambda b, i: (b, 0, 0, 0))],
        out_specs=[pl.BlockSpec((T, TOK_WIDTH), lambda b, i: (b * nt + i, 0)),
                   pl.BlockSpec((None, B_KV_GROUPS, T, 128), lambda b, i: (b, 0, i, 0))],
        out_shape=[jax.ShapeDtypeStruct((M, TOK_WIDTH), BF16),
                   jax.ShapeDtypeStruct((n_batch, B_KV_GROUPS, L, 128), BF16)],
        scratch_shapes=[pltpu.VMEM((nsl, T), F32)],
        compiler_params=_cparams("parallel", "parallel"),
        name="cmp_select_prompt",
    )(projb, kc, vc)


def _sel_win_kernel(q_ref, ks_ref, vs_ref, kw0_ref, kw1_ref, kw2_ref, vw0_ref, vw1_ref, vw2_ref, oc_ref, mem_ref,
                    gt_ref, mix_ref, *, T, CK):
    R, DH = B_GROUP, HEAD_DIM
    g = pl.program_id(1)
    i = pl.program_id(2)
    q3 = jnp.concatenate([q_ref[:, r * DH:(r + 1) * DH] for r in range(R)], axis=0)
    q3 = (q3 * SCALE).astype(BF16)
    qpos = i * T + lax.broadcasted_iota(jnp.int32, (T, CK), 0)
    memb = mem_ref[...]
    j_e = lax.broadcasted_iota(jnp.int32, (128, CK), 0)
    k_e = lax.broadcasted_iota(jnp.int32, (128, CK), 1)
    n_chunks = (i * T + T + CK - 1) // CK

    def chunk(c, carry):
        m, l, acc = carry
        k0 = pl.multiple_of(c * CK, CK)
        kc = ks_ref[pl.ds(k0, CK), :].astype(BF16)
        vc = vs_ref[pl.ds(k0, CK), :].astype(BF16)
        expand = jnp.where((k0 + k_e) // SLC_BLOCK == j_e, 1.0, 0.0).astype(BF16)
        sel = jnp.dot(memb, expand, preferred_element_type=F32) > 0.5
        ok = sel & ((k0 + lax.broadcasted_iota(jnp.int32, (T, CK), 1)) <= qpos)
        bias = jnp.where(ok, 0.0, NEG)
        s = lax.dot_general(q3, kc, _NT, preferred_element_type=F32) + jnp.concatenate([bias] * R, axis=0)
        m_new = jnp.maximum(m, jnp.max(s, axis=-1, keepdims=True))
        alpha = jnp.exp(m - m_new)
        p = jnp.exp(s - m_new)
        l = alpha * l + jnp.sum(p, axis=-1, keepdims=True)
        acc = alpha * acc + jnp.dot(p.astype(BF16), vc, preferred_element_type=F32)
        return m_new, l, acc

    m0 = jnp.full((R * T, 1), NEG, F32)
    _, l, acc = lax.fori_loop(0, n_chunks, chunk, (m0, jnp.zeros((R * T, 1), F32), jnp.zeros((R * T, DH), F32)))
    o_s = acc / l

    kw = jnp.concatenate([kw0_ref[...], kw1_ref[...], kw2_ref[...]], axis=0).astype(BF16)
    vw = jnp.concatenate([vw0_ref[...], vw1_ref[...], vw2_ref[...]], axis=0).astype(BF16)
    qp = i * T + lax.broadcasted_iota(jnp.int32, (T, 3 * T), 0)
    kp = (i - 2) * T + lax.broadcasted_iota(jnp.int32, (T, 3 * T), 1)
    biasw = jnp.where((kp <= qp) & (kp > qp - WINDOW) & (kp >= 0), 0.0, NEG)
    sw = lax.dot_general(q3, kw, _NT, preferred_element_type=F32) + jnp.concatenate([biasw] * R, axis=0)
    ew = jnp.exp(sw - jnp.max(sw, axis=-1, keepdims=True))
    pw = ew / jnp.sum(ew, axis=-1, keepdims=True)
    o_w = jnp.dot(pw.astype(BF16), vw, preferred_element_type=F32)

    gates = jax.nn.sigmoid(gt_ref[...])
    for r in range(R):
        c0 = r * 3
        gsel = _gate_cols(gates, g, c0)
        out = (gsel[0] * oc_ref[:, r * DH:(r + 1) * DH].astype(F32)
               + gsel[1] * o_s[r * T:(r + 1) * T] + gsel[2] * o_w[r * T:(r + 1) * T])
        mix_ref[:, r * DH:(r + 1) * DH] = out.astype(mix_ref.dtype)


def _gate_cols(gates, g, c0):
    lane = lax.broadcasted_iota(jnp.int32, gates.shape, 1)
    tgt = g * (B_GROUP * 3) + c0
    return [jnp.sum(jnp.where(lane == tgt + k, gates, 0.0), axis=-1, keepdims=True) for k in range(3)]


def sel_win_combine(projb, kvp, oc, mem, n_batch, *, T=256, CK=512):
    M = projb.shape[0]
    L = M // n_batch
    nt = L // T
    GW = B_GROUP * HEAD_DIM
    kcol = lambda base: (lambda b, g, i: (b, base + g))
    wspec = lambda base, d: pl.BlockSpec((T, HEAD_DIM), lambda b, g, i: (b * nt + jnp.maximum(i - d, 0), base + g))
    return pl.pallas_call(
        functools.partial(_sel_win_kernel, T=T, CK=CK),
        grid=(n_batch, B_KV_GROUPS, nt),
        in_specs=[pl.BlockSpec((T, GW), lambda b, g, i: (b * nt + i, g)),
                  pl.BlockSpec((L, HEAD_DIM), kcol(8)),
                  pl.BlockSpec((L, HEAD_DIM), kcol(12)),
                  wspec(16, 2), wspec(16, 1), wspec(16, 0),
                  wspec(20, 2), wspec(20, 1), wspec(20, 0),
                  pl.BlockSpec((T, GW), lambda b, g, i: (b * nt + i, g)),
                  pl.BlockSpec((None, None, T, 128), lambda b, g, i: (b, g, i, 0)),
                  pl.BlockSpec((T, 128), lambda b, g, i: (b * nt + i, 16))],
        out_specs=pl.BlockSpec((T, GW), lambda b, g, i: (b * nt + i, g)),
        out_shape=jax.ShapeDtypeStruct((M, TOK_WIDTH), BF16),
        compiler_params=_cparams("parallel", "parallel", "arbitrary"),
        name="sel_win_combine",
    )(projb, kvp, kvp, kvp, kvp, kvp, kvp, kvp, kvp, oc, mem, projb)


def _mem_attn_kernel(q_ref, k_ref, v_ref, o_ref):
    DH = HEAD_DIM
    for h in range(MEM_HEADS):
        qh = (q_ref[:, h * DH:(h + 1) * DH] * SCALE).astype(BF16)
        kh = k_ref[:, h * DH:(h + 1) * DH].astype(BF16)
        vh = v_ref[:, h * DH:(h + 1) * DH].astype(BF16)
        s = lax.dot_general(qh, kh, _NT, preferred_element_type=F32)
        e = jnp.exp(s - jnp.max(s, axis=-1, keepdims=True))
        p = e / jnp.sum(e, axis=-1, keepdims=True)
        o_ref[:, h * DH:(h + 1) * DH] = jnp.dot(p.astype(BF16), vh, preferred_element_type=F32).astype(o_ref.dtype)


def mem_attn_prompt(proj, q_block, memkv, n_batch, *, T=512):
    M = proj.shape[0]
    nt = M // n_batch // T
    return pl.pallas_call(
        _mem_attn_kernel,
        grid=(n_batch, nt),
        in_specs=[pl.BlockSpec((T, MEM_WIDTH), lambda b, i: (b * nt + i, q_block)),
                  pl.BlockSpec((MEM_TOKENS, MEM_WIDTH), lambda b, i: (b, 0)),
                  pl.BlockSpec((MEM_TOKENS, MEM_WIDTH), lambda b, i: (b, 1))],
        out_specs=pl.BlockSpec((T, MEM_WIDTH), lambda b, i: (b * nt + i, 0)),
        out_shape=jax.ShapeDtypeStruct((M, MEM_WIDTH), BF16),
        compiler_params=_cparams("parallel", "parallel"),
        name="mem_attn_prompt",
    )(proj, memkv, memkv)


def _log_sigmoid(x):
    return jnp.minimum(x, 0.0) - jnp.log1p(jnp.exp(-jnp.abs(x)))


def _mlstm_kernel(q_ref, k_ref, v_ref, o_ref, gt_ref, gb_ref, hg_ref, mix_ref, c_ref, n_ref, m_ref):
    H, DH, CL = A_HEADS, HEAD_DIM, MLSTM_CHUNK

    @pl.when(pl.program_id(1) == 0)
    def _():
        c_ref[...] = jnp.zeros_like(c_ref)
        n_ref[...] = jnp.zeros_like(n_ref)
        m_ref[...] = jnp.zeros_like(m_ref)

    gt = (gt_ref[...] + gb_ref[...]).T
    ig = gt[0:16]
    lf = _log_sigmoid(gt[16:32])
    s_r = lax.broadcasted_iota(jnp.int32, (CL, CL), 0)
    s_c = lax.broadcasted_iota(jnp.int32, (CL, CL), 1)
    upper = jnp.where(s_r <= s_c, 1.0, 0.0).astype(BF16)
    lf_hi = lf.astype(BF16)
    r1 = lf - lf_hi.astype(F32)
    lf_mid = r1.astype(BF16)
    lf_lo = (r1 - lf_mid.astype(F32)).astype(BF16)
    b_rows = (jnp.dot(lf_hi, upper, preferred_element_type=F32) + jnp.dot(lf_mid, upper, preferred_element_type=F32)
              + jnp.dot(lf_lo, upper, preferred_element_type=F32))
    u_rows = ig - b_rows
    bl_all = jnp.sum(lf, axis=-1, keepdims=True)
    tril = s_c <= s_r
    for h in range(H):
        b_row = b_rows[h:h + 1]
        u_row = u_rows[h:h + 1]
        bl = bl_all[h:h + 1]
        m_h = m_ref[h:h + 1, :]
        b_col = jnp.broadcast_to(b_row, (CL, CL)).T
        dlog = jnp.where(tril, b_col + u_row, -jnp.inf)
        inter = b_col[:, 0:1] + m_h[:, 0:1]
        mt = jnp.maximum(inter, jnp.max(dlog, axis=-1, keepdims=True))
        a = jnp.exp(inter - mt)
        qh = q_ref[:, h * DH:(h + 1) * DH]
        kh = k_ref[:, h * DH:(h + 1) * DH] * SCALE
        qb, kb, vb = qh.astype(BF16), kh.astype(BF16), v_ref[:, h * DH:(h + 1) * DH].astype(BF16)
        w = jnp.exp(dlog - mt) * lax.dot_general(qb, kb, _NT, preferred_element_type=F32)
        c_old = c_ref[h]
        num = a * jnp.dot(qb, c_old.astype(BF16), preferred_element_type=F32) + jnp.dot(w.astype(BF16), vb, preferred_element_type=F32)
        n_old = n_ref[h:h + 1, :]
        den = a * jnp.sum(qh * n_old, axis=-1, keepdims=True) + jnp.sum(w, axis=-1, keepdims=True)
        hval = num / jnp.maximum(jnp.abs(den), jnp.exp(-mt))
        g_row = bl + u_row
        m_new = jnp.maximum(bl + m_h, jnp.max(g_row, axis=-1, keepdims=True))
        decay = jnp.exp(bl + m_h - m_new)
        wk_row = jnp.exp(g_row - m_new)
        k_t = kh.T
        c_ref[h] = decay[:, 0:1] * c_old + jnp.dot((k_t * wk_row).astype(BF16), vb, preferred_element_type=F32)
        wk8 = jnp.broadcast_to(wk_row, (8, CL)).astype(BF16)
        n_ref[h:h + 1, :] = decay * n_old + jnp.dot(wk8, kb, preferred_element_type=F32)[0:1]
        m_ref[h:h + 1, :] = m_new
        hn = hval * lax.rsqrt(jnp.mean(hval * hval, axis=-1, keepdims=True) + EPS) * hg_ref[:, h * DH:(h + 1) * DH]
        mix_ref[:, h * DH:(h + 1) * DH] = (jax.nn.sigmoid(o_ref[:, h * DH:(h + 1) * DH]) * hn).astype(mix_ref.dtype)


def mlstm_prompt(proja, gate_bias, head_gain, n_batch):
    M = proja.shape[0]
    nc = M // n_batch // MLSTM_CHUNK
    blk = lambda j: pl.BlockSpec((MLSTM_CHUNK, TOK_WIDTH), lambda b, c: (b * nc + c, j))
    return pl.pallas_call(
        _mlstm_kernel,
        grid=(n_batch, nc),
        in_specs=[blk(0), blk(1), blk(2), blk(3),
                  pl.BlockSpec((MLSTM_CHUNK, 128), lambda b, c: (b * nc + c, (4 * TOK_WIDTH + MEM_WIDTH) // 128)),
                  pl.BlockSpec((1, 128), lambda b, c: (0, 0)),
                  pl.BlockSpec((1, TOK_WIDTH), lambda b, c: (0, 0))],
        out_specs=[pl.BlockSpec((MLSTM_CHUNK, TOK_WIDTH), lambda b, c: (b * nc + c, 0)),
                   pl.BlockSpec((None, A_HEADS, HEAD_DIM, HEAD_DIM), lambda b, c: (b, 0, 0, 0)),
                   pl.BlockSpec((None, 16, HEAD_DIM), lambda b, c: (b, 0, 0)),
                   pl.BlockSpec((None, 16, 128), lambda b, c: (b, 0, 0))],
        out_shape=[jax.ShapeDtypeStruct((M, TOK_WIDTH), BF16),
                   jax.ShapeDtypeStruct((n_batch, A_HEADS, HEAD_DIM, HEAD_DIM), F32),
                   jax.ShapeDtypeStruct((n_batch, 16, HEAD_DIM), F32),
                   jax.ShapeDtypeStruct((n_batch, 16, 128), F32)],
        compiler_params=_cparams("parallel", "arbitrary"),
        name="mlstm_prompt",
    )(proja, proja, proja, proja, proja, gate_bias, head_gain.reshape(1, TOK_WIDTH))


def _lane_bcast_rows(row):
    return jnp.broadcast_to(row, (128, 128)).T


def _mlstm_sample_kernel(p_ref, gb_ref, m0_ref, hg_ref, c0_ref, n0_ref, mix_ref, c_ref, n_ref, m_ref):
    H, DH, W = A_HEADS, HEAD_DIM, TOK_WIDTH
    gcol = 4 * W + MEM_WIDTH
    gates = _lane_bcast_rows(p_ref[:, gcol:gcol + 128] + gb_ref[...])
    m_all = _lane_bcast_rows(m0_ref[...])
    m_ref[...] = jnp.zeros_like(m_ref)
    for h in range(H):
        ig = gates[h:h + 1]
        lf = _log_sigmoid(gates[16 + h:17 + h])
        m_old = m_all[h:h + 1]
        m_new = jnp.maximum(lf + m_old, ig)
        a = jnp.exp(lf + m_old - m_new)
        wgt = jnp.exp(ig - m_new)
        q = p_ref[:, h * DH:(h + 1) * DH]
        k = p_ref[:, W + h * DH:W + (h + 1) * DH] * SCALE
        v = p_ref[:, 2 * W + h * DH:2 * W + (h + 1) * DH]
        o = p_ref[:, 3 * W + h * DH:3 * W + (h + 1) * DH]
        c_old = c0_ref[h]
        n_old = n0_ref[h:h + 1, :]
        q_col = jnp.broadcast_to(q, (DH, DH)).T
        k_col = jnp.broadcast_to(k, (DH, DH)).T
        wqk = wgt * jnp.sum(q * k, axis=-1, keepdims=True)
        num = a * jnp.sum(q_col * c_old, axis=0, keepdims=True) + wqk * v
        den = a * jnp.sum(q * n_old, axis=-1, keepdims=True) + wqk
        hval = num / jnp.maximum(jnp.abs(den), jnp.exp(-m_new))
        c_ref[h] = a * c_old + (wgt * k_col) * v
        n_ref[h:h + 1, :] = a * n_old + wgt * k
        m_ref[h:h + 1, :] = m_new
        hn = hval * lax.rsqrt(jnp.mean(hval * hval, axis=-1, keepdims=True) + EPS) * hg_ref[:, h * DH:(h + 1) * DH]
        mix_ref[:, h * DH:(h + 1) * DH] = (jax.nn.sigmoid(o) * hn).astype(mix_ref.dtype)


def mlstm_sample(proja, gate_bias, m0, head_gain, c0, n0):
    Bs, N = proja.shape
    m0p = jnp.pad(m0, ((0, 0), (0, 128 - A_HEADS))).reshape(Bs, 1, 128)
    row = lambda n: pl.BlockSpec((None, 1, n), lambda b: (b, 0, 0))
    const = lambda n: pl.BlockSpec((1, n), lambda b: (0, 0))
    cspec = pl.BlockSpec((None, A_HEADS, HEAD_DIM, HEAD_DIM), lambda b: (b, 0, 0, 0))
    nspec = pl.BlockSpec((None, A_HEADS, HEAD_DIM), lambda b: (b, 0, 0))
    return pl.pallas_call(
        _mlstm_sample_kernel,
        grid=(Bs,),
        in_specs=[row(N), const(128), row(128), const(TOK_WIDTH), cspec, nspec],
        out_specs=[row(TOK_WIDTH), cspec, nspec, pl.BlockSpec((None, 16, 128), lambda b: (b, 0, 0))],
        out_shape=[jax.ShapeDtypeStruct((Bs, 1, TOK_WIDTH), BF16),
                   jax.ShapeDtypeStruct(c0.shape, F32), jax.ShapeDtypeStruct(n0.shape, F32),
                   jax.ShapeDtypeStruct((Bs, 16, 128), F32)],
        compiler_params=_cparams("parallel"),
        name="mlstm_sample",
    )(proja.reshape(Bs, 1, N), gate_bias, m0p, head_gain.reshape(1, TOK_WIDTH), c0, n0)


def _mem_attn_sample_kernel(q_ref, k_ref, v_ref, o_ref, *, nb):
    H, DH = MEM_HEADS, HEAD_DIM
    nrow = k_ref.shape[1]
    row_h = lax.broadcasted_iota(jnp.int32, (8, nrow), 0)
    col_h = lax.broadcasted_iota(jnp.int32, (8, nrow), 1) % H
    bias = jnp.where(row_h == col_h, 0.0, NEG)
    for b in range(nb):
        q8 = jnp.concatenate([q_ref[b:b + 1, h * DH:(h + 1) * DH] for h in range(H)] + [jnp.zeros((8 - H, DH), F32)], axis=0)
        s = lax.dot_general((q8 * SCALE).astype(BF16), k_ref[b].astype(BF16), _NT, preferred_element_type=F32) + bias
        e = jnp.exp(s - jnp.max(s, axis=-1, keepdims=True))
        p = e / jnp.sum(e, axis=-1, keepdims=True)
        o = jnp.dot(p.astype(BF16), v_ref[b].astype(BF16), preferred_element_type=F32)
        o_ref[b * H:(b + 1) * H, :] = o[0:H]


def mem_attn_sample(proj, q_block, cache_k, cache_v, layer, *, nb=8):
    Bs, N = proj.shape
    nrow = MEM_TOKENS * MEM_HEADS
    k2 = cache_k.reshape(cache_k.shape[0], Bs, nrow, HEAD_DIM)
    v2 = cache_v.reshape(cache_v.shape[0], Bs, nrow, HEAD_DIM)
    cspec = pl.BlockSpec((None, nb, nrow, HEAD_DIM), lambda i: (layer, i, 0, 0))
    return pl.pallas_call(
        functools.partial(_mem_attn_sample_kernel, nb=nb),
        grid=(Bs // nb,),
        in_specs=[pl.BlockSpec((None, nb, MEM_WIDTH), lambda i: (i, 0, q_block)), cspec, cspec],
        out_specs=pl.BlockSpec((nb * MEM_HEADS, HEAD_DIM), lambda i: (i, 0)),
        out_shape=jax.ShapeDtypeStruct((Bs * MEM_HEADS, HEAD_DIM), F32),
        compiler_params=_cparams("parallel"),
        name="mem_attn_sample",
    )(proj.reshape(Bs // nb, nb, N), k2, v2).reshape(Bs, MEM_WIDTH)


def compress_sample(cache, page_table, pe, w1, w2):
    Bs, n_pages = page_table.shape
    cpp = PAGE_SIZE // CMP_STRIDE
    nch = n_pages * cpp
    pe_rows, w1cat, w2b = _compress_weights(pe, w1, w2)
    pages = cache.reshape(cache.shape[0], PAGE_SIZE * B_KV_GROUPS, HEAD_DIM)
    page_specs = [pl.BlockSpec((None, PAGE_SIZE * B_KV_GROUPS, HEAD_DIM), lambda b, pt, p=p: (pt[b, p], 0, 0))
                  for p in range(n_pages)]

    def body(pt_ref, *refs):
        _compress_kernel(*refs, n_pages=n_pages, cpp=cpp, paged=True)

    return pl.pallas_call(
        body,
        grid_spec=pltpu.PrefetchScalarGridSpec(
            num_scalar_prefetch=1, grid=(Bs,),
            in_specs=page_specs + [pl.BlockSpec((CMP_STRIDE, 8, HEAD_DIM), lambda b, pt: (0, 0, 0)),
                                   pl.BlockSpec((CMP_STRIDE * HEAD_DIM, 2 * HEAD_DIM), lambda b, pt: (0, 0)),
                                   pl.BlockSpec((HEAD_DIM, HEAD_DIM), lambda b, pt: (0, 0))],
            out_specs=pl.BlockSpec((None, B_KV_GROUPS, nch, HEAD_DIM), lambda b, pt: (b, 0, 0, 0))),
        out_shape=jax.ShapeDtypeStruct((Bs, B_KV_GROUPS, nch, HEAD_DIM), F32),
        compiler_params=_cparams("parallel"),
        name="compress_sample",
    )(page_table, *([pages] * n_pages), pe_rows, w1cat, w2b)


def _nsa_sample_kernel(pt_ref, *refs, n_pages):
    p_ref, kv_ref, kc_ref, vc_ref, ex_ref = refs[:5]
    sk_refs = refs[5:5 + n_pages]
    sv_refs = refs[5 + n_pages:5 + 2 * n_pages]
    wk_ref, wv_ref, mix_ref = refs[5 + 2 * n_pages:]
    G, R, DH, W = B_KV_GROUPS, B_GROUP, HEAD_DIM, TOK_WIDTH
    pos = PAST_LEN
    nch = kc_ref.shape[1]
    ncol = n_pages * sk_refs[0].shape[0]
    nsl = -(-(pos + 1) // SLC_BLOCK)
    wrow = wk_ref.shape[0]
    w_buf = wrow // G

    n_l = lax.broadcasted_iota(jnp.int32, (8, nch), 1)
    cvalid = n_l * CMP_STRIDE + CMP_LEN - 1 <= pos
    n_o = lax.broadcasted_iota(jnp.int32, (nch, 128), 0) * CMP_STRIDE
    j_o = lax.broadcasted_iota(jnp.int32, (nch, 128), 1)
    ovl = jnp.where((n_o < (j_o + 1) * SLC_BLOCK) & (n_o + CMP_LEN > j_o * SLC_BLOCK), 1.0, 0.0).astype(BF16)
    j_l = lax.broadcasted_iota(jnp.int32, (1, 128), 1)
    cur = pos // SLC_BLOCK
    forced = (j_l == 0) | (j_l == cur) | (j_l == cur - 1)
    jr = lax.broadcasted_iota(jnp.int32, (128, 128), 0)
    jc = lax.broadcasted_iota(jnp.int32, (128, 128), 1)
    wcol = lax.broadcasted_iota(jnp.int32, (8, wrow), 1)
    wvalid = (pos - w_buf + wcol // G) > pos - WINDOW
    gates = jax.nn.sigmoid(p_ref[:, W + MEM_WIDTH:W + MEM_WIDTH + 128])

    q_rows, oc_rows, sb_rows, wb_rows = [], [], [], []
    for g in range(G):
        q8 = jnp.concatenate([p_ref[:, (g * R + r) * DH:(g * R + r + 1) * DH] for r in range(R)]
                             + [jnp.zeros((8 - R, DH), F32)], axis=0)
        s = lax.dot_general(q8.astype(BF16), kc_ref[g].astype(BF16), _NT, preferred_element_type=F32) * SCALE
        sm = jnp.where(cvalid, s, NEG)
        e = jnp.exp(sm - jnp.max(sm, axis=-1, keepdims=True))
        p = jnp.where(cvalid, e / jnp.sum(e, axis=-1, keepdims=True), 0.0)
        o_c = jnp.dot(p.astype(BF16), vc_ref[g].astype(BF16), preferred_element_type=F32)
        psum = jnp.broadcast_to(p[0:1] + p[1:2] + p[2:3], (8, nch))
        p_hi, p_lo = _split2(psum)
        imp = (jnp.dot(p_hi, ovl, preferred_element_type=F32) + jnp.dot(p_lo, ovl, preferred_element_type=F32))[0:1]
        score = jnp.where(forced, FORCED_SCORE, jnp.where(j_l * SLC_BLOCK <= pos, imp, -1.0))
        score = jnp.where(j_l < nsl, score, -2.0)
        sc_c = jnp.broadcast_to(score, (128, 128))
        sc_r = sc_c.T
        beats = (sc_r > sc_c) | ((sc_r == sc_c) & (jr < jc))
        rank = jnp.sum(jnp.where(beats, 1.0, 0.0), axis=0, keepdims=True)
        member = jnp.where((rank < min(N_SELECT, nsl)) & (j_l < nsl), 1.0, 0.0)
        sb_rows.append(jnp.broadcast_to(member, (8, 128)))
        wb_rows.append(jnp.where(wvalid & (wcol % G == g), 0.0, NEG))
        q_rows.append(q8)
        oc_rows.append(o_c)

    q32 = jnp.concatenate(q_rows, axis=0) * SCALE
    q32b = q32.astype(BF16)
    o_c = jnp.concatenate(oc_rows, axis=0)
    mexp = jnp.dot(jnp.concatenate(sb_rows, axis=0).astype(BF16), ex_ref[...], preferred_element_type=F32)
    row_g = lax.broadcasted_iota(jnp.int32, (8 * G, ncol), 0) // 8
    col_g = lax.broadcasted_iota(jnp.int32, (8 * G, ncol), 1) % G
    sbias = jnp.where((mexp > 0.5) & (col_g == row_g), 0.0, NEG)

    def new_rows(base):
        return jnp.concatenate([jnp.broadcast_to(kv_ref[:, base + g * DH:base + (g + 1) * DH], (8, DH)) for g in range(G)], axis=0)

    def branch(k_rows, v_rows, bias, k_base, v_base):
        s = lax.dot_general(q32b, k_rows, _NT, preferred_element_type=F32) + bias
        s_new = jnp.sum(q32 * new_rows(k_base), axis=-1, keepdims=True)
        mx = jnp.maximum(jnp.max(s, axis=-1, keepdims=True), s_new)
        p = jnp.exp(s - mx)
        p_new = jnp.exp(s_new - mx)
        num = jnp.dot(p.astype(BF16), v_rows, preferred_element_type=F32) + p_new * new_rows(v_base)
        return num / (jnp.sum(p, axis=-1, keepdims=True) + p_new)

    ks = jnp.concatenate([r_[...].astype(BF16) for r_ in sk_refs], axis=0)
    vs = jnp.concatenate([r_[...].astype(BF16) for r_ in sv_refs], axis=0)
    o_s = branch(ks, vs, sbias, 2 * G * DH, 3 * G * DH)
    o_w = branch(wk_ref[...].astype(BF16), wv_ref[...].astype(BF16), jnp.concatenate(wb_rows, axis=0), 4 * G * DH, 5 * G * DH)
    for g in range(G):
        for r in range(R):
            h, row = g * R + r, g * 8 + r
            gsel = _gate_cols(gates, g, r * 3)
            out = gsel[0] * o_c[row:row + 1] + gsel[1] * o_s[row:row + 1] + gsel[2] * o_w[row:row + 1]
            mix_ref[:, h * DH:(h + 1) * DH] = out.astype(mix_ref.dtype)


def nsa_sample(projb, kvp, kc, vc, cache_slc_k, cache_slc_v, cache_win_k, cache_win_v, page_table):
    Bs, n_pages = page_table.shape
    G = B_KV_GROUPS
    prow = PAGE_SIZE * G
    wrow = cache_win_k.shape[1] * G
    sk = cache_slc_k.reshape(cache_slc_k.shape[0], prow, HEAD_DIM)
    sv = cache_slc_v.reshape(cache_slc_v.shape[0], prow, HEAD_DIM)
    wk = cache_win_k.reshape(Bs, wrow, HEAD_DIM)
    wv = cache_win_v.reshape(Bs, wrow, HEAD_DIM)
    ncol = n_pages * prow
    expand = (jnp.arange(ncol, dtype=jnp.int32)[None, :] // (G * SLC_BLOCK)
              == jnp.arange(128, dtype=jnp.int32)[:, None]).astype(BF16)
    row = lambda n: pl.BlockSpec((None, 1, n), lambda b, pt: (b, 0, 0))
    cspec = pl.BlockSpec((None, G, kc.shape[2], HEAD_DIM), lambda b, pt: (b, 0, 0, 0))
    page_specs = [pl.BlockSpec((None, prow, HEAD_DIM), lambda b, pt, p=p: (pt[b, p], 0, 0)) for p in range(n_pages)]
    wspec = pl.BlockSpec((None, wrow, HEAD_DIM), lambda b, pt: (b, 0, 0))
    return pl.pallas_call(
        functools.partial(_nsa_sample_kernel, n_pages=n_pages),
        grid_spec=pltpu.PrefetchScalarGridSpec(
            num_scalar_prefetch=1, grid=(Bs,),
            in_specs=[row(projb.shape[1]), row(kvp.shape[1]), cspec, cspec,
                      pl.BlockSpec((128, ncol), lambda b, pt: (0, 0))] + page_specs + page_specs + [wspec, wspec],
            out_specs=row(TOK_WIDTH)),
        out_shape=jax.ShapeDtypeStruct((Bs, 1, TOK_WIDTH), BF16),
        compiler_params=_cparams("parallel"),
        name="nsa_sample",
    )(page_table, projb.reshape(Bs, 1, -1), kvp.reshape(Bs, 1, -1), kc, vc, expand, *([sk] * n_pages),
      *([sv] * n_pages), wk, wv).reshape(Bs, TOK_WIDTH)


def _j_masked_probs(s, mask):
    p = jax.nn.softmax(jnp.where(mask, s, NEG), axis=-1)
    return jnp.where(mask, p, 0.0)


def _j_mem_attend(q, k, v):
    s = jnp.einsum('blhd,bmhd->bhlm', q, k).astype(F32) * SCALE
    p = jax.nn.softmax(s, axis=-1).astype(v.dtype)
    return jnp.einsum('bhlm,bmhd->blhd', p, v)


def _j_head_rmsnorm(h, g):
    h = h * lax.rsqrt(jnp.mean(h * h, axis=-1, keepdims=True) + EPS)
    return h * g.reshape(h.shape[-2:]).astype(F32)


def _j_mlstm(q, k, v, ig, lf, C0, n0, m0):
    B, L, H, D = q.shape
    c = min(MLSTM_CHUNK, L)
    nc = -(-L // c)
    pad = nc * c - L

    def blocks(a, fill):
        a = jnp.pad(a.astype(F32), [(0, 0), (0, pad)] + [(0, 0)] * (a.ndim - 2), constant_values=fill)
        a = a.reshape((B, nc, c) + a.shape[2:])
        return jnp.swapaxes(jnp.moveaxis(a, 1, 0), 2, 3)

    xs = (blocks(q, 0.0), blocks(k, 0.0), blocks(v, 0.0), blocks(ig, NEG), blocks(lf, 0.0))
    tril = jnp.tril(jnp.ones((c, c), dtype=bool))

    def step(carry, xs_c):
        C, n, m = carry
        qc, kc, vc, ic, fc = xs_c
        b = jnp.cumsum(fc, axis=-1)
        dlog = jnp.where(tril, b[..., :, None] - b[..., None, :] + ic[..., None, :], -jnp.inf)
        inter = b + m[..., None]
        mt = jnp.maximum(inter, jnp.max(dlog, axis=-1))
        a = jnp.exp(inter - mt)
        w = jnp.exp(dlog - mt[..., None]) * jnp.einsum('bhtd,bhsd->bhts', qc, kc)
        num = a[..., None] * jnp.einsum('bhtd,bhde->bhte', qc, C) + jnp.einsum('bhts,bhse->bhte', w, vc)
        den = a * jnp.einsum('bhtd,bhd->bht', qc, n) + jnp.sum(w, axis=-1)
        h = num / jnp.maximum(jnp.abs(den), jnp.exp(-mt))[..., None]
        bl = b[..., -1]
        g = bl[..., None] - b + ic
        m_new = jnp.maximum(bl + m, jnp.max(g, axis=-1))
        decay = jnp.exp(bl + m - m_new)
        wk = jnp.exp(g - m_new[..., None])
        C_new = decay[..., None, None] * C + jnp.einsum('bhs,bhsd,bhse->bhde', wk, kc, vc)
        n_new = decay[..., None] * n + jnp.einsum('bhs,bhsd->bhd', wk, kc)
        return (C_new, n_new, m_new), h

    (Cf, nf, mf), hs = lax.scan(step, (C0.astype(F32), n0.astype(F32), m0.astype(F32)), xs)
    hs = jnp.moveaxis(jnp.swapaxes(hs, 2, 3), 0, 1).reshape(B, nc * c, H, D)[:, :L]
    return hs, (Cf, nf, mf)


def _j_compress(x, pe, w1, w2):
    B, T, G, D = x.shape
    nch = T // CMP_STRIDE
    xs = x[:, :nch * CMP_STRIDE].reshape(B, nch, CMP_STRIDE, G, D)
    pe2 = pe.reshape(2, CMP_STRIDE, D)
    w12 = w1.reshape(2, CMP_STRIDE, D, -1)
    first = jnp.einsum('bncgd,cde->bnge', xs + pe2[0][:, None, :], w12[0])
    second = jnp.einsum('bncgd,cde->bnge', xs + pe2[1][:, None, :], w12[1])
    hid = jax.nn.gelu(first[:, :-1] + second[:, 1:])
    return hid @ w2


def _j_cmp_select(q, pos, kc, vc, n_slc):
    s = jnp.einsum('blgrd,bngd->bgrln', q, kc).astype(F32) * SCALE
    n_cmp = kc.shape[1]
    start = jnp.arange(n_cmp, dtype=jnp.int32) * CMP_STRIDE
    mask = (start + CMP_LEN - 1)[None, :] <= pos[:, None]
    p = _j_masked_probs(s, mask)
    o_c = jnp.einsum('bgrln,bngd->blgrd', p.astype(vc.dtype), vc)
    j = jnp.arange(n_slc, dtype=jnp.int32)
    overlap = ((start[:, None] < (j[None, :] + 1) * SLC_BLOCK)
               & (start[:, None] + CMP_LEN > j[None, :] * SLC_BLOCK)).astype(F32)
    imp = jnp.einsum('bgrln,nj->bglj', p, overlap)
    cur = (pos // SLC_BLOCK)[:, None]
    forced = (j[None, :] == 0) | (j[None, :] == cur) | (j[None, :] == cur - 1)
    valid = j[None, :] * SLC_BLOCK <= pos[:, None]
    score = jnp.where(forced, FORCED_SCORE, jnp.where(valid, imp, -1.0))
    _, idx = lax.top_k(score, min(N_SELECT, n_slc))
    return o_c, jnp.transpose(idx, (0, 2, 1, 3))


def _j_sel_attend(q, pos, idx, gather):
    kb, vb = gather(idx)
    B, L, G, K, C, D = kb.shape
    R = q.shape[3]
    s = jnp.einsum('blgrd,blgkcd->blgrkc', q, kb).astype(F32) * SCALE
    kpos = idx[..., None] * SLC_BLOCK + jnp.arange(SLC_BLOCK, dtype=jnp.int32)
    mask = (kpos <= pos[None, :, None, None, None]).reshape(B, L, G, 1, K * C)
    p = _j_masked_probs(s.reshape(B, L, G, R, K * C), mask).astype(vb.dtype)
    return jnp.einsum('blgrn,blgnd->blgrd', p, vb.reshape(B, L, G, K * C, D))


def _j_sel_attend_blocked(q, pos, idx, gather):
    B, L = q.shape[:2]
    qb = SEL_QBLK if L % SEL_QBLK == 0 else L
    nb = L // qb
    if nb == 1:
        return _j_sel_attend(q, pos, idx, gather)
    qs = jnp.swapaxes(q.reshape((B, nb, qb) + q.shape[2:]), 0, 1)
    ps = pos.reshape(nb, qb)
    ids = jnp.swapaxes(idx.reshape((B, nb, qb) + idx.shape[2:]), 0, 1)
    out = lax.map(lambda a: _j_sel_attend(a[0], a[1], a[2], gather), (qs, ps, ids))
    return jnp.swapaxes(out, 0, 1).reshape((B, L) + out.shape[3:])


def _j_window_prompt(q, kw, vw):
    B, S, G, R, D = q.shape
    nb = S // WIN_QBLK
    span = WINDOW + WIN_QBLK
    padw = [(0, 0), (WINDOW, 0), (0, 0), (0, 0)]
    kp, vp = jnp.pad(kw, padw), jnp.pad(vw, padw)
    idx = jnp.arange(nb, dtype=jnp.int32)[:, None] * WIN_QBLK + jnp.arange(span, dtype=jnp.int32)[None, :]
    kb, vb = kp[:, idx], vp[:, idx]
    qb = q.reshape(B, nb, WIN_QBLK, G, R, D)
    s = jnp.einsum('bnqgrd,bnkgd->bngrqk', qb, kb).astype(F32) * SCALE
    kpos = idx - WINDOW
    qpos = jnp.arange(nb, dtype=jnp.int32)[:, None] * WIN_QBLK + jnp.arange(WIN_QBLK, dtype=jnp.int32)[None, :]
    mask = ((kpos[:, None, :] <= qpos[:, :, None]) & (kpos[:, None, :] > qpos[:, :, None] - WINDOW)
            & (kpos[:, None, :] >= 0))
    p = _j_masked_probs(s, mask[None, :, None, None]).astype(vb.dtype)
    return jnp.einsum('bngrqk,bnkgd->bnqgrd', p, vb).reshape(B, S, G, R, D)


def _j_window_sample(q, pos, kw_new, vw_new, kw_buf, vw_buf):
    w_buf = kw_buf.shape[1]
    kk = jnp.concatenate([kw_buf, kw_new], axis=1)
    vv = jnp.concatenate([vw_buf, vw_new], axis=1)
    kpos = PAST_LEN - w_buf + jnp.arange(kk.shape[1], dtype=jnp.int32)
    s = jnp.einsum('blgrd,bkgd->bgrlk', q, kk).astype(F32) * SCALE
    mask = (kpos[None, :] <= pos[:, None]) & (kpos[None, :] > pos[:, None] - WINDOW)
    p = _j_masked_probs(s, mask).astype(vv.dtype)
    return jnp.einsum('bgrlk,bkgd->blgrd', p, vv)


def _j_combine(gates, o_c, o_s, o_w):
    out = gates[..., 0:1] * o_c + gates[..., 1:2] * o_s + gates[..., 2:3] * o_w
    return out.reshape(out.shape[:2] + (-1,))


def _j_nsa_prompt(q, gates, kv, pe_k, w1_k, w2_k, pe_v, w1_v, w2_v):
    kc_raw, vc_raw, ks, vs, kw, vw = kv
    B, L, G = ks.shape[:3]
    pos = jnp.arange(L, dtype=jnp.int32)
    kc = _j_compress(kc_raw, pe_k, w1_k, w2_k)
    vc = _j_compress(vc_raw, pe_v, w1_v, w2_v)
    n_slc = -(-L // SLC_BLOCK)
    o_c, idx = _j_cmp_select(q, pos, kc, vc, n_slc)
    ksb = ks.reshape(B, n_slc, SLC_BLOCK, G, HEAD_DIM)
    vsb = vs.reshape(B, n_slc, SLC_BLOCK, G, HEAD_DIM)
    b_i = jnp.arange(B)[:, None, None, None]
    g_i = jnp.arange(G)[None, None, :, None]
    gather = lambda ix: (ksb[b_i, ix, :, g_i, :], vsb[b_i, ix, :, g_i, :])
    o_s = _j_sel_attend_blocked(q, pos, idx, gather)
    o_w = _j_window_prompt(q, kw, vw)
    return _j_combine(gates, o_c, o_s, o_w)


def _j_nsa_sample(q, gates, kv, cache_cmp_k, cache_cmp_v, cache_slc_k, cache_slc_v, cache_win_k, cache_win_v,
                  page_table, pe_k, w1_k, w2_k, pe_v, w1_v, w2_v):
    kc_new, vc_new, ks_new, vs_new, kw_new, vw_new = kv
    B, L, G = ks_new.shape[:3]
    n_pages = PAST_LEN // PAGE_SIZE
    pos = PAST_LEN + jnp.arange(L, dtype=jnp.int32)

    def paged_rows(pool):
        return pool[page_table].reshape(B, n_pages * PAGE_SIZE, G, HEAD_DIM)

    kc = _j_compress(jnp.concatenate([paged_rows(cache_cmp_k), kc_new], axis=1), pe_k, w1_k, w2_k)
    vc = _j_compress(jnp.concatenate([paged_rows(cache_cmp_v), vc_new], axis=1), pe_v, w1_v, w2_v)
    n_slc = -(-(PAST_LEN + L) // SLC_BLOCK)
    o_c, idx = _j_cmp_select(q, pos, kc, vc, n_slc)
    n_pb = PAST_LEN // SLC_BLOCK
    n_tail = n_slc - n_pb
    bpp = PAGE_SIZE // SLC_BLOCK

    def tail_blocks(x):
        x = jnp.pad(x, ((0, 0), (0, n_tail * SLC_BLOCK - L), (0, 0), (0, 0)))
        return x.reshape(B, n_tail, SLC_BLOCK, G, HEAD_DIM)

    ks_tail, vs_tail = tail_blocks(ks_new), tail_blocks(vs_new)
    ks_pool = cache_slc_k.reshape(-1, bpp, SLC_BLOCK, G, HEAD_DIM)
    vs_pool = cache_slc_v.reshape(-1, bpp, SLC_BLOCK, G, HEAD_DIM)
    b_i = jnp.arange(B)[:, None, None, None]
    g_i = jnp.arange(G)[None, None, :, None]

    def gather(ix):
        in_past = (ix < n_pb)[..., None, None]
        phys = page_table[b_i, jnp.clip(ix // bpp, 0, n_pages - 1)]
        sub = ix % bpp
        jt = jnp.clip(ix - n_pb, 0, n_tail - 1)
        pick = lambda pool, tail: jnp.where(in_past, pool[phys, sub, :, g_i, :], tail[b_i, jt, :, g_i, :])
        return pick(ks_pool, ks_tail), pick(vs_pool, vs_tail)

    o_s = _j_sel_attend_blocked(q, pos, idx, gather)
    o_w = _j_window_sample(q, pos, kw_new, vw_new, cache_win_k, cache_win_v)
    return _j_combine(gates, o_c, o_s, o_w)


def kernel(x_prompt, x_sample, mem_prompt, cache_mem_k, cache_mem_v, state_mlstm_C, state_mlstm_n, state_mlstm_m, cache_cmp_k, cache_cmp_v, cache_slc_k, cache_slc_v, cache_win_k, cache_win_v, page_table, ffn1_norm, ffn1_w_in, ffn1_w_out, ffn2_norm, ffn2_w_in, ffn2_w_out, mix_norm, a_w_in, a_b_i, a_b_f, a_head_norm, b_w_in, mem_norm, w_mem_kv, w_out, kv_norm, w_kv, cmp_pe_k, cmp_w1_k, cmp_w2_k, cmp_pe_v, cmp_w1_v, cmp_w2_v, final_norm):
    G, R, DH = B_KV_GROUPS, B_GROUP, HEAD_DIM
    ffn1_wi, ffn1_wo = ffn1_w_in.astype(BF16), ffn1_w_out.astype(BF16)
    ffn2_wi, ffn2_wo = ffn2_w_in.astype(BF16), ffn2_w_out.astype(BF16)
    w_o = w_out.astype(BF16)
    w_kv_b = w_kv.astype(BF16)
    w_mkv = w_mem_kv.astype(BF16)

    wa = a_w_in[0]
    zc = lambda n: jnp.zeros((D_MODEL, n), F32)
    a_wr = jnp.concatenate([wa[:, :4 * TOK_WIDTH], wa[:, 4 * TOK_WIDTH + 2 * A_HEADS:],
                            wa[:, 4 * TOK_WIDTH:4 * TOK_WIDTH + A_HEADS], zc(16 - A_HEADS),
                            wa[:, 4 * TOK_WIDTH + A_HEADS:4 * TOK_WIDTH + 2 * A_HEADS], zc(128 - 16 - A_HEADS),
                            zc(128)], axis=1).astype(BF16)
    gate_bias = jnp.concatenate([a_b_i[0], jnp.zeros((16 - A_HEADS,), F32), a_b_f[0],
                                 jnp.zeros((128 - 16 - A_HEADS,), F32)]).reshape(1, 128)
    wb = b_w_in[0]
    b_wr = jnp.concatenate([wb[:, :TOK_WIDTH], wb[:, TOK_WIDTH + 3 * B_HEADS:],
                            wb[:, TOK_WIDTH:TOK_WIDTH + 3 * B_HEADS], zc(256 - 3 * B_HEADS)], axis=1).astype(BF16)

    Bp, S = x_prompt.shape[:2]
    Mp = Bp * S
    memx = mem_prompt.reshape(Bp * MEM_TOKENS, D_MODEL)
    memkv = [norm_matmul(memx, mem_norm[l], w_mkv[l], tm=512, tn=512) for l in range(DEPTH)]
    mem5 = jnp.stack(memkv).reshape(DEPTH, Bp, MEM_TOKENS, 2, MEM_HEADS, DH)
    mem_k_p, mem_v_p = mem5[:, :, :, 0], mem5[:, :, :, 1]

    x = x_prompt.reshape(Mp, D_MODEL)
    x = ffn(x, ffn1_norm[0], ffn1_wi, ffn1_wo, 0, tm=512, tf=512)
    proja = norm_matmul(x, mix_norm[0], a_wr, tm=1024, tn=768)
    mix, C_fin, n_fin, m_fin = mlstm_prompt(proja, gate_bias, a_head_norm[0], Bp)
    mo = mem_attn_prompt(proja, 4 * TOK_WIDTH // MEM_WIDTH, memkv[0], Bp)
    x = outproj(x, mix, mo, w_o, 0, tm=512)
    x = ffn(x, ffn2_norm[0], ffn2_wi, ffn2_wo, 0, tm=512, tf=512)
    kvp, kv4 = kv_proj(x, kv_norm, w_kv_b, tm=1024)
    x = ffn(x, ffn1_norm[1], ffn1_wi, ffn1_wo, 1, tm=512, tf=512)
    projb = norm_matmul(x, mix_norm[1], b_wr, tm=1024, tn=768)
    kc = compress_prompt(kvp, 0, Bp, cmp_pe_k, cmp_w1_k, cmp_w2_k)
    vc = compress_prompt(kvp, 1, Bp, cmp_pe_v, cmp_w1_v, cmp_w2_v)
    oc, member = cmp_select_prompt(projb, kc, vc, Bp)
    mix = sel_win_combine(projb, kvp, oc, member, Bp)
    mo = mem_attn_prompt(projb, TOK_WIDTH // MEM_WIDTH, memkv[1], Bp)
    x = outproj(x, mix, mo, w_o, 1, tm=512)
    y_prompt = ffn(x, ffn2_norm[1], ffn2_wi, ffn2_wo, 1, final_norm, tm=512, tf=512).reshape(Bp, S, D_MODEL)
    C_p = C_fin[None]
    n_p = n_fin[None, :, :A_HEADS]
    m_p = m_fin[None, :, :A_HEADS, 0]
    kv_p = tuple(kv4[i].reshape(Bp, S, G, DH) for i in range(6))

    Bs = x_sample.shape[0]
    xs = x_sample.reshape(Bs, D_MODEL)
    xs = ffn(xs, ffn1_norm[0], ffn1_wi, ffn1_wo, 0, tm=Bs, tf=512)
    proja_s = norm_matmul(xs, mix_norm[0], a_wr, tm=Bs, tn=768)
    mix_s, C_new, n_new, m_new = mlstm_sample(proja_s, gate_bias, state_mlstm_m[0], a_head_norm[0],
                                              state_mlstm_C[0], state_mlstm_n[0])
    mo_s = mem_attn_sample(proja_s, 4 * TOK_WIDTH // MEM_WIDTH, cache_mem_k, cache_mem_v, 0)
    xs = outproj(xs, mix_s.reshape(Bs, TOK_WIDTH), mo_s, w_o, 0, tm=Bs)
    xs = ffn(xs, ffn2_norm[0], ffn2_wi, ffn2_wo, 0, tm=Bs, tf=512)
    kvp_s = norm_matmul(xs, kv_norm, w_kv_b, tm=Bs, tn=512)
    xs = ffn(xs, ffn1_norm[1], ffn1_wi, ffn1_wo, 1, tm=Bs, tf=512)
    projb_s = norm_matmul(xs, mix_norm[1], b_wr, tm=Bs, tn=768)
    kc_s = compress_sample(cache_cmp_k, page_table, cmp_pe_k, cmp_w1_k, cmp_w2_k)
    vc_s = compress_sample(cache_cmp_v, page_table, cmp_pe_v, cmp_w1_v, cmp_w2_v)
    mix_s = nsa_sample(projb_s, kvp_s, kc_s, vc_s, cache_slc_k, cache_slc_v, cache_win_k, cache_win_v, page_table)
    mo_s = mem_attn_sample(projb_s, TOK_WIDTH // MEM_WIDTH, cache_mem_k, cache_mem_v, 1)
    xs = outproj(xs, mix_s, mo_s, w_o, 1, tm=Bs)
    y_sample = ffn(xs, ffn2_norm[1], ffn2_wi, ffn2_wo, 1, final_norm, tm=Bs, tf=512).reshape(Bs, 1, D_MODEL)
    C_s = C_new[None]
    n_s = n_new[None]
    m_s = m_new[None, :, :A_HEADS, 0]
    kvs5 = kvp_s.reshape(Bs, 1, 6, G, DH)
    kv_s = tuple(kvs5[:, :, i] for i in range(6))

    w_p = min(WINDOW, S)
    cmp_k_p, cmp_v_p, slc_k_p, slc_v_p, win_k_all, win_v_all = kv_p
    win_k_p = win_k_all[:, S - w_p:]
    win_v_p = win_v_all[:, S - w_p:]
    cmp_k_s, cmp_v_s, slc_k_s, slc_v_s, win_k_s, win_v_s = kv_s
    return (y_prompt, y_sample, mem_k_p, mem_v_p, C_p, n_p, m_p, C_s, n_s, m_s,
            cmp_k_p, cmp_v_p, slc_k_p, slc_v_p, win_k_p, win_v_p,
            cmp_k_s, cmp_v_s, slc_k_s, slc_v_s, win_k_s, win_v_s)
```

```python
import functools

import jax
import jax.numpy as jnp
from jax import lax
from jax.experimental import pallas as pl
from jax.experimental.pallas import tpu as pltpu

D_MODEL = 2048
DEPTH = 2
PAST_LEN = 2048
PAGE_SIZE = 128
HEAD_DIM = 128
N_A_LAYERS = DEPTH // 2
MEM_TOKENS = 256
MEM_HEADS = 4
MEM_WIDTH = MEM_HEADS * HEAD_DIM
TOK_WIDTH = D_MODEL - MEM_WIDTH
A_HEADS = TOK_WIDTH // HEAD_DIM
B_HEADS = TOK_WIDTH // HEAD_DIM
B_KV_GROUPS = 4
B_GROUP = B_HEADS // B_KV_GROUPS
D_FF = ((8 * D_MODEL // 3 + 255) // 256) * 256
MLSTM_CHUNK = 128
CMP_STRIDE = 16
CMP_LEN = 2 * CMP_STRIDE
SLC_BLOCK = 64
N_SELECT = 16
WINDOW = 512
WIN_QBLK = 128
SEL_QBLK = 64
FORCED_SCORE = 1e4
NEG = -1e30
EPS = 1e-6
SCALE = HEAD_DIM ** -0.5

F32 = jnp.float32
BF16 = jnp.bfloat16
VMEM_LIMIT = 56 * 1024 * 1024


def _cparams(*sem):
    return pltpu.CompilerParams(dimension_semantics=sem, vmem_limit_bytes=VMEM_LIMIT)


def _rms_bf16(x, g):
    ms = jnp.mean(x * x, axis=-1, keepdims=True)
    return (x * lax.rsqrt(ms + EPS) * g).astype(BF16)


def _norm_matmul_kernel(x_ref, g_ref, w_ref, o_ref, h_ref):
    @pl.when(pl.program_id(1) == 0)
    def _():
        h_ref[...] = _rms_bf16(x_ref[...], g_ref[...])

    o_ref[...] = jnp.dot(h_ref[...], w_ref[...], preferred_element_type=F32).astype(o_ref.dtype)


def norm_matmul(x, g, w, *, tm, tn, out_dtype=F32):
    M, D = x.shape
    N = w.shape[1]
    return pl.pallas_call(
        _norm_matmul_kernel,
        grid=(M // tm, N // tn),
        in_specs=[pl.BlockSpec((tm, D), lambda i, j: (i, 0)),
                  pl.BlockSpec((1, D), lambda i, j: (0, 0)),
                  pl.BlockSpec((D, tn), lambda i, j: (0, j))],
        out_specs=pl.BlockSpec((tm, tn), lambda i, j: (i, j)),
        out_shape=jax.ShapeDtypeStruct((M, N), out_dtype),
        scratch_shapes=[pltpu.VMEM((tm, D), BF16)],
        compiler_params=_cparams("parallel", "arbitrary"),
        name="norm_matmul",
    )(x, g.reshape(1, D), w)


def _kv_proj_kernel(x_ref, g_ref, w_ref, o_ref, o4_ref, h_ref):
    @pl.when(pl.program_id(1) == 0)
    def _():
        h_ref[...] = _rms_bf16(x_ref[...], g_ref[...])

    y = jnp.dot(h_ref[...], w_ref[...], preferred_element_type=F32)
    o_ref[...] = y
    tm = y.shape[0]
    for g in range(B_KV_GROUPS):
        o4_ref[pl.ds(g, tm, stride=B_KV_GROUPS), :] = y[:, g * HEAD_DIM:(g + 1) * HEAD_DIM]


def kv_proj(x, g, w, *, tm):
    M, D = x.shape
    GD = B_KV_GROUPS * HEAD_DIM
    nbr = w.shape[1] // GD
    return pl.pallas_call(
        _kv_proj_kernel,
        grid=(M // tm, nbr),
        in_specs=[pl.BlockSpec((tm, D), lambda i, j: (i, 0)),
                  pl.BlockSpec((1, D), lambda i, j: (0, 0)),
                  pl.BlockSpec((D, GD), lambda i, j: (0, j))],
        out_specs=[pl.BlockSpec((tm, GD), lambda i, j: (i, j)),
                   pl.BlockSpec((None, tm * B_KV_GROUPS, HEAD_DIM), lambda i, j: (j, i, 0))],
        out_shape=[jax.ShapeDtypeStruct((M, nbr * GD), F32),
                   jax.ShapeDtypeStruct((nbr, M * B_KV_GROUPS, HEAD_DIM), F32)],
        scratch_shapes=[pltpu.VMEM((tm, D), BF16)],
        compiler_params=_cparams("parallel", "arbitrary"),
        name="kv_proj",
    )(x, g.reshape(1, D), w)


def _ffn_kernel(x_ref, g_ref, wg_ref, wu_ref, wo_ref, fg_ref, o_ref, h_ref, acc_ref, *, final_norm):
    f = pl.program_id(1)

    @pl.when(f == 0)
    def _():
        h_ref[...] = _rms_bf16(x_ref[...], g_ref[...])
        acc_ref[...] = jnp.zeros_like(acc_ref)

    h = h_ref[...]
    gate = jnp.dot(h, wg_ref[...], preferred_element_type=F32)
    up = jnp.dot(h, wu_ref[...], preferred_element_type=F32)
    act = (gate * jax.nn.sigmoid(gate) * up).astype(BF16)
    acc_ref[...] += jnp.dot(act, wo_ref[...], preferred_element_type=F32)

    @pl.when(f == pl.num_programs(1) - 1)
    def _():
        y = x_ref[...] + 0.5 * acc_ref[...]
        if final_norm:
            ms = jnp.mean(y * y, axis=-1, keepdims=True)
            y = y * lax.rsqrt(ms + EPS) * fg_ref[...]
        o_ref[...] = y


def ffn(x, g, w_in, w_out, layer, final_gain=None, *, tm, tf):
    M, D = x.shape
    F = w_out.shape[1]
    nf = F // tf
    fg = jnp.ones((D,), F32) if final_gain is None else final_gain
    return pl.pallas_call(
        functools.partial(_ffn_kernel, final_norm=final_gain is not None),
        grid=(M // tm, nf),
        in_specs=[pl.BlockSpec((tm, D), lambda i, f: (i, 0)),
                  pl.BlockSpec((1, D), lambda i, f: (0, 0)),
                  pl.BlockSpec((None, D, tf), lambda i, f: (layer, 0, f)),
                  pl.BlockSpec((None, D, tf), lambda i, f: (layer, 0, f + nf)),
                  pl.BlockSpec((None, tf, D), lambda i, f: (layer, f, 0)),
                  pl.BlockSpec((1, D), lambda i, f: (0, 0))],
        out_specs=pl.BlockSpec((tm, D), lambda i, f: (i, 0)),
        out_shape=jax.ShapeDtypeStruct((M, D), F32),
        scratch_shapes=[pltpu.VMEM((tm, D), BF16), pltpu.VMEM((tm, D), F32)],
        compiler_params=_cparams("parallel", "arbitrary"),
        name="ffn",
    )(x, g.reshape(1, D), w_in, w_in, w_out, fg.reshape(1, D))


def _outproj_kernel(x_ref, mix_ref, mo_ref, wa_ref, wb_ref, o_ref):
    y = jnp.dot(mix_ref[...].astype(BF16), wa_ref[...], preferred_element_type=F32)
    y += jnp.dot(mo_ref[...].astype(BF16), wb_ref[...], preferred_element_type=F32)
    o_ref[...] = x_ref[...] + y


def outproj(x, mix, mo, w, layer, *, tm):
    M, D = x.shape
    Ka, Kb = mix.shape[1], mo.shape[1]
    assert Ka % Kb == 0
    return pl.pallas_call(
        _outproj_kernel,
        grid=(M // tm,),
        in_specs=[pl.BlockSpec((tm, D), lambda i: (i, 0)),
                  pl.BlockSpec((tm, Ka), lambda i: (i, 0)),
                  pl.BlockSpec((tm, Kb), lambda i: (i, 0)),
                  pl.BlockSpec((None, Ka, D), lambda i: (layer, 0, 0)),
                  pl.BlockSpec((None, Kb, D), lambda i: (layer, Ka // Kb, 0))],
        out_specs=pl.BlockSpec((tm, D), lambda i: (i, 0)),
        out_shape=jax.ShapeDtypeStruct((M, D), F32),
        compiler_params=_cparams("parallel"),
        name="outproj",
    )(x, mix, mo, w, w)


def _compress_kernel(x_ref, pe_ref, w1_ref, w2_ref, o_ref, *, nch):
    G, DH = B_KV_GROUPS, HEAD_DIM
    lhs = jnp.concatenate(
        [jnp.concatenate([x_ref[:, c, g * DH:(g + 1) * DH] for g in range(G)] + [pe_ref[c]], axis=0).astype(BF16)
         for c in range(CMP_STRIDE)], axis=1)
    acc = jnp.dot(lhs, w1_ref[...], preferred_element_type=F32)
    bias_first = acc[G * nch:G * nch + 1, :DH]
    bias_second = acc[G * nch + 1:G * nch + 2, DH:]
    for g in range(G):
        first = acc[g * nch:(g + 1) * nch, :DH] + bias_first
        second = acc[g * nch:(g + 1) * nch, DH:] + bias_second
        hid = jax.nn.gelu(first + pltpu.roll(second, nch - 1, 0))
        o_ref[g] = jnp.dot(hid.astype(BF16), w2_ref[...], preferred_element_type=F32)


def _compress_weights(pe, w1, w2):
    DH = HEAD_DIM
    pe2 = pe.reshape(2, CMP_STRIDE, DH)
    pe_rows = jnp.concatenate([pe2[0][:, None], pe2[1][:, None], jnp.zeros((CMP_STRIDE, 6, DH), F32)], axis=1)
    w12 = w1.reshape(2, CMP_STRIDE, DH, -1)
    w1cat = jnp.concatenate([w12[0], w12[1]], axis=-1).astype(BF16)
    return pe_rows, w1cat.reshape(CMP_STRIDE * DH, -1), w2.astype(BF16)


def compress_prompt(kvp, col_block, n_batch, pe, w1, w2):
    M = kvp.shape[0]
    nch = M // n_batch // CMP_STRIDE
    pe_rows, w1cat, w2b = _compress_weights(pe, w1, w2)
    x = kvp.reshape(M // CMP_STRIDE, CMP_STRIDE, kvp.shape[1])
    return pl.pallas_call(
        functools.partial(_compress_kernel, nch=nch),
        grid=(n_batch,),
        in_specs=[pl.BlockSpec((nch, CMP_STRIDE, 512), lambda b: (b, 0, col_block)),
                  pl.BlockSpec((CMP_STRIDE, 8, HEAD_DIM), lambda b: (0, 0, 0)),
                  pl.BlockSpec((CMP_STRIDE * HEAD_DIM, 2 * HEAD_DIM), lambda b: (0, 0)),
                  pl.BlockSpec((HEAD_DIM, HEAD_DIM), lambda b: (0, 0))],
        out_specs=pl.BlockSpec((None, B_KV_GROUPS, nch, HEAD_DIM), lambda b: (b, 0, 0, 0)),
        out_shape=jax.ShapeDtypeStruct((n_batch, B_KV_GROUPS, nch, HEAD_DIM), F32),
        compiler_params=_cparams("parallel"),
        name="compress_prompt",
    )(x, pe_rows, w1cat, w2b)


_NT = (((1,), (1,)), ((), ()))


def _split2(x):
    hi = x.astype(BF16)
    return hi, (x - hi.astype(F32)).astype(BF16)


def _cmp_select_kernel(q_ref, kc_ref, vc_ref, oc_ref, mem_ref, sc_ref, *, T, nch, nsl, pos0):
    G, R, DH = B_KV_GROUPS, B_GROUP, HEAD_DIM
    base = pos0 + pl.program_id(1) * T
    pos_r = base + lax.broadcasted_iota(jnp.int32, (T, nch), 0)
    n_c = lax.broadcasted_iota(jnp.int32, (T, nch), 1)
    cmask = (n_c * CMP_STRIDE + CMP_LEN - 1) <= pos_r
    j_o = lax.broadcasted_iota(jnp.int32, (nsl, nch), 0)
    st_o = lax.broadcasted_iota(jnp.int32, (nsl, nch), 1) * CMP_STRIDE
    ovl_t = jnp.where((st_o < (j_o + 1) * SLC_BLOCK) & (st_o + CMP_LEN > j_o * SLC_BLOCK), 1.0, 0.0).astype(BF16)
    j_s = lax.broadcasted_iota(jnp.int32, (nsl, T), 0)
    pos_s = base + lax.broadcasted_iota(jnp.int32, (nsl, T), 1)
    cur = pos_s // SLC_BLOCK
    forced = (j_s == 0) | (j_s == cur) | (j_s == cur - 1)
    valid = j_s * SLC_BLOCK <= pos_s
    for g in range(G):
        kcg = kc_ref[g].astype(BF16)
        vcg = vc_ref[g].astype(BF16)
        psum = jnp.zeros((T, nch), F32)
        for r in range(R):
            h = g * R + r
            qh = q_ref[:, h * DH:(h + 1) * DH].astype(BF16)
            s = lax.dot_general(qh, kcg, _NT, preferred_element_type=F32) * SCALE
            sm = jnp.where(cmask, s, NEG)
            e = jnp.exp(sm - jnp.max(sm, axis=-1, keepdims=True))
            p = jnp.where(cmask, e / jnp.sum(e, axis=-1, keepdims=True), 0.0)
            oc_ref[:, h * DH:(h + 1) * DH] = jnp.dot(p.astype(BF16), vcg, preferred_element_type=F32).astype(oc_ref.dtype)
            psum = psum + p
        p_hi, p_lo = _split2(psum)
        imp_t = (lax.dot_general(ovl_t, p_hi, _NT, preferred_element_type=F32)
                 + lax.dot_general(ovl_t, p_lo, _NT, preferred_element_type=F32))
        score = jnp.where(forced, FORCED_SCORE, jnp.where(valid, imp_t, -1.0))
        sc_ref[...] = score
        rank = jnp.zeros((nsl, T), F32)
        for jp in range(nsl):
            row = sc_ref[jp:jp + 1, :]
            beats = (row > score) | ((row == score) & (j_s > jp))
            rank = rank + jnp.where(beats, 1.0, 0.0)
        mem_t = jnp.where(rank < min(N_SELECT, nsl), 1.0, 0.0)
        if nsl < 128:
            mem_t = jnp.concatenate([mem_t, jnp.zeros((128 - nsl, T), F32)], axis=0)
        mem_ref[g] = mem_t.T.astype(mem_ref.dtype)


def cmp_select_prompt(projb, kc, vc, n_batch, *, T=256):
    M = projb.shape[0]
    L = M // n_batch
    nt, nch, nsl = L // T, L // CMP_STRIDE, L // SLC_BLOCK
    return pl.pallas_call(
        functools.partial(_cmp_select_kernel, T=T, nch=nch, nsl=nsl, pos0=0),
        grid=(n_batch, nt),
        in_specs=[pl.BlockSpec((T, TOK_WIDTH), lambda b, i: (b * nt + i, 0)),
                  pl.BlockSpec((None, B_KV_GROUPS, nch, HEAD_DIM), lambda b, i: (b, 0, 0, 0)),
                  pl.BlockSpec((None, B_KV_GROUPS, nch, HEAD_DIM), lambda b, i: (b, 0, 0, 0))],
        out_specs=[pl.BlockSpec((T, TOK_WIDTH), lambda b, i: (b * nt + i, 0)),
                   pl.BlockSpec((None, B_KV_GROUPS, T, 128), lambda b, i: (b, 0, i, 0))],
        out_shape=[jax.ShapeDtypeStruct((M, TOK_WIDTH), BF16),
                   jax.ShapeDtypeStruct((n_batch, B_KV_GROUPS, L, 128), BF16)],
        scratch_shapes=[pltpu.VMEM((nsl, T), F32)],
        compiler_params=_cparams("parallel", "parallel"),
        name="cmp_select_prompt",
    )(projb, kc, vc)


def _sel_win_kernel(q_ref, ks_ref, vs_ref, kw0_ref, kw1_ref, kw2_ref, vw0_ref, vw1_ref, vw2_ref, oc_ref, mem_ref,
                    gt_ref, ex_ref, mix_ref, *, T, CK):
    R, DH = B_GROUP, HEAD_DIM
    g = pl.program_id(1)
    i = pl.program_id(2)
    q3 = jnp.concatenate([q_ref[:, r * DH:(r + 1) * DH] for r in range(R)], axis=0)
    q3 = (q3 * SCALE).astype(BF16)
    qpos = i * T + lax.broadcasted_iota(jnp.int32, (T, CK), 0)
    memb = mem_ref[...]
    n_chunks = (i * T + T + CK - 1) // CK

    def chunk(c, carry):
        m, l, acc = carry
        k0 = pl.multiple_of(c * CK, CK)
        kc = ks_ref[pl.ds(k0, CK), :].astype(BF16)
        vc = vs_ref[pl.ds(k0, CK), :].astype(BF16)
        sel = jnp.dot(memb, ex_ref[:, pl.ds(k0, CK)], preferred_element_type=F32) > 0.5
        ok = sel & ((k0 + lax.broadcasted_iota(jnp.int32, (T, CK), 1)) <= qpos)
        bias = jnp.where(ok, 0.0, NEG)
        s = lax.dot_general(q3, kc, _NT, preferred_element_type=F32) + jnp.concatenate([bias] * R, axis=0)
        m_new = jnp.maximum(m, jnp.max(s, axis=-1, keepdims=True))
        alpha = jnp.exp(m - m_new)
        p = jnp.exp(s - m_new)
        l = alpha * l + jnp.sum(p, axis=-1, keepdims=True)
        acc = alpha * acc + jnp.dot(p.astype(BF16), vc, preferred_element_type=F32)
        return m_new, l, acc

    m0 = jnp.full((R * T, 1), NEG, F32)
    _, l, acc = lax.fori_loop(0, n_chunks, chunk, (m0, jnp.zeros((R * T, 1), F32), jnp.zeros((R * T, DH), F32)))
    o_s = acc / l

    kw = jnp.concatenate([kw0_ref[...], kw1_ref[...], kw2_ref[...]], axis=0).astype(BF16)
    vw = jnp.concatenate([vw0_ref[...], vw1_ref[...], vw2_ref[...]], axis=0).astype(BF16)
    qp = i * T + lax.broadcasted_iota(jnp.int32, (T, 3 * T), 0)
    kp = (i - 2) * T + lax.broadcasted_iota(jnp.int32, (T, 3 * T), 1)
    biasw = jnp.where((kp <= qp) & (kp > qp - WINDOW) & (kp >= 0), 0.0, NEG)
    sw = lax.dot_general(q3, kw, _NT, preferred_element_type=F32) + jnp.concatenate([biasw] * R, axis=0)
    ew = jnp.exp(sw - jnp.max(sw, axis=-1, keepdims=True))
    pw = ew / jnp.sum(ew, axis=-1, keepdims=True)
    o_w = jnp.dot(pw.astype(BF16), vw, preferred_element_type=F32)

    gates = jax.nn.sigmoid(gt_ref[...])
    for r in range(R):
        gsel = _gate_cols(gates, g, r * 3)
        out = (gsel[0] * oc_ref[:, r * DH:(r + 1) * DH].astype(F32)
               + gsel[1] * o_s[r * T:(r + 1) * T] + gsel[2] * o_w[r * T:(r + 1) * T])
        mix_ref[:, r * DH:(r + 1) * DH] = out.astype(mix_ref.dtype)


def _gate_cols(gates, g, c0):
    lane = lax.broadcasted_iota(jnp.int32, gates.shape, 1)
    tgt = g * (B_GROUP * 3) + c0
    return [jnp.sum(jnp.where(lane == tgt + k, gates, 0.0), axis=-1, keepdims=True) for k in range(3)]


def sel_win_combine(projb, kvp, oc, mem, n_batch, *, T=256, CK=512):
    M = projb.shape[0]
    L = M // n_batch
    nt = L // T
    GW = B_GROUP * HEAD_DIM
    kcol = lambda base: (lambda b, g, i: (b, base + g))
    wspec = lambda base, d: pl.BlockSpec((T, HEAD_DIM), lambda b, g, i: (b * nt + jnp.maximum(i - d, 0), base + g))
    expand = (jnp.arange(L, dtype=jnp.int32)[None, :] // SLC_BLOCK
              == jnp.arange(128, dtype=jnp.int32)[:, None]).astype(BF16)
    return pl.pallas_call(
        functools.partial(_sel_win_kernel, T=T, CK=CK),
        grid=(n_batch, B_KV_GROUPS, nt),
        in_specs=[pl.BlockSpec((T, GW), lambda b, g, i: (b * nt + i, g)),
                  pl.BlockSpec((L, HEAD_DIM), kcol(8)),
                  pl.BlockSpec((L, HEAD_DIM), kcol(12)),
                  wspec(16, 2), wspec(16, 1), wspec(16, 0),
                  wspec(20, 2), wspec(20, 1), wspec(20, 0),
                  pl.BlockSpec((T, GW), lambda b, g, i: (b * nt + i, g)),
                  pl.BlockSpec((None, None, T, 128), lambda b, g, i: (b, g, i, 0)),
                  pl.BlockSpec((T, 128), lambda b, g, i: (b * nt + i, 16)),
                  pl.BlockSpec((128, L), lambda b, g, i: (0, 0))],
        out_specs=pl.BlockSpec((T, GW), lambda b, g, i: (b * nt + i, g)),
        out_shape=jax.ShapeDtypeStruct((M, TOK_WIDTH), BF16),
        compiler_params=_cparams("parallel", "parallel", "arbitrary"),
        name="sel_win_combine",
    )(projb, kvp, kvp, kvp, kvp, kvp, kvp, kvp, kvp, oc, mem, projb, expand)


def _mem_attn_kernel(q_ref, k_ref, v_ref, o_ref):
    DH = HEAD_DIM
    for h in range(MEM_HEADS):
        qh = (q_ref[:, h * DH:(h + 1) * DH] * SCALE).astype(BF16)
        kh = k_ref[:, h * DH:(h + 1) * DH].astype(BF16)
        vh = v_ref[:, h * DH:(h + 1) * DH].astype(BF16)
        s = lax.dot_general(qh, kh, _NT, preferred_element_type=F32)
        e = jnp.exp(s - jnp.max(s, axis=-1, keepdims=True))
        p = e / jnp.sum(e, axis=-1, keepdims=True)
        o_ref[:, h * DH:(h + 1) * DH] = jnp.dot(p.astype(BF16), vh, preferred_element_type=F32).astype(o_ref.dtype)


def mem_attn_prompt(proj, q_block, memkv, n_batch, *, T=512):
    M = proj.shape[0]
    nt = M // n_batch // T
    return pl.pallas_call(
        _mem_attn_kernel,
        grid=(n_batch, nt),
        in_specs=[pl.BlockSpec((T, MEM_WIDTH), lambda b, i: (b * nt + i, q_block)),
                  pl.BlockSpec((MEM_TOKENS, MEM_WIDTH), lambda b, i: (b, 0)),
                  pl.BlockSpec((MEM_TOKENS, MEM_WIDTH), lambda b, i: (b, 1))],
        out_specs=pl.BlockSpec((T, MEM_WIDTH), lambda b, i: (b * nt + i, 0)),
        out_shape=jax.ShapeDtypeStruct((M, MEM_WIDTH), BF16),
        compiler_params=_cparams("parallel", "parallel"),
        name="mem_attn_prompt",
    )(proj, memkv, memkv)


def _log_sigmoid(x):
    return jnp.minimum(x, 0.0) - jnp.log1p(jnp.exp(-jnp.abs(x)))


def _mlstm_kernel(q_ref, k_ref, v_ref, o_ref, gt_ref, gb_ref, hg_ref, mix_ref, c_ref, n_ref, m_ref):
    H, DH, CL = A_HEADS, HEAD_DIM, MLSTM_CHUNK

    @pl.when(pl.program_id(1) == 0)
    def _():
        c_ref[...] = jnp.zeros_like(c_ref)
        n_ref[...] = jnp.zeros_like(n_ref)
        m_ref[...] = jnp.zeros_like(m_ref)

    gt = (gt_ref[...] + gb_ref[...]).T
    ig = gt[0:16]
    lf = _log_sigmoid(gt[16:32])
    s_r = lax.broadcasted_iota(jnp.int32, (CL, CL), 0)
    s_c = lax.broadcasted_iota(jnp.int32, (CL, CL), 1)
    upper = jnp.where(s_r <= s_c, 1.0, 0.0).astype(BF16)
    lf_hi = lf.astype(BF16)
    r1 = lf - lf_hi.astype(F32)
    lf_mid = r1.astype(BF16)
    lf_lo = (r1 - lf_mid.astype(F32)).astype(BF16)
    b_rows = (jnp.dot(lf_hi, upper, preferred_element_type=F32) + jnp.dot(lf_mid, upper, preferred_element_type=F32)
              + jnp.dot(lf_lo, upper, preferred_element_type=F32))
    u_rows = ig - b_rows
    bl_all = jnp.sum(lf, axis=-1, keepdims=True)
    tril = s_c <= s_r
    for h in range(H):
        b_row = b_rows[h:h + 1]
        u_row = u_rows[h:h + 1]
        bl = bl_all[h:h + 1]
        m_h = m_ref[h:h + 1, :]
        b_col = jnp.broadcast_to(b_row, (CL, CL)).T
        dlog = jnp.where(tril, b_col + u_row, -jnp.inf)
        inter = b_col[:, 0:1] + m_h[:, 0:1]
        mt = jnp.maximum(inter, jnp.max(dlog, axis=-1, keepdims=True))
        a = jnp.exp(inter - mt)
        qh = q_ref[:, h * DH:(h + 1) * DH]
        kh = k_ref[:, h * DH:(h + 1) * DH] * SCALE
        qb, kb, vb = qh.astype(BF16), kh.astype(BF16), v_ref[:, h * DH:(h + 1) * DH].astype(BF16)
        w = jnp.exp(dlog - mt) * lax.dot_general(qb, kb, _NT, preferred_element_type=F32)
        c_old = c_ref[h]
        num = a * jnp.dot(qb, c_old.astype(BF16), preferred_element_type=F32) + jnp.dot(w.astype(BF16), vb, preferred_element_type=F32)
        n_old = n_ref[h:h + 1, :]
        den = a * jnp.sum(qh * n_old, axis=-1, keepdims=True) + jnp.sum(w, axis=-1, keepdims=True)
        hval = num / jnp.maximum(jnp.abs(den), jnp.exp(-mt))
        g_row = bl + u_row
        m_new = jnp.maximum(bl + m_h, jnp.max(g_row, axis=-1, keepdims=True))
        decay = jnp.exp(bl + m_h - m_new)
        wk_row = jnp.exp(g_row - m_new)
        k_t = kh.T
        c_ref[h] = decay[:, 0:1] * c_old + jnp.dot((k_t * wk_row).astype(BF16), vb, preferred_element_type=F32)
        wk8 = jnp.broadcast_to(wk_row, (8, CL)).astype(BF16)
        n_ref[h:h + 1, :] = decay * n_old + jnp.dot(wk8, kb, preferred_element_type=F32)[0:1]
        m_ref[h:h + 1, :] = m_new
        hn = hval * lax.rsqrt(jnp.mean(hval * hval, axis=-1, keepdims=True) + EPS) * hg_ref[:, h * DH:(h + 1) * DH]
        mix_ref[:, h * DH:(h + 1) * DH] = (jax.nn.sigmoid(o_ref[:, h * DH:(h + 1) * DH]) * hn).astype(mix_ref.dtype)


def mlstm_prompt(proja, gate_bias, head_gain, n_batch):
    M = proja.shape[0]
    nc = M // n_batch // MLSTM_CHUNK
    blk = lambda j: pl.BlockSpec((MLSTM_CHUNK, TOK_WIDTH), lambda b, c: (b * nc + c, j))
    return pl.pallas_call(
        _mlstm_kernel,
        grid=(n_batch, nc),
        in_specs=[blk(0), blk(1), blk(2), blk(3),
                  pl.BlockSpec((MLSTM_CHUNK, 128), lambda b, c: (b * nc + c, (4 * TOK_WIDTH + MEM_WIDTH) // 128)),
                  pl.BlockSpec((1, 128), lambda b, c: (0, 0)),
                  pl.BlockSpec((1, TOK_WIDTH), lambda b, c: (0, 0))],
        out_specs=[pl.BlockSpec((MLSTM_CHUNK, TOK_WIDTH), lambda b, c: (b * nc + c, 0)),
                   pl.BlockSpec((None, A_HEADS, HEAD_DIM, HEAD_DIM), lambda b, c: (b, 0, 0, 0)),
                   pl.BlockSpec((None, 16, HEAD_DIM), lambda b, c: (b, 0, 0)),
                   pl.BlockSpec((None, 16, 128), lambda b, c: (b, 0, 0))],
        out_shape=[jax.ShapeDtypeStruct((M, TOK_WIDTH), BF16),
                   jax.ShapeDtypeStruct((n_batch, A_HEADS, HEAD_DIM, HEAD_DIM), F32),
                   jax.ShapeDtypeStruct((n_batch, 16, HEAD_DIM), F32),
                   jax.ShapeDtypeStruct((n_batch, 16, 128), F32)],
        compiler_params=_cparams("parallel", "arbitrary"),
        name="mlstm_prompt",
    )(proja, proja, proja, proja, proja, gate_bias, head_gain.reshape(1, TOK_WIDTH))


def _lane_bcast_rows(row):
    return jnp.broadcast_to(row, (128, 128)).T


def _mlstm_sample_kernel(p_ref, gb_ref, m0_ref, hg_ref, c0_ref, n0_ref, mix_ref, c_ref, n_ref, m_ref):
    H, DH, W = A_HEADS, HEAD_DIM, TOK_WIDTH
    gcol = 4 * W + MEM_WIDTH
    gates = _lane_bcast_rows(p_ref[:, gcol:gcol + 128] + gb_ref[...])
    m_all = _lane_bcast_rows(m0_ref[...])
    m_ref[...] = jnp.zeros_like(m_ref)
    for h in range(H):
        ig = gates[h:h + 1]
        lf = _log_sigmoid(gates[16 + h:17 + h])
        m_old = m_all[h:h + 1]
        m_new = jnp.maximum(lf + m_old, ig)
        a = jnp.exp(lf + m_old - m_new)
        wgt = jnp.exp(ig - m_new)
        q = p_ref[:, h * DH:(h + 1) * DH]
        k = p_ref[:, W + h * DH:W + (h + 1) * DH] * SCALE
        v = p_ref[:, 2 * W + h * DH:2 * W + (h + 1) * DH]
        o = p_ref[:, 3 * W + h * DH:3 * W + (h + 1) * DH]
        c_old = c0_ref[h]
        n_old = n0_ref[h:h + 1, :]
        q_col = jnp.broadcast_to(q, (DH, DH)).T
        k_col = jnp.broadcast_to(k, (DH, DH)).T
        wqk = wgt * jnp.sum(q * k, axis=-1, keepdims=True)
        num = a * jnp.sum(q_col * c_old, axis=0, keepdims=True) + wqk * v
        den = a * jnp.sum(q * n_old, axis=-1, keepdims=True) + wqk
        hval = num / jnp.maximum(jnp.abs(den), jnp.exp(-m_new))
        c_ref[h] = a * c_old + (wgt * k_col) * v
        n_ref[h:h + 1, :] = a * n_old + wgt * k
        m_ref[h:h + 1, :] = m_new
        hn = hval * lax.rsqrt(jnp.mean(hval * hval, axis=-1, keepdims=True) + EPS) * hg_ref[:, h * DH:(h + 1) * DH]
        mix_ref[:, h * DH:(h + 1) * DH] = (jax.nn.sigmoid(o) * hn).astype(mix_ref.dtype)


def mlstm_sample(proja, gate_bias, m0, head_gain, c0, n0):
    Bs, N = proja.shape
    m0p = jnp.pad(m0, ((0, 0), (0, 128 - A_HEADS))).reshape(Bs, 1, 128)
    row = lambda n: pl.BlockSpec((None, 1, n), lambda b: (b, 0, 0))
    const = lambda n: pl.BlockSpec((1, n), lambda b: (0, 0))
    cspec = pl.BlockSpec((None, A_HEADS, HEAD_DIM, HEAD_DIM), lambda b: (b, 0, 0, 0))
    nspec = pl.BlockSpec((None, A_HEADS, HEAD_DIM), lambda b: (b, 0, 0))
    return pl.pallas_call(
        _mlstm_sample_kernel,
        grid=(Bs,),
        in_specs=[row(N), const(128), row(128), const(TOK_WIDTH), cspec, nspec],
        out_specs=[row(TOK_WIDTH), cspec, nspec, pl.BlockSpec((None, 16, 128), lambda b: (b, 0, 0))],
        out_shape=[jax.ShapeDtypeStruct((Bs, 1, TOK_WIDTH), BF16),
                   jax.ShapeDtypeStruct(c0.shape, F32), jax.ShapeDtypeStruct(n0.shape, F32),
                   jax.ShapeDtypeStruct((Bs, 16, 128), F32)],
        compiler_params=_cparams("parallel"),
        name="mlstm_sample",
    )(proja.reshape(Bs, 1, N), gate_bias, m0p, head_gain.reshape(1, TOK_WIDTH), c0, n0)


def _mem_attn_sample_kernel(q_ref, k_ref, v_ref, o_ref, *, nb):
    H, DH = MEM_HEADS, HEAD_DIM
    nrow = k_ref.shape[1]
    row_h = lax.broadcasted_iota(jnp.int32, (8, nrow), 0)
    col_h = lax.broadcasted_iota(jnp.int32, (8, nrow), 1) % H
    bias = jnp.where(row_h == col_h, 0.0, NEG)
    for b in range(nb):
        q8 = jnp.concatenate([q_ref[b:b + 1, h * DH:(h + 1) * DH] for h in range(H)] + [jnp.zeros((8 - H, DH), F32)], axis=0)
        s = lax.dot_general((q8 * SCALE).astype(BF16), k_ref[b].astype(BF16), _NT, preferred_element_type=F32) + bias
        e = jnp.exp(s - jnp.max(s, axis=-1, keepdims=True))
        p = e / jnp.sum(e, axis=-1, keepdims=True)
        o = jnp.dot(p.astype(BF16), v_ref[b].astype(BF16), preferred_element_type=F32)
        o_ref[b * H:(b + 1) * H, :] = o[0:H]


def mem_attn_sample(proj, q_block, cache_k, cache_v, layer, *, nb=8):
    Bs, N = proj.shape
    nrow = MEM_TOKENS * MEM_HEADS
    k2 = cache_k.reshape(cache_k.shape[0], Bs, nrow, HEAD_DIM)
    v2 = cache_v.reshape(cache_v.shape[0], Bs, nrow, HEAD_DIM)
    cspec = pl.BlockSpec((None, nb, nrow, HEAD_DIM), lambda i: (layer, i, 0, 0))
    return pl.pallas_call(
        functools.partial(_mem_attn_sample_kernel, nb=nb),
        grid=(Bs // nb,),
        in_specs=[pl.BlockSpec((None, nb, MEM_WIDTH), lambda i: (i, 0, q_block)), cspec, cspec],
        out_specs=pl.BlockSpec((nb * MEM_HEADS, HEAD_DIM), lambda i: (i, 0)),
        out_shape=jax.ShapeDtypeStruct((Bs * MEM_HEADS, HEAD_DIM), F32),
        compiler_params=_cparams("parallel"),
        name="mem_attn_sample",
    )(proj.reshape(Bs // nb, nb, N), k2, v2).reshape(Bs, MEM_WIDTH)


def compress_sample(cache, page_table, pe, w1, w2):
    Bs, n_pages = page_table.shape
    G, DH = B_KV_GROUPS, HEAD_DIM
    prow = PAGE_SIZE * G
    nrow = n_pages * (PAGE_SIZE // CMP_STRIDE) * G
    pe_rows, w1cat, w2b = _compress_weights(pe, w1, w2)
    pe_flat = jnp.transpose(pe_rows, (1, 0, 2)).reshape(8, CMP_STRIDE * DH)
    pages = cache.reshape(cache.shape[0], prow, DH)
    page_specs = [pl.BlockSpec((None, prow, DH), lambda b, pt, p=p: (pt[b, p], 0, 0)) for p in range(n_pages)]

    def body(pt_ref, *refs):
        _compress_paged_kernel(*refs, n_pages=n_pages)

    return pl.pallas_call(
        body,
        grid_spec=pltpu.PrefetchScalarGridSpec(
            num_scalar_prefetch=1, grid=(Bs,),
            in_specs=page_specs + [pl.BlockSpec((8, CMP_STRIDE * DH), lambda b, pt: (0, 0)),
                                   pl.BlockSpec((CMP_STRIDE * DH, 2 * DH), lambda b, pt: (0, 0)),
                                   pl.BlockSpec((DH, DH), lambda b, pt: (0, 0))],
            out_specs=pl.BlockSpec((None, nrow, DH), lambda b, pt: (b, 0, 0))),
        out_shape=jax.ShapeDtypeStruct((Bs, nrow, DH), F32),
        compiler_params=_cparams("parallel"),
        name="compress_sample",
    )(page_table, *([pages] * n_pages), pe_flat, w1cat, w2b)


def _compress_paged_kernel(*refs, n_pages):
    page_refs = refs[:n_pages]
    pe_ref, w1_ref, w2_ref, o_ref = refs[n_pages:]
    G, DH = B_KV_GROUPS, HEAD_DIM
    cpp = PAGE_SIZE // CMP_STRIDE
    rows = n_pages * cpp * G
    low = lax.broadcasted_iota(jnp.int32, (cpp // 2, 8, DH), 1) < G
    pieces = [[] for _ in range(CMP_STRIDE)]
    for p in range(n_pages):
        x5 = page_refs[p][...].reshape(cpp // 2, 2, CMP_STRIDE // 2, 8, DH)
        for k in range(CMP_STRIDE // 2):
            a, b = x5[:, 0, k], x5[:, 1, k]
            even = jnp.where(low, a, pltpu.roll(b, G, 1))
            odd = jnp.where(low, pltpu.roll(a, G, 1), b)
            pieces[2 * k].append(even.reshape(cpp * G, DH).astype(BF16))
            pieces[2 * k + 1].append(odd.reshape(cpp * G, DH).astype(BF16))
    lhs = jnp.concatenate([jnp.concatenate(pc, axis=0) for pc in pieces], axis=1)
    lhs = jnp.concatenate([lhs, pe_ref[...].astype(BF16)], axis=0)
    acc = jnp.dot(lhs, w1_ref[...], preferred_element_type=F32)
    first = acc[:rows, :DH] + acc[rows:rows + 1, :DH]
    second = acc[:rows, DH:] + acc[rows + 1:rows + 2, DH:]
    hid = jax.nn.gelu(first + pltpu.roll(second, rows - G, 0))
    o_ref[...] = jnp.dot(hid.astype(BF16), w2_ref[...], preferred_element_type=F32)


def _nsa_sample_kernel(pt_ref, *refs, n_pages):
    p_ref, kv_ref, kc_ref, vc_ref, ex_ref = refs[:5]
    sk_refs = refs[5:5 + n_pages]
    sv_refs = refs[5 + n_pages:5 + 2 * n_pages]
    wk_ref, wv_ref, mix_ref = refs[5 + 2 * n_pages:]
    G, R, DH, W = B_KV_GROUPS, B_GROUP, HEAD_DIM, TOK_WIDTH
    pos = PAST_LEN
    crow = kc_ref.shape[0]
    ncol = n_pages * sk_refs[0].shape[0]
    nsl = -(-(pos + 1) // SLC_BLOCK)
    wrow = wk_ref.shape[0]
    w_buf = wrow // G

    j_l = lax.broadcasted_iota(jnp.int32, (8, 128), 1)
    cur = pos // SLC_BLOCK
    forced = (j_l == 0) | (j_l == cur) | (j_l == cur - 1)
    jr = lax.broadcasted_iota(jnp.int32, (128, 128), 0)
    jc = lax.broadcasted_iota(jnp.int32, (128, 128), 1)
    wcol = lax.broadcasted_iota(jnp.int32, (8, wrow), 1)
    wvalid = (pos - w_buf + wcol // G) > pos - WINDOW
    gates = jax.nn.sigmoid(p_ref[:, W + MEM_WIDTH:W + MEM_WIDTH + 128])

    q32r = jnp.concatenate([p_ref[:, (g * R + r) * DH:(g * R + r + 1) * DH] if r < R else jnp.zeros((1, DH), F32)
                            for g in range(G) for r in range(8)], axis=0)
    q32 = q32r * SCALE
    q32b = q32.astype(BF16)

    c_row = lax.broadcasted_iota(jnp.int32, (8 * G, crow), 0) // 8
    c_col = lax.broadcasted_iota(jnp.int32, (8 * G, crow), 1)
    cvalid = (c_col % G == c_row) & ((c_col // G) * CMP_STRIDE + CMP_LEN - 1 <= pos)
    s = lax.dot_general(q32r.astype(BF16), kc_ref[...].astype(BF16), _NT, preferred_element_type=F32) * SCALE
    sm = jnp.where(cvalid, s, NEG)
    e = jnp.exp(sm - jnp.max(sm, axis=-1, keepdims=True))
    p = jnp.where(cvalid, e / jnp.sum(e, axis=-1, keepdims=True), 0.0)
    o_c = jnp.dot(p.astype(BF16), vc_ref[...].astype(BF16), preferred_element_type=F32)
    psum = jnp.concatenate([p[8 * g:8 * g + 1] + p[8 * g + 1:8 * g + 2] + p[8 * g + 2:8 * g + 3] for g in range(G)]
                           + [jnp.zeros((8 - G, crow), F32)], axis=0)
    n_o = (lax.broadcasted_iota(jnp.int32, (crow, 128), 0) // G) * CMP_STRIDE
    j_o = lax.broadcasted_iota(jnp.int32, (crow, 128), 1)
    ovl = jnp.where((n_o < (j_o + 1) * SLC_BLOCK) & (n_o + CMP_LEN > j_o * SLC_BLOCK), 1.0, 0.0).astype(BF16)
    p_hi, p_lo = _split2(psum)
    imp = jnp.dot(p_hi, ovl, preferred_element_type=F32) + jnp.dot(p_lo, ovl, preferred_element_type=F32)
    score_all = jnp.where(forced, FORCED_SCORE, jnp.where(j_l * SLC_BLOCK <= pos, imp, -1.0))
    score_all = jnp.where(j_l < nsl, score_all, -2.0)

    sb_rows, wb_rows = [], []
    for g in range(G):
        sc_c = jnp.broadcast_to(score_all[g:g + 1], (128, 128))
        sc_r = sc_c.T
        beats = (sc_r > sc_c) | ((sc_r == sc_c) & (jr < jc))
        rank = jnp.sum(jnp.where(beats, 1.0, 0.0), axis=0, keepdims=True)
        member = jnp.where((rank < min(N_SELECT, nsl)) & (j_l[0:1] < nsl), 1.0, 0.0)
        sb_rows.append(jnp.broadcast_to(member, (8, 128)))
        wb_rows.append(jnp.where(wvalid & (wcol % G == g), 0.0, NEG))

    mexp = jnp.dot(jnp.concatenate(sb_rows, axis=0).astype(BF16), ex_ref[...], preferred_element_type=F32)
    row_g = lax.broadcasted_iota(jnp.int32, (8 * G, ncol), 0) // 8
    col_g = lax.broadcasted_iota(jnp.int32, (8 * G, ncol), 1) % G
    sbias = jnp.where((mexp > 0.5) & (col_g == row_g), 0.0, NEG)

    def new_rows(base):
        return jnp.concatenate([jnp.broadcast_to(kv_ref[:, base + g * DH:base + (g + 1) * DH], (8, DH)) for g in range(G)], axis=0)

    def branch(k_rows, v_rows, bias, k_base, v_base):
        s = lax.dot_general(q32b, k_rows, _NT, preferred_element_type=F32) + bias
        s_new = jnp.sum(q32 * new_rows(k_base), axis=-1, keepdims=True)
        mx = jnp.maximum(jnp.max(s, axis=-1, keepdims=True), s_new)
        p = jnp.exp(s - mx)
        p_new = jnp.exp(s_new - mx)
        num = jnp.dot(p.astype(BF16), v_rows, preferred_element_type=F32) + p_new * new_rows(v_base)
        return num / (jnp.sum(p, axis=-1, keepdims=True) + p_new)

    ks = jnp.concatenate([r_[...].astype(BF16) for r_ in sk_refs], axis=0)
    vs = jnp.concatenate([r_[...].astype(BF16) for r_ in sv_refs], axis=0)
    o_s = branch(ks, vs, sbias, 2 * G * DH, 3 * G * DH)
    o_w = branch(wk_ref[...].astype(BF16), wv_ref[...].astype(BF16), jnp.concatenate(wb_rows, axis=0), 4 * G * DH, 5 * G * DH)
    for g in range(G):
        for r in range(R):
            h, row = g * R + r, g * 8 + r
            gsel = _gate_cols(gates, g, r * 3)
            out = gsel[0] * o_c[row:row + 1] + gsel[1] * o_s[row:row + 1] + gsel[2] * o_w[row:row + 1]
            mix_ref[:, h * DH:(h + 1) * DH] = out.astype(mix_ref.dtype)


def nsa_sample(projb, kvp, kc, vc, cache_slc_k, cache_slc_v, cache_win_k, cache_win_v, page_table):
    Bs, n_pages = page_table.shape
    G = B_KV_GROUPS
    prow = PAGE_SIZE * G
    wrow = cache_win_k.shape[1] * G
    sk = cache_slc_k.reshape(cache_slc_k.shape[0], prow, HEAD_DIM)
    sv = cache_slc_v.reshape(cache_slc_v.shape[0], prow, HEAD_DIM)
    wk = cache_win_k.reshape(Bs, wrow, HEAD_DIM)
    wv = cache_win_v.reshape(Bs, wrow, HEAD_DIM)
    ncol = n_pages * prow
    expand = (jnp.arange(ncol, dtype=jnp.int32)[None, :] // (G * SLC_BLOCK)
              == jnp.arange(128, dtype=jnp.int32)[:, None]).astype(BF16)
    row = lambda n: pl.BlockSpec((None, 1, n), lambda b, pt: (b, 0, 0))
    cspec = pl.BlockSpec((None, kc.shape[1], HEAD_DIM), lambda b, pt: (b, 0, 0))
    page_specs = [pl.BlockSpec((None, prow, HEAD_DIM), lambda b, pt, p=p: (pt[b, p], 0, 0)) for p in range(n_pages)]
    wspec = pl.BlockSpec((None, wrow, HEAD_DIM), lambda b, pt: (b, 0, 0))
    return pl.pallas_call(
        functools.partial(_nsa_sample_kernel, n_pages=n_pages),
        grid_spec=pltpu.PrefetchScalarGridSpec(
            num_scalar_prefetch=1, grid=(Bs,),
            in_specs=[row(projb.shape[1]), row(kvp.shape[1]), cspec, cspec,
                      pl.BlockSpec((128, ncol), lambda b, pt: (0, 0))] + page_specs + page_specs + [wspec, wspec],
            out_specs=row(TOK_WIDTH)),
        out_shape=jax.ShapeDtypeStruct((Bs, 1, TOK_WIDTH), BF16),
        compiler_params=_cparams("parallel"),
        name="nsa_sample",
    )(page_table, projb.reshape(Bs, 1, -1), kvp.reshape(Bs, 1, -1), kc, vc, expand, *([sk] * n_pages),
      *([sv] * n_pages), wk, wv).reshape(Bs, TOK_WIDTH)


def kernel(x_prompt, x_sample, mem_prompt, cache_mem_k, cache_mem_v, state_mlstm_C, state_mlstm_n, state_mlstm_m, cache_cmp_k, cache_cmp_v, cache_slc_k, cache_slc_v, cache_win_k, cache_win_v, page_table, ffn1_norm, ffn1_w_in, ffn1_w_out, ffn2_norm, ffn2_w_in, ffn2_w_out, mix_norm, a_w_in, a_b_i, a_b_f, a_head_norm, b_w_in, mem_norm, w_mem_kv, w_out, kv_norm, w_kv, cmp_pe_k, cmp_w1_k, cmp_w2_k, cmp_pe_v, cmp_w1_v, cmp_w2_v, final_norm):
    G, R, DH = B_KV_GROUPS, B_GROUP, HEAD_DIM
    ffn1_wi, ffn1_wo = ffn1_w_in.astype(BF16), ffn1_w_out.astype(BF16)
    ffn2_wi, ffn2_wo = ffn2_w_in.astype(BF16), ffn2_w_out.astype(BF16)
    w_o = w_out.astype(BF16)
    w_kv_b = w_kv.astype(BF16)
    w_mkv = w_mem_kv.astype(BF16)

    wa = a_w_in[0]
    zc = lambda n: jnp.zeros((D_MODEL, n), F32)
    a_wr = jnp.concatenate([wa[:, :4 * TOK_WIDTH], wa[:, 4 * TOK_WIDTH + 2 * A_HEADS:],
                            wa[:, 4 * TOK_WIDTH:4 * TOK_WIDTH + A_HEADS], zc(16 - A_HEADS),
                            wa[:, 4 * TOK_WIDTH + A_HEADS:4 * TOK_WIDTH + 2 * A_HEADS], zc(128 - 16 - A_HEADS),
                            zc(128)], axis=1).astype(BF16)
    gate_bias = jnp.concatenate([a_b_i[0], jnp.zeros((16 - A_HEADS,), F32), a_b_f[0],
                                 jnp.zeros((128 - 16 - A_HEADS,), F32)]).reshape(1, 128)
    wb = b_w_in[0]
    b_wr = jnp.concatenate([wb[:, :TOK_WIDTH], wb[:, TOK_WIDTH + 3 * B_HEADS:],
                            wb[:, TOK_WIDTH:TOK_WIDTH + 3 * B_HEADS], zc(256 - 3 * B_HEADS)], axis=1).astype(BF16)

    Bp, S = x_prompt.shape[:2]
    Mp = Bp * S
    memx = mem_prompt.reshape(Bp * MEM_TOKENS, D_MODEL)
    memkv = [norm_matmul(memx, mem_norm[l], w_mkv[l], tm=512, tn=512) for l in range(DEPTH)]
    mem5 = jnp.stack(memkv).reshape(DEPTH, Bp, MEM_TOKENS, 2, MEM_HEADS, DH)
    mem_k_p, mem_v_p = mem5[:, :, :, 0], mem5[:, :, :, 1]

    x = x_prompt.reshape(Mp, D_MODEL)
    x = ffn(x, ffn1_norm[0], ffn1_wi, ffn1_wo, 0, tm=512, tf=512)
    proja = norm_matmul(x, mix_norm[0], a_wr, tm=1024, tn=768)
    mix, C_fin, n_fin, m_fin = mlstm_prompt(proja, gate_bias, a_head_norm[0], Bp)
    mo = mem_attn_prompt(proja, 4 * TOK_WIDTH // MEM_WIDTH, memkv[0], Bp)
    x = outproj(x, mix, mo, w_o, 0, tm=512)
    x = ffn(x, ffn2_norm[0], ffn2_wi, ffn2_wo, 0, tm=512, tf=512)
    kvp, kv4 = kv_proj(x, kv_norm, w_kv_b, tm=1024)
    x = ffn(x, ffn1_norm[1], ffn1_wi, ffn1_wo, 1, tm=512, tf=512)
    projb = norm_matmul(x, mix_norm[1], b_wr, tm=1024, tn=768)
    kc = compress_prompt(kvp, 0, Bp, cmp_pe_k, cmp_w1_k, cmp_w2_k)
    vc = compress_prompt(kvp, 1, Bp, cmp_pe_v, cmp_w1_v, cmp_w2_v)
    oc, member = cmp_select_prompt(projb, kc, vc, Bp)
    mix = sel_win_combine(projb, kvp, oc, member, Bp)
    mo = mem_attn_prompt(projb, TOK_WIDTH // MEM_WIDTH, memkv[1], Bp)
    x = outproj(x, mix, mo, w_o, 1, tm=512)
    y_prompt = ffn(x, ffn2_norm[1], ffn2_wi, ffn2_wo, 1, final_norm, tm=512, tf=512).reshape(Bp, S, D_MODEL)
    C_p = C_fin[None]
    n_p = n_fin[None, :, :A_HEADS]
    m_p = m_fin[None, :, :A_HEADS, 0]
    kv_p = tuple(kv4[i].reshape(Bp, S, G, DH) for i in range(6))

    Bs = x_sample.shape[0]
    xs = x_sample.reshape(Bs, D_MODEL)
    xs = ffn(xs, ffn1_norm[0], ffn1_wi, ffn1_wo, 0, tm=Bs, tf=512)
    proja_s = norm_matmul(xs, mix_norm[0], a_wr, tm=Bs, tn=768)
    mix_s, C_new, n_new, m_new = mlstm_sample(proja_s, gate_bias, state_mlstm_m[0], a_head_norm[0],
                                              state_mlstm_C[0], state_mlstm_n[0])
    mo_s = mem_attn_sample(proja_s, 4 * TOK_WIDTH // MEM_WIDTH, cache_mem_k, cache_mem_v, 0)
    xs = outproj(xs, mix_s.reshape(Bs, TOK_WIDTH), mo_s, w_o, 0, tm=Bs)
    xs = ffn(xs, ffn2_norm[0], ffn2_wi, ffn2_wo, 0, tm=Bs, tf=512)
    kvp_s = norm_matmul(xs, kv_norm, w_kv_b, tm=Bs, tn=512)
    xs = ffn(xs, ffn1_norm[1], ffn1_wi, ffn1_wo, 1, tm=Bs, tf=512)
    projb_s = norm_matmul(xs, mix_norm[1], b_wr, tm=Bs, tn=768)
    kc_s = compress_sample(cache_cmp_k, page_table, cmp_pe_k, cmp_w1_k, cmp_w2_k)
    vc_s = compress_sample(cache_cmp_v, page_table, cmp_pe_v, cmp_w1_v, cmp_w2_v)
    mix_s = nsa_sample(projb_s, kvp_s, kc_s, vc_s, cache_slc_k, cache_slc_v, cache_win_k, cache_win_v, page_table)
    mo_s = mem_attn_sample(projb_s, TOK_WIDTH // MEM_WIDTH, cache_mem_k, cache_mem_v, 1)
    xs = outproj(xs, mix_s, mo_s, w_o, 1, tm=Bs)
    y_sample = ffn(xs, ffn2_norm[1], ffn2_wi, ffn2_wo, 1, final_norm, tm=Bs, tf=512).reshape(Bs, 1, D_MODEL)
    C_s = C_new[None]
    n_s = n_new[None]
    m_s = m_new[None, :, :A_HEADS, 0]
    kvs5 = kvp_s.reshape(Bs, 1, 6, G, DH)
    kv_s = tuple(kvs5[:, :, i] for i in range(6))

    w_p = min(WINDOW, S)
    cmp_k_p, cmp_v_p, slc_k_p, slc_v_p, win_k_all, win_v_all = kv_p
    win_k_p = win_k_all[:, S - w_p:]
    win_v_p = win_v_all[:, S - w_p:]
    cmp_k_s, cmp_v_s, slc_k_s, slc_v_s, win_k_s, win_v_s = kv_s
    return (y_prompt, y_sample, mem_k_p, mem_v_p, C_p, n_p, m_p, C_s, n_s, m_s,
            cmp_k_p, cmp_v_p, slc_k_p, slc_v_p, win_k_p, win_v_p,
            cmp_k_s, cmp_v_s, slc_k_s, slc_v_s, win_k_s, win_v_s)
```

```python
import functools

import jax
import jax.numpy as jnp
from jax import lax
from jax.experimental import pallas as pl
from jax.experimental.pallas import tpu as pltpu

D_MODEL = 2048
DEPTH = 2
PAST_LEN = 2048
PAGE_SIZE = 128
HEAD_DIM = 128
N_A_LAYERS = DEPTH // 2
MEM_TOKENS = 256
MEM_HEADS = 4
MEM_WIDTH = MEM_HEADS * HEAD_DIM
TOK_WIDTH = D_MODEL - MEM_WIDTH
A_HEADS = TOK_WIDTH // HEAD_DIM
B_HEADS = TOK_WIDTH // HEAD_DIM
B_KV_GROUPS = 4
B_GROUP = B_HEADS // B_KV_GROUPS
D_FF = ((8 * D_MODEL // 3 + 255) // 256) * 256
MLSTM_CHUNK = 128
CMP_STRIDE = 16
CMP_LEN = 2 * CMP_STRIDE
SLC_BLOCK = 64
N_SELECT = 16
WINDOW = 512
WIN_QBLK = 128
SEL_QBLK = 64
FORCED_SCORE = 1e4
NEG = -1e30
EPS = 1e-6
SCALE = HEAD_DIM ** -0.5

F32 = jnp.float32
BF16 = jnp.bfloat16
VMEM_LIMIT = 56 * 1024 * 1024


def _cparams(*sem):
    return pltpu.CompilerParams(dimension_semantics=sem, vmem_limit_bytes=VMEM_LIMIT)


def _rms_bf16(x, g):
    ms = jnp.mean(x * x, axis=-1, keepdims=True)
    return (x * lax.rsqrt(ms + EPS) * g).astype(BF16)


def _norm_matmul_kernel(x_ref, g_ref, w_ref, o_ref, h_ref):
    @pl.when(pl.program_id(1) == 0)
    def _():
        h_ref[...] = _rms_bf16(x_ref[...], g_ref[...])

    o_ref[...] = jnp.dot(h_ref[...], w_ref[...], preferred_element_type=F32).astype(o_ref.dtype)


def norm_matmul(x, g, w, *, tm, tn, out_dtype=F32):
    M, D = x.shape
    N = w.shape[1]
    return pl.pallas_call(
        _norm_matmul_kernel,
        grid=(M // tm, N // tn),
        in_specs=[pl.BlockSpec((tm, D), lambda i, j: (i, 0)),
                  pl.BlockSpec((1, D), lambda i, j: (0, 0)),
                  pl.BlockSpec((D, tn), lambda i, j: (0, j))],
        out_specs=pl.BlockSpec((tm, tn), lambda i, j: (i, j)),
        out_shape=jax.ShapeDtypeStruct((M, N), out_dtype),
        scratch_shapes=[pltpu.VMEM((tm, D), BF16)],
        compiler_params=_cparams("parallel", "arbitrary"),
        name="norm_matmul",
    )(x, g.reshape(1, D), w)


def _kv_proj_kernel(x_ref, g_ref, w_ref, o_ref, o4_ref, h_ref):
    @pl.when(pl.program_id(1) == 0)
    def _():
        h_ref[...] = _rms_bf16(x_ref[...], g_ref[...])

    y = jnp.dot(h_ref[...], w_ref[...], preferred_element_type=F32)
    o_ref[...] = y
    tm = y.shape[0]
    for g in range(B_KV_GROUPS):
        o4_ref[pl.ds(g, tm, stride=B_KV_GROUPS), :] = y[:, g * HEAD_DIM:(g + 1) * HEAD_DIM]


def kv_proj(x, g, w, *, tm):
    M, D = x.shape
    GD = B_KV_GROUPS * HEAD_DIM
    nbr = w.shape[1] // GD
    return pl.pallas_call(
        _kv_proj_kernel,
        grid=(M // tm, nbr),
        in_specs=[pl.BlockSpec((tm, D), lambda i, j: (i, 0)),
                  pl.BlockSpec((1, D), lambda i, j: (0, 0)),
                  pl.BlockSpec((D, GD), lambda i, j: (0, j))],
        out_specs=[pl.BlockSpec((tm, GD), lambda i, j: (i, j)),
                   pl.BlockSpec((None, tm * B_KV_GROUPS, HEAD_DIM), lambda i, j: (j, i, 0))],
        out_shape=[jax.ShapeDtypeStruct((M, nbr * GD), F32),
                   jax.ShapeDtypeStruct((nbr, M * B_KV_GROUPS, HEAD_DIM), F32)],
        scratch_shapes=[pltpu.VMEM((tm, D), BF16)],
        compiler_params=_cparams("parallel", "arbitrary"),
        name="kv_proj",
    )(x, g.reshape(1, D), w)


def _ffn_kernel(x_ref, g_ref, wg_ref, wu_ref, wo_ref, fg_ref, o_ref, h_ref, acc_ref, *, final_norm):
    f = pl.program_id(1)

    @pl.when(f == 0)
    def _():
        h_ref[...] = _rms_bf16(x_ref[...], g_ref[...])
        acc_ref[...] = jnp.zeros_like(acc_ref)

    h = h_ref[...]
    gate = jnp.dot(h, wg_ref[...], preferred_element_type=F32)
    up = jnp.dot(h, wu_ref[...], preferred_element_type=F32)
    act = (gate * jax.nn.sigmoid(gate) * up).astype(BF16)
    acc_ref[...] += jnp.dot(act, wo_ref[...], preferred_element_type=F32)

    @pl.when(f == pl.num_programs(1) - 1)
    def _():
        y = x_ref[...] + 0.5 * acc_ref[...]
        if final_norm:
            ms = jnp.mean(y * y, axis=-1, keepdims=True)
            y = y * lax.rsqrt(ms + EPS) * fg_ref[...]
        o_ref[...] = y


def ffn(x, g, w_in, w_out, layer, final_gain=None, *, tm, tf):
    M, D = x.shape
    F = w_out.shape[1]
    nf = F // tf
    fg = jnp.ones((D,), F32) if final_gain is None else final_gain
    return pl.pallas_call(
        functools.partial(_ffn_kernel, final_norm=final_gain is not None),
        grid=(M // tm, nf),
        in_specs=[pl.BlockSpec((tm, D), lambda i, f: (i, 0)),
                  pl.BlockSpec((1, D), lambda i, f: (0, 0)),
                  pl.BlockSpec((None, D, tf), lambda i, f: (layer, 0, f)),
                  pl.BlockSpec((None, D, tf), lambda i, f: (layer, 0, f + nf)),
                  pl.BlockSpec((None, tf, D), lambda i, f: (layer, f, 0)),
                  pl.BlockSpec((1, D), lambda i, f: (0, 0))],
        out_specs=pl.BlockSpec((tm, D), lambda i, f: (i, 0)),
        out_shape=jax.ShapeDtypeStruct((M, D), F32),
        scratch_shapes=[pltpu.VMEM((tm, D), BF16), pltpu.VMEM((tm, D), F32)],
        compiler_params=_cparams("parallel", "arbitrary"),
        name="ffn",
    )(x, g.reshape(1, D), w_in, w_in, w_out, fg.reshape(1, D))


def _outproj_kernel(x_ref, mix_ref, mo_ref, wa_ref, wb_ref, o_ref):
    y = jnp.dot(mix_ref[...].astype(BF16), wa_ref[...], preferred_element_type=F32)
    y += jnp.dot(mo_ref[...].astype(BF16), wb_ref[...], preferred_element_type=F32)
    o_ref[...] = x_ref[...] + y


def outproj(x, mix, mo, w, layer, *, tm):
    M, D = x.shape
    Ka, Kb = mix.shape[1], mo.shape[1]
    assert Ka % Kb == 0
    return pl.pallas_call(
        _outproj_kernel,
        grid=(M // tm,),
        in_specs=[pl.BlockSpec((tm, D), lambda i: (i, 0)),
                  pl.BlockSpec((tm, Ka), lambda i: (i, 0)),
                  pl.BlockSpec((tm, Kb), lambda i: (i, 0)),
                  pl.BlockSpec((None, Ka, D), lambda i: (layer, 0, 0)),
                  pl.BlockSpec((None, Kb, D), lambda i: (layer, Ka // Kb, 0))],
        out_specs=pl.BlockSpec((tm, D), lambda i: (i, 0)),
        out_shape=jax.ShapeDtypeStruct((M, D), F32),
        compiler_params=_cparams("parallel"),
        name="outproj",
    )(x, mix, mo, w, w)


def _compress_kernel(x_ref, pe_ref, w1_ref, w2_ref, o_ref, *, nch):
    G, DH = B_KV_GROUPS, HEAD_DIM
    lhs = jnp.concatenate(
        [jnp.concatenate([x_ref[:, c, g * DH:(g + 1) * DH] for g in range(G)] + [pe_ref[c]], axis=0).astype(BF16)
         for c in range(CMP_STRIDE)], axis=1)
    acc = jnp.dot(lhs, w1_ref[...], preferred_element_type=F32)
    bias_first = acc[G * nch:G * nch + 1, :DH]
    bias_second = acc[G * nch + 1:G * nch + 2, DH:]
    for g in range(G):
        first = acc[g * nch:(g + 1) * nch, :DH] + bias_first
        second = acc[g * nch:(g + 1) * nch, DH:] + bias_second
        hid = jax.nn.gelu(first + pltpu.roll(second, nch - 1, 0))
        o_ref[g] = jnp.dot(hid.astype(BF16), w2_ref[...], preferred_element_type=F32)


def _compress_weights(pe, w1, w2):
    DH = HEAD_DIM
    pe2 = pe.reshape(2, CMP_STRIDE, DH)
    pe_rows = jnp.concatenate([pe2[0][:, None], pe2[1][:, None], jnp.zeros((CMP_STRIDE, 6, DH), F32)], axis=1)
    w12 = w1.reshape(2, CMP_STRIDE, DH, -1)
    w1cat = jnp.concatenate([w12[0], w12[1]], axis=-1).astype(BF16)
    return pe_rows, w1cat.reshape(CMP_STRIDE * DH, -1), w2.astype(BF16)


def compress_prompt(kvp, col_block, n_batch, pe, w1, w2):
    M = kvp.shape[0]
    nch = M // n_batch // CMP_STRIDE
    pe_rows, w1cat, w2b = _compress_weights(pe, w1, w2)
    x = kvp.reshape(M // CMP_STRIDE, CMP_STRIDE, kvp.shape[1])
    return pl.pallas_call(
        functools.partial(_compress_kernel, nch=nch),
        grid=(n_batch,),
        in_specs=[pl.BlockSpec((nch, CMP_STRIDE, 512), lambda b: (b, 0, col_block)),
                  pl.BlockSpec((CMP_STRIDE, 8, HEAD_DIM), lambda b: (0, 0, 0)),
                  pl.BlockSpec((CMP_STRIDE * HEAD_DIM, 2 * HEAD_DIM), lambda b: (0, 0)),
                  pl.BlockSpec((HEAD_DIM, HEAD_DIM), lambda b: (0, 0))],
        out_specs=pl.BlockSpec((None, B_KV_GROUPS, nch, HEAD_DIM), lambda b: (b, 0, 0, 0)),
        out_shape=jax.ShapeDtypeStruct((n_batch, B_KV_GROUPS, nch, HEAD_DIM), F32),
        compiler_params=_cparams("parallel"),
        name="compress_prompt",
    )(x, pe_rows, w1cat, w2b)


_NT = (((1,), (1,)), ((), ()))


def _split2(x):
    hi = x.astype(BF16)
    return hi, (x - hi.astype(F32)).astype(BF16)


def _cmp_select_kernel(q_ref, kc_ref, vc_ref, oc_ref, mem_ref, sc_ref, *, T, nch, nsl, pos0):
    G, R, DH = B_KV_GROUPS, B_GROUP, HEAD_DIM
    base = pos0 + pl.program_id(1) * T
    pos_r = base + lax.broadcasted_iota(jnp.int32, (T, nch), 0)
    n_c = lax.broadcasted_iota(jnp.int32, (T, nch), 1)
    cmask = (n_c * CMP_STRIDE + CMP_LEN - 1) <= pos_r
    j_o = lax.broadcasted_iota(jnp.int32, (nsl, nch), 0)
    st_o = lax.broadcasted_iota(jnp.int32, (nsl, nch), 1) * CMP_STRIDE
    ovl_t = jnp.where((st_o < (j_o + 1) * SLC_BLOCK) & (st_o + CMP_LEN > j_o * SLC_BLOCK), 1.0, 0.0).astype(BF16)
    j_s = lax.broadcasted_iota(jnp.int32, (nsl, T), 0)
    pos_s = base + lax.broadcasted_iota(jnp.int32, (nsl, T), 1)
    cur = pos_s // SLC_BLOCK
    forced = (j_s == 0) | (j_s == cur) | (j_s == cur - 1)
    valid = j_s * SLC_BLOCK <= pos_s
    for g in range(G):
        kcg = kc_ref[g].astype(BF16)
        vcg = vc_ref[g].astype(BF16)
        psum = jnp.zeros((T, nch), F32)
        for r in range(R):
            h = g * R + r
            qh = q_ref[:, h * DH:(h + 1) * DH].astype(BF16)
            s = lax.dot_general(qh, kcg, _NT, preferred_element_type=F32) * SCALE
            sm = jnp.where(cmask, s, NEG)
            e = jnp.exp(sm - jnp.max(sm, axis=-1, keepdims=True))
            p = jnp.where(cmask, e / jnp.sum(e, axis=-1, keepdims=True), 0.0)
            oc_ref[:, h * DH:(h + 1) * DH] = jnp.dot(p.astype(BF16), vcg, preferred_element_type=F32).astype(oc_ref.dtype)
            psum = psum + p
        p_hi, p_lo = _split2(psum)
        imp_t = (lax.dot_general(ovl_t, p_hi, _NT, preferred_element_type=F32)
                 + lax.dot_general(ovl_t, p_lo, _NT, preferred_element_type=F32))
        score = jnp.where(forced, FORCED_SCORE, jnp.where(valid, imp_t, -1.0))
        sc_ref[...] = score
        rank = jnp.zeros((nsl, T), F32)
        for jp in range(nsl):
            row = sc_ref[jp:jp + 1, :]
            beats = (row > score) | ((row == score) & (j_s > jp))
            rank = rank + jnp.where(beats, 1.0, 0.0)
        mem_t = jnp.where(rank < min(N_SELECT, nsl), 1.0, 0.0)
        if nsl < 128:
            mem_t = jnp.concatenate([mem_t, jnp.zeros((128 - nsl, T), F32)], axis=0)
        mem_ref[g] = mem_t.T.astype(mem_ref.dtype)


def cmp_select_prompt(projb, kc, vc, n_batch, *, T=256):
    M = projb.shape[0]
    L = M // n_batch
    nt, nch, nsl = L // T, L // CMP_STRIDE, L // SLC_BLOCK
    return pl.pallas_call(
        functools.partial(_cmp_select_kernel, T=T, nch=nch, nsl=nsl, pos0=0),
        grid=(n_batch, nt),
        in_specs=[pl.BlockSpec((T, TOK_WIDTH), lambda b, i: (b * nt + i, 0)),
                  pl.BlockSpec((None, B_KV_GROUPS, nch, HEAD_DIM), lambda b, i: (b, 0, 0, 0)),
                  pl.BlockSpec((None, B_KV_GROUPS, nch, HEAD_DIM), lambda b, i: (b, 0, 0, 0))],
        out_specs=[pl.BlockSpec((T, TOK_WIDTH), lambda b, i: (b * nt + i, 0)),
                   pl.BlockSpec((None, B_KV_GROUPS, T, 128), lambda b, i: (b, 0, i, 0))],
        out_shape=[jax.ShapeDtypeStruct((M, TOK_WIDTH), BF16),
                   jax.ShapeDtypeStruct((n_batch, B_KV_GROUPS, L, 128), BF16)],
        scratch_shapes=[pltpu.VMEM((nsl, T), F32)],
        compiler_params=_cparams("parallel", "parallel"),
        name="cmp_select_prompt",
    )(projb, kc, vc)


def _sel_win_kernel(q_ref, ks_ref, vs_ref, kw0_ref, kw1_ref, kw2_ref, vw0_ref, vw1_ref, vw2_ref, oc_ref, mem_ref,
                    gt_ref, ex_ref, mix_ref, *, T, CK):
    R, DH = B_GROUP, HEAD_DIM
    g = pl.program_id(1)
    i = pl.program_id(2)
    q3 = jnp.concatenate([q_ref[:, r * DH:(r + 1) * DH] for r in range(R)], axis=0)
    q3 = (q3 * SCALE).astype(BF16)
    qpos = i * T + lax.broadcasted_iota(jnp.int32, (T, CK), 0)
    memb = mem_ref[...]
    n_chunks = (i * T + T + CK - 1) // CK

    def chunk(c, carry):
        m, l, acc = carry
        k0 = pl.multiple_of(c * CK, CK)
        kc = ks_ref[pl.ds(k0, CK), :].astype(BF16)
        vc = vs_ref[pl.ds(k0, CK), :].astype(BF16)
        sel = jnp.dot(memb, ex_ref[:, pl.ds(k0, CK)], preferred_element_type=F32) > 0.5
        ok = sel & ((k0 + lax.broadcasted_iota(jnp.int32, (T, CK), 1)) <= qpos)
        bias = jnp.where(ok, 0.0, NEG)
        s = lax.dot_general(q3, kc, _NT, preferred_element_type=F32) + jnp.concatenate([bias] * R, axis=0)
        m_new = jnp.maximum(m, jnp.max(s, axis=-1, keepdims=True))
        alpha = jnp.exp(m - m_new)
        p = jnp.exp(s - m_new)
        l = alpha * l + jnp.sum(p, axis=-1, keepdims=True)
        acc = alpha * acc + jnp.dot(p.astype(BF16), vc, preferred_element_type=F32)
        return m_new, l, acc

    m0 = jnp.full((R * T, 1), NEG, F32)
    _, l, acc = lax.fori_loop(0, n_chunks, chunk, (m0, jnp.zeros((R * T, 1), F32), jnp.zeros((R * T, DH), F32)))
    o_s = acc / l

    kw = jnp.concatenate([kw0_ref[...], kw1_ref[...], kw2_ref[...]], axis=0).astype(BF16)
    vw = jnp.concatenate([vw0_ref[...], vw1_ref[...], vw2_ref[...]], axis=0).astype(BF16)
    qp = i * T + lax.broadcasted_iota(jnp.int32, (T, 3 * T), 0)
    kp = (i - 2) * T + lax.broadcasted_iota(jnp.int32, (T, 3 * T), 1)
    biasw = jnp.where((kp <= qp) & (kp > qp - WINDOW) & (kp >= 0), 0.0, NEG)
    sw = lax.dot_general(q3, kw, _NT, preferred_element_type=F32) + jnp.concatenate([biasw] * R, axis=0)
    ew = jnp.exp(sw - jnp.max(sw, axis=-1, keepdims=True))
    pw = ew / jnp.sum(ew, axis=-1, keepdims=True)
    o_w = jnp.dot(pw.astype(BF16), vw, preferred_element_type=F32)

    gates = jax.nn.sigmoid(gt_ref[...])
    for r in range(R):
        gsel = _gate_cols(gates, g, r * 3)
        out = (gsel[0] * oc_ref[:, r * DH:(r + 1) * DH].astype(F32)
               + gsel[1] * o_s[r * T:(r + 1) * T] + gsel[2] * o_w[r * T:(r + 1) * T])
        mix_ref[:, r * DH:(r + 1) * DH] = out.astype(mix_ref.dtype)


def _gate_cols(gates, g, c0):
    lane = lax.broadcasted_iota(jnp.int32, gates.shape, 1)
    tgt = g * (B_GROUP * 3) + c0
    return [jnp.sum(jnp.where(lane == tgt + k, gates, 0.0), axis=-1, keepdims=True) for k in range(3)]


def sel_win_combine(projb, kvp, oc, mem, n_batch, *, T=256, CK=512):
    M = projb.shape[0]
    L = M // n_batch
    nt = L // T
    GW = B_GROUP * HEAD_DIM
    kcol = lambda base: (lambda b, g, i: (b, base + g))
    wspec = lambda base, d: pl.BlockSpec((T, HEAD_DIM), lambda b, g, i: (b * nt + jnp.maximum(i - d, 0), base + g))
    expand = (jnp.arange(L, dtype=jnp.int32)[None, :] // SLC_BLOCK
              == jnp.arange(128, dtype=jnp.int32)[:, None]).astype(BF16)
    return pl.pallas_call(
        functools.partial(_sel_win_kernel, T=T, CK=CK),
        grid=(n_batch, B_KV_GROUPS, nt),
        in_specs=[pl.BlockSpec((T, GW), lambda b, g, i: (b * nt + i, g)),
                  pl.BlockSpec((L, HEAD_DIM), kcol(8)),
                  pl.BlockSpec((L, HEAD_DIM), kcol(12)),
                  wspec(16, 2), wspec(16, 1), wspec(16, 0),
                  wspec(20, 2), wspec(20, 1), wspec(20, 0),
                  pl.BlockSpec((T, GW), lambda b, g, i: (b * nt + i, g)),
                  pl.BlockSpec((None, None, T, 128), lambda b, g, i: (b, g, i, 0)),
                  pl.BlockSpec((T, 128), lambda b, g, i: (b * nt + i, 16)),
                  pl.BlockSpec((128, L), lambda b, g, i: (0, 0))],
        out_specs=pl.BlockSpec((T, GW), lambda b, g, i: (b * nt + i, g)),
        out_shape=jax.ShapeDtypeStruct((M, TOK_WIDTH), BF16),
        compiler_params=_cparams("parallel", "parallel", "arbitrary"),
        name="sel_win_combine",
    )(projb, kvp, kvp, kvp, kvp, kvp, kvp, kvp, kvp, oc, mem, projb, expand)


def _mem_attn_kernel(q_ref, k_ref, v_ref, o_ref):
    DH = HEAD_DIM
    for h in range(MEM_HEADS):
        qh = (q_ref[:, h * DH:(h + 1) * DH] * SCALE).astype(BF16)
        kh = k_ref[:, h * DH:(h + 1) * DH].astype(BF16)
        vh = v_ref[:, h * DH:(h + 1) * DH].astype(BF16)
        s = lax.dot_general(qh, kh, _NT, preferred_element_type=F32)
        e = jnp.exp(s - jnp.max(s, axis=-1, keepdims=True))
        p = e / jnp.sum(e, axis=-1, keepdims=True)
        o_ref[:, h * DH:(h + 1) * DH] = jnp.dot(p.astype(BF16), vh, preferred_element_type=F32).astype(o_ref.dtype)


def mem_attn_prompt(proj, q_block, memkv, n_batch, *, T=512):
    M = proj.shape[0]
    nt = M // n_batch // T
    return pl.pallas_call(
        _mem_attn_kernel,
        grid=(n_batch, nt),
        in_specs=[pl.BlockSpec((T, MEM_WIDTH), lambda b, i: (b * nt + i, q_block)),
                  pl.BlockSpec((MEM_TOKENS, MEM_WIDTH), lambda b, i: (b, 0)),
                  pl.BlockSpec((MEM_TOKENS, MEM_WIDTH), lambda b, i: (b, 1))],
        out_specs=pl.BlockSpec((T, MEM_WIDTH), lambda b, i: (b * nt + i, 0)),
        out_shape=jax.ShapeDtypeStruct((M, MEM_WIDTH), BF16),
        compiler_params=_cparams("parallel", "parallel"),
        name="mem_attn_prompt",
    )(proj, memkv, memkv)


def _log_sigmoid(x):
    return jnp.minimum(x, 0.0) - jnp.log1p(jnp.exp(-jnp.abs(x)))


def _mlstm_kernel(q_ref, k_ref, v_ref, o_ref, gt_ref, gb_ref, hg_ref, mix_ref, c_ref, n_ref, m_ref, *, nb):
    @pl.when(pl.program_id(0) == 0)
    def _():
        c_ref[...] = jnp.zeros_like(c_ref)
        n_ref[...] = jnp.zeros_like(n_ref)
        m_ref[...] = jnp.zeros_like(m_ref)

    for b in range(nb):
        _mlstm_chunk(q_ref.at[b], k_ref.at[b], v_ref.at[b], o_ref.at[b], gt_ref.at[b], gb_ref, hg_ref,
                     mix_ref.at[b], c_ref.at[b], n_ref.at[b], m_ref.at[b])


def _mlstm_chunk(q_ref, k_ref, v_ref, o_ref, gt_ref, gb_ref, hg_ref, mix_ref, c_ref, n_ref, m_ref):
    H, DH, CL = A_HEADS, HEAD_DIM, MLSTM_CHUNK
    gt = (gt_ref[...] + gb_ref[...]).T
    ig = gt[0:16]
    lf = _log_sigmoid(gt[16:32])
    s_r = lax.broadcasted_iota(jnp.int32, (CL, CL), 0)
    s_c = lax.broadcasted_iota(jnp.int32, (CL, CL), 1)
    upper = jnp.where(s_r <= s_c, 1.0, 0.0).astype(BF16)
    lf_hi = lf.astype(BF16)
    r1 = lf - lf_hi.astype(F32)
    lf_mid = r1.astype(BF16)
    lf_lo = (r1 - lf_mid.astype(F32)).astype(BF16)
    b_rows = (jnp.dot(lf_hi, upper, preferred_element_type=F32) + jnp.dot(lf_mid, upper, preferred_element_type=F32)
              + jnp.dot(lf_lo, upper, preferred_element_type=F32))
    u_rows = ig - b_rows
    bl_all = jnp.sum(lf, axis=-1, keepdims=True)
    tril = s_c <= s_r
    ones_b = jnp.ones((CL, DH), BF16)
    for h in range(H):
        b_row = b_rows[h:h + 1]
        u_row = u_rows[h:h + 1]
        bl = bl_all[h:h + 1]
        m_h = m_ref[h:h + 1, :]
        b_col = jnp.broadcast_to(b_row, (CL, CL)).T
        dlog = jnp.where(tril, b_col + u_row, -jnp.inf)
        inter = b_col[:, 0:1] + m_h[:, 0:1]
        mt = jnp.maximum(inter, jnp.max(dlog, axis=-1, keepdims=True))
        a = jnp.exp(inter - mt)
        qh = q_ref[:, h * DH:(h + 1) * DH]
        kh = k_ref[:, h * DH:(h + 1) * DH] * SCALE
        qb, kb = qh.astype(BF16), kh.astype(BF16)
        v1 = jnp.concatenate([v_ref[:, h * DH:(h + 1) * DH].astype(BF16), ones_b], axis=1)
        w = jnp.exp(dlog - mt) * lax.dot_general(qb, kb, _NT, preferred_element_type=F32)
        c_old = c_ref[h]
        n_old = n_ref[h]
        qcn = jnp.dot(qb, jnp.concatenate([c_old.astype(BF16), n_old.astype(BF16)], axis=1), preferred_element_type=F32)
        wv1 = jnp.dot(w.astype(BF16), v1, preferred_element_type=F32)
        num = a * qcn[:, :DH] + wv1[:, :DH]
        den = a * qcn[:, DH:DH + 1] + wv1[:, DH:DH + 1]
        hval = num / jnp.maximum(jnp.abs(den), jnp.exp(-mt))
        g_row = bl + u_row
        m_new = jnp.maximum(bl + m_h, jnp.max(g_row, axis=-1, keepdims=True))
        decay = jnp.exp(bl + m_h - m_new)
        wk_row = jnp.exp(g_row - m_new)
        k_t = kh.T
        kv1 = jnp.dot((k_t * wk_row).astype(BF16), v1, preferred_element_type=F32)
        c_ref[h] = decay[:, 0:1] * c_old + kv1[:, :DH]
        n_ref[h] = decay * n_old + kv1[:, DH:]
        m_ref[h:h + 1, :] = m_new
        hn = hval * lax.rsqrt(jnp.mean(hval * hval, axis=-1, keepdims=True) + EPS) * hg_ref[:, h * DH:(h + 1) * DH]
        mix_ref[:, h * DH:(h + 1) * DH] = (jax.nn.sigmoid(o_ref[:, h * DH:(h + 1) * DH]) * hn).astype(mix_ref.dtype)


def mlstm_prompt(proja, gate_bias, head_gain, n_batch):
    M, N = proja.shape
    L = M // n_batch
    nc = L // MLSTM_CHUNK
    p3 = proja.reshape(n_batch, L, N)
    blk = lambda j: pl.BlockSpec((n_batch, MLSTM_CHUNK, TOK_WIDTH), lambda c: (0, c, j))
    full = lambda *tail: pl.BlockSpec((n_batch,) + tail, lambda c: (0,) * (1 + len(tail)))
    mix, c_fin, n_fin, m_fin = pl.pallas_call(
        functools.partial(_mlstm_kernel, nb=n_batch),
        grid=(nc,),
        in_specs=[blk(0), blk(1), blk(2), blk(3),
                  pl.BlockSpec((n_batch, MLSTM_CHUNK, 128), lambda c: (0, c, (4 * TOK_WIDTH + MEM_WIDTH) // 128)),
                  pl.BlockSpec((1, 128), lambda c: (0, 0)),
                  pl.BlockSpec((1, TOK_WIDTH), lambda c: (0, 0))],
        out_specs=[pl.BlockSpec((n_batch, MLSTM_CHUNK, TOK_WIDTH), lambda c: (0, c, 0)),
                   full(A_HEADS, HEAD_DIM, HEAD_DIM), full(A_HEADS, HEAD_DIM, 128), full(16, 128)],
        out_shape=[jax.ShapeDtypeStruct((n_batch, L, TOK_WIDTH), BF16),
                   jax.ShapeDtypeStruct((n_batch, A_HEADS, HEAD_DIM, HEAD_DIM), F32),
                   jax.ShapeDtypeStruct((n_batch, A_HEADS, HEAD_DIM, 128), F32),
                   jax.ShapeDtypeStruct((n_batch, 16, 128), F32)],
        compiler_params=_cparams("arbitrary"),
        name="mlstm_prompt",
    )(p3, p3, p3, p3, p3, gate_bias, head_gain.reshape(1, TOK_WIDTH))
    return mix.reshape(M, TOK_WIDTH), c_fin, n_fin[..., 0], m_fin


def _lane_bcast_rows(row):
    return jnp.broadcast_to(row, (128, 128)).T


def _mlstm_sample_kernel(p_ref, gb_ref, m0_ref, hg_ref, c0_ref, n0_ref, mix_ref, c_ref, n_ref, m_ref, *, nb):
    for b in range(nb):
        _mlstm_sample_one(p_ref.at[b], gb_ref, m0_ref.at[b], hg_ref, c0_ref.at[b], n0_ref.at[b],
                          mix_ref.at[b], c_ref.at[b], n_ref.at[b], m_ref.at[b])


def _mlstm_sample_one(p_ref, gb_ref, m0_ref, hg_ref, c0_ref, n0_ref, mix_ref, c_ref, n_ref, m_ref):
    H, DH, W = A_HEADS, HEAD_DIM, TOK_WIDTH
    gcol = 4 * W + MEM_WIDTH
    gates = _lane_bcast_rows(p_ref[:, gcol:gcol + 128] + gb_ref[...])
    m_all = _lane_bcast_rows(m0_ref[...])
    m_ref[...] = jnp.zeros_like(m_ref)
    for h in range(H):
        ig = gates[h:h + 1]
        lf = _log_sigmoid(gates[16 + h:17 + h])
        m_old = m_all[h:h + 1]
        m_new = jnp.maximum(lf + m_old, ig)
        a = jnp.exp(lf + m_old - m_new)
        wgt = jnp.exp(ig - m_new)
        q = p_ref[:, h * DH:(h + 1) * DH]
        k = p_ref[:, W + h * DH:W + (h + 1) * DH] * SCALE
        v = p_ref[:, 2 * W + h * DH:2 * W + (h + 1) * DH]
        o = p_ref[:, 3 * W + h * DH:3 * W + (h + 1) * DH]
        c_old = c0_ref[h]
        n_old = n0_ref[h:h + 1, :]
        q_col = jnp.broadcast_to(q, (DH, DH)).T
        k_col = jnp.broadcast_to(k, (DH, DH)).T
        wqk = wgt * jnp.sum(q * k, axis=-1, keepdims=True)
        num = a * jnp.sum(q_col * c_old, axis=0, keepdims=True) + wqk * v
        den = a * jnp.sum(q * n_old, axis=-1, keepdims=True) + wqk
        hval = num / jnp.maximum(jnp.abs(den), jnp.exp(-m_new))
        c_ref[h] = a * c_old + (wgt * k_col) * v
        n_ref[h:h + 1, :] = a * n_old + wgt * k
        m_ref[h:h + 1, :] = m_new
        hn = hval * lax.rsqrt(jnp.mean(hval * hval, axis=-1, keepdims=True) + EPS) * hg_ref[:, h * DH:(h + 1) * DH]
        mix_ref[:, h * DH:(h + 1) * DH] = (jax.nn.sigmoid(o) * hn).astype(mix_ref.dtype)


def mlstm_sample(proja, gate_bias, m0, head_gain, c0, n0, *, nb=4):
    Bs, N = proja.shape
    m0p = jnp.pad(m0, ((0, 0), (0, 128 - A_HEADS))).reshape(Bs, 1, 128)
    row = lambda n: pl.BlockSpec((nb, 1, n), lambda b: (b, 0, 0))
    const = lambda n: pl.BlockSpec((1, n), lambda b: (0, 0))
    cspec = pl.BlockSpec((nb, A_HEADS, HEAD_DIM, HEAD_DIM), lambda b: (b, 0, 0, 0))
    nspec = pl.BlockSpec((nb, A_HEADS, HEAD_DIM), lambda b: (b, 0, 0))
    return pl.pallas_call(
        functools.partial(_mlstm_sample_kernel, nb=nb),
        grid=(Bs // nb,),
        in_specs=[row(N), const(128), row(128), const(TOK_WIDTH), cspec, nspec],
        out_specs=[row(TOK_WIDTH), cspec, nspec, pl.BlockSpec((nb, 16, 128), lambda b: (b, 0, 0))],
        out_shape=[jax.ShapeDtypeStruct((Bs, 1, TOK_WIDTH), BF16),
                   jax.ShapeDtypeStruct(c0.shape, F32), jax.ShapeDtypeStruct(n0.shape, F32),
                   jax.ShapeDtypeStruct((Bs, 16, 128), F32)],
        compiler_params=_cparams("parallel"),
        name="mlstm_sample",
    )(proja.reshape(Bs, 1, N), gate_bias, m0p, head_gain.reshape(1, TOK_WIDTH), c0, n0)


def _mem_attn_sample_kernel(q_ref, k_ref, v_ref, o_ref, *, nb):
    H, DH = MEM_HEADS, HEAD_DIM
    nrow = k_ref.shape[1]
    row_h = lax.broadcasted_iota(jnp.int32, (8, nrow), 0)
    col_h = lax.broadcasted_iota(jnp.int32, (8, nrow), 1) % H
    bias = jnp.where(row_h == col_h, 0.0, NEG)
    for b in range(nb):
        q8 = jnp.concatenate([q_ref[b:b + 1, h * DH:(h + 1) * DH] for h in range(H)] + [jnp.zeros((8 - H, DH), F32)], axis=0)
        s = lax.dot_general((q8 * SCALE).astype(BF16), k_ref[b].astype(BF16), _NT, preferred_element_type=F32) + bias
        e = jnp.exp(s - jnp.max(s, axis=-1, keepdims=True))
        p = e / jnp.sum(e, axis=-1, keepdims=True)
        o = jnp.dot(p.astype(BF16), v_ref[b].astype(BF16), preferred_element_type=F32)
        o_ref[b * H:(b + 1) * H, :] = o[0:H]


def mem_attn_sample(proj, q_block, cache_k, cache_v, layer, *, nb=8):
    Bs, N = proj.shape
    nrow = MEM_TOKENS * MEM_HEADS
    k2 = cache_k.reshape(cache_k.shape[0], Bs, nrow, HEAD_DIM)
    v2 = cache_v.reshape(cache_v.shape[0], Bs, nrow, HEAD_DIM)
    cspec = pl.BlockSpec((None, nb, nrow, HEAD_DIM), lambda i: (layer, i, 0, 0))
    return pl.pallas_call(
        functools.partial(_mem_attn_sample_kernel, nb=nb),
        grid=(Bs // nb,),
        in_specs=[pl.BlockSpec((None, nb, MEM_WIDTH), lambda i: (i, 0, q_block)), cspec, cspec],
        out_specs=pl.BlockSpec((nb * MEM_HEADS, HEAD_DIM), lambda i: (i, 0)),
        out_shape=jax.ShapeDtypeStruct((Bs * MEM_HEADS, HEAD_DIM), F32),
        compiler_params=_cparams("parallel"),
        name="mem_attn_sample",
    )(proj.reshape(Bs // nb, nb, N), k2, v2).reshape(Bs, MEM_WIDTH)


def compress_sample(cache, page_table, pe, w1, w2):
    Bs, n_pages = page_table.shape
    G, DH = B_KV_GROUPS, HEAD_DIM
    prow = PAGE_SIZE * G
    nrow = n_pages * (PAGE_SIZE // CMP_STRIDE) * G
    pe_rows, w1cat, w2b = _compress_weights(pe, w1, w2)
    pe_flat = jnp.transpose(pe_rows, (1, 0, 2)).reshape(8, CMP_STRIDE * DH)
    pages = cache.reshape(cache.shape[0], prow, DH)
    page_specs = [pl.BlockSpec((None, prow, DH), lambda b, pt, p=p: (pt[b, p], 0, 0)) for p in range(n_pages)]

    def body(pt_ref, *refs):
        _compress_paged_kernel(*refs, n_pages=n_pages)

    return pl.pallas_call(
        body,
        grid_spec=pltpu.PrefetchScalarGridSpec(
            num_scalar_prefetch=1, grid=(Bs,),
            in_specs=page_specs + [pl.BlockSpec((8, CMP_STRIDE * DH), lambda b, pt: (0, 0)),
                                   pl.BlockSpec((CMP_STRIDE * DH, 2 * DH), lambda b, pt: (0, 0)),
                                   pl.BlockSpec((DH, DH), lambda b, pt: (0, 0))],
            out_specs=pl.BlockSpec((None, nrow, DH), lambda b, pt: (b, 0, 0))),
        out_shape=jax.ShapeDtypeStruct((Bs, nrow, DH), F32),
        compiler_params=_cparams("parallel"),
        name="compress_sample",
    )(page_table, *([pages] * n_pages), pe_flat, w1cat, w2b)


def _compress_paged_kernel(*refs, n_pages):
    page_refs = refs[:n_pages]
    pe_ref, w1_ref, w2_ref, o_ref = refs[n_pages:]
    G, DH = B_KV_GROUPS, HEAD_DIM
    cpp = PAGE_SIZE // CMP_STRIDE
    rows = n_pages * cpp * G
    low = lax.broadcasted_iota(jnp.int32, (cpp // 2, 8, DH), 1) < G
    pieces = [[] for _ in range(CMP_STRIDE)]
    for p in range(n_pages):
        x5 = page_refs[p][...].reshape(cpp // 2, 2, CMP_STRIDE // 2, 8, DH)
        for k in range(CMP_STRIDE // 2):
            a, b = x5[:, 0, k], x5[:, 1, k]
            even = jnp.where(low, a, pltpu.roll(b, G, 1))
            odd = jnp.where(low, pltpu.roll(a, G, 1), b)
            pieces[2 * k].append(even.reshape(cpp * G, DH).astype(BF16))
            pieces[2 * k + 1].append(odd.reshape(cpp * G, DH).astype(BF16))
    lhs = jnp.concatenate([jnp.concatenate(pc, axis=0) for pc in pieces], axis=1)
    lhs = jnp.concatenate([lhs, pe_ref[...].astype(BF16)], axis=0)
    acc = jnp.dot(lhs, w1_ref[...], preferred_element_type=F32)
    first = acc[:rows, :DH] + acc[rows:rows + 1, :DH]
    second = acc[:rows, DH:] + acc[rows + 1:rows + 2, DH:]
    hid = jax.nn.gelu(first + pltpu.roll(second, rows - G, 0))
    o_ref[...] = jnp.dot(hid.astype(BF16), w2_ref[...], preferred_element_type=F32)


def _nsa_sample_kernel(pt_ref, *refs, n_pages):
    p_ref, kv_ref, kc_ref, vc_ref, ex_ref = refs[:5]
    sk_refs = refs[5:5 + n_pages]
    sv_refs = refs[5 + n_pages:5 + 2 * n_pages]
    wk_ref, wv_ref, mix_ref = refs[5 + 2 * n_pages:]
    G, R, DH, W = B_KV_GROUPS, B_GROUP, HEAD_DIM, TOK_WIDTH
    pos = PAST_LEN
    crow = kc_ref.shape[0]
    ncol = n_pages * sk_refs[0].shape[0]
    nsl = -(-(pos + 1) // SLC_BLOCK)
    wrow = wk_ref.shape[0]
    w_buf = wrow // G

    j_l = lax.broadcasted_iota(jnp.int32, (8, 128), 1)
    cur = pos // SLC_BLOCK
    forced = (j_l == 0) | (j_l == cur) | (j_l == cur - 1)
    jr = lax.broadcasted_iota(jnp.int32, (128, 128), 0)
    jc = lax.broadcasted_iota(jnp.int32, (128, 128), 1)
    wcol = lax.broadcasted_iota(jnp.int32, (8, wrow), 1)
    wvalid = (pos - w_buf + wcol // G) > pos - WINDOW
    gates = jax.nn.sigmoid(p_ref[:, W + MEM_WIDTH:W + MEM_WIDTH + 128])

    q32r = jnp.concatenate([p_ref[:, (g * R + r) * DH:(g * R + r + 1) * DH] if r < R else jnp.zeros((1, DH), F32)
                            for g in range(G) for r in range(8)], axis=0)
    q32 = q32r * SCALE
    q32b = q32.astype(BF16)

    c_row = lax.broadcasted_iota(jnp.int32, (8 * G, crow), 0) // 8
    c_col = lax.broadcasted_iota(jnp.int32, (8 * G, crow), 1)
    cvalid = (c_col % G == c_row) & ((c_col // G) * CMP_STRIDE + CMP_LEN - 1 <= pos)
    s = lax.dot_general(q32r.astype(BF16), kc_ref[...].astype(BF16), _NT, preferred_element_type=F32) * SCALE
    sm = jnp.where(cvalid, s, NEG)
    e = jnp.exp(sm - jnp.max(sm, axis=-1, keepdims=True))
    p = jnp.where(cvalid, e / jnp.sum(e, axis=-1, keepdims=True), 0.0)
    o_c = jnp.dot(p.astype(BF16), vc_ref[...].astype(BF16), preferred_element_type=F32)
    psum = jnp.concatenate([p[8 * g:8 * g + 1] + p[8 * g + 1:8 * g + 2] + p[8 * g + 2:8 * g + 3] for g in range(G)]
                           + [jnp.zeros((8 - G, crow), F32)], axis=0)
    n_o = (lax.broadcasted_iota(jnp.int32, (crow, 128), 0) // G) * CMP_STRIDE
    j_o = lax.broadcasted_iota(jnp.int32, (crow, 128), 1)
    ovl = jnp.where((n_o < (j_o + 1) * SLC_BLOCK) & (n_o + CMP_LEN > j_o * SLC_BLOCK), 1.0, 0.0).astype(BF16)
    p_hi, p_lo = _split2(psum)
    imp = jnp.dot(p_hi, ovl, preferred_element_type=F32) + jnp.dot(p_lo, ovl, preferred_element_type=F32)
    score_all = jnp.where(forced, FORCED_SCORE, jnp.where(j_l * SLC_BLOCK <= pos, imp, -1.0))
    score_all = jnp.where(j_l < nsl, score_all, -2.0)

    sb_rows, wb_rows = [], []
    for g in range(G):
        sc_c = jnp.broadcast_to(score_all[g:g + 1], (128, 128))
        sc_r = sc_c.T
        beats = (sc_r > sc_c) | ((sc_r == sc_c) & (jr < jc))
        rank = jnp.sum(jnp.where(beats, 1.0, 0.0), axis=0, keepdims=True)
        member = jnp.where((rank < min(N_SELECT, nsl)) & (j_l[0:1] < nsl), 1.0, 0.0)
        sb_rows.append(jnp.broadcast_to(member, (8, 128)))
        wb_rows.append(jnp.where(wvalid & (wcol % G == g), 0.0, NEG))

    mexp = jnp.dot(jnp.concatenate(sb_rows, axis=0).astype(BF16), ex_ref[...], preferred_element_type=F32)
    row_g = lax.broadcasted_iota(jnp.int32, (8 * G, ncol), 0) // 8
    col_g = lax.broadcasted_iota(jnp.int32, (8 * G, ncol), 1) % G
    sbias = jnp.where((mexp > 0.5) & (col_g == row_g), 0.0, NEG)

    def new_rows(base):
        return jnp.concatenate([jnp.broadcast_to(kv_ref[:, base + g * DH:base + (g + 1) * DH], (8, DH)) for g in range(G)], axis=0)

    def branch(k_rows, v_rows, bias, k_base, v_base):
        s = lax.dot_general(q32b, k_rows, _NT, preferred_element_type=F32) + bias
        s_new = jnp.sum(q32 * new_rows(k_base), axis=-1, keepdims=True)
        mx = jnp.maximum(jnp.max(s, axis=-1, keepdims=True), s_new)
        p = jnp.exp(s - mx)
        p_new = jnp.exp(s_new - mx)
        num = jnp.dot(p.astype(BF16), v_rows, preferred_element_type=F32) + p_new * new_rows(v_base)
        return num / (jnp.sum(p, axis=-1, keepdims=True) + p_new)

    ks = jnp.concatenate([r_[...].astype(BF16) for r_ in sk_refs], axis=0)
    vs = jnp.concatenate([r_[...].astype(BF16) for r_ in sv_refs], axis=0)
    o_s = branch(ks, vs, sbias, 2 * G * DH, 3 * G * DH)
    o_w = branch(wk_ref[...].astype(BF16), wv_ref[...].astype(BF16), jnp.concatenate(wb_rows, axis=0), 4 * G * DH, 5 * G * DH)
    for g in range(G):
        for r in range(R):
            h, row = g * R + r, g * 8 + r
            gsel = _gate_cols(gates, g, r * 3)
            out = gsel[0] * o_c[row:row + 1] + gsel[1] * o_s[row:row + 1] + gsel[2] * o_w[row:row + 1]
            mix_ref[:, h * DH:(h + 1) * DH] = out.astype(mix_ref.dtype)


def nsa_sample(projb, kvp, kc, vc, cache_slc_k, cache_slc_v, cache_win_k, cache_win_v, page_table):
    Bs, n_pages = page_table.shape
    G = B_KV_GROUPS
    prow = PAGE_SIZE * G
    wrow = cache_win_k.shape[1] * G
    sk = cache_slc_k.reshape(cache_slc_k.shape[0], prow, HEAD_DIM)
    sv = cache_slc_v.reshape(cache_slc_v.shape[0], prow, HEAD_DIM)
    wk = cache_win_k.reshape(Bs, wrow, HEAD_DIM)
    wv = cache_win_v.reshape(Bs, wrow, HEAD_DIM)
    ncol = n_pages * prow
    expand = (jnp.arange(ncol, dtype=jnp.int32)[None, :] // (G * SLC_BLOCK)
              == jnp.arange(128, dtype=jnp.int32)[:, None]).astype(BF16)
    row = lambda n: pl.BlockSpec((None, 1, n), lambda b, pt: (b, 0, 0))
    cspec = pl.BlockSpec((None, kc.shape[1], HEAD_DIM), lambda b, pt: (b, 0, 0))
    page_specs = [pl.BlockSpec((None, prow, HEAD_DIM), lambda b, pt, p=p: (pt[b, p], 0, 0)) for p in range(n_pages)]
    wspec = pl.BlockSpec((None, wrow, HEAD_DIM), lambda b, pt: (b, 0, 0))
    return pl.pallas_call(
        functools.partial(_nsa_sample_kernel, n_pages=n_pages),
        grid_spec=pltpu.PrefetchScalarGridSpec(
            num_scalar_prefetch=1, grid=(Bs,),
            in_specs=[row(projb.shape[1]), row(kvp.shape[1]), cspec, cspec,
                      pl.BlockSpec((128, ncol), lambda b, pt: (0, 0))] + page_specs + page_specs + [wspec, wspec],
            out_specs=row(TOK_WIDTH)),
        out_shape=jax.ShapeDtypeStruct((Bs, 1, TOK_WIDTH), BF16),
        compiler_params=_cparams("parallel"),
        name="nsa_sample",
    )(page_table, projb.reshape(Bs, 1, -1), kvp.reshape(Bs, 1, -1), kc, vc, expand, *([sk] * n_pages),
      *([sv] * n_pages), wk, wv).reshape(Bs, TOK_WIDTH)


def kernel(x_prompt, x_sample, mem_prompt, cache_mem_k, cache_mem_v, state_mlstm_C, state_mlstm_n, state_mlstm_m, cache_cmp_k, cache_cmp_v, cache_slc_k, cache_slc_v, cache_win_k, cache_win_v, page_table, ffn1_norm, ffn1_w_in, ffn1_w_out, ffn2_norm, ffn2_w_in, ffn2_w_out, mix_norm, a_w_in, a_b_i, a_b_f, a_head_norm, b_w_in, mem_norm, w_mem_kv, w_out, kv_norm, w_kv, cmp_pe_k, cmp_w1_k, cmp_w2_k, cmp_pe_v, cmp_w1_v, cmp_w2_v, final_norm):
    G, R, DH = B_KV_GROUPS, B_GROUP, HEAD_DIM
    ffn1_wi, ffn1_wo = ffn1_w_in.astype(BF16), ffn1_w_out.astype(BF16)
    ffn2_wi, ffn2_wo = ffn2_w_in.astype(BF16), ffn2_w_out.astype(BF16)
    w_o = w_out.astype(BF16)
    w_kv_b = w_kv.astype(BF16)
    w_mkv = w_mem_kv.astype(BF16)

    wa = a_w_in[0]
    zc = lambda n: jnp.zeros((D_MODEL, n), F32)
    a_wr = jnp.concatenate([wa[:, :4 * TOK_WIDTH], wa[:, 4 * TOK_WIDTH + 2 * A_HEADS:],
                            wa[:, 4 * TOK_WIDTH:4 * TOK_WIDTH + A_HEADS], zc(16 - A_HEADS),
                            wa[:, 4 * TOK_WIDTH + A_HEADS:4 * TOK_WIDTH + 2 * A_HEADS], zc(128 - 16 - A_HEADS),
                            zc(128)], axis=1).astype(BF16)
    gate_bias = jnp.concatenate([a_b_i[0], jnp.zeros((16 - A_HEADS,), F32), a_b_f[0],
                                 jnp.zeros((128 - 16 - A_HEADS,), F32)]).reshape(1, 128)
    wb = b_w_in[0]
    b_wr = jnp.concatenate([wb[:, :TOK_WIDTH], wb[:, TOK_WIDTH + 3 * B_HEADS:],
                            wb[:, TOK_WIDTH:TOK_WIDTH + 3 * B_HEADS], zc(256 - 3 * B_HEADS)], axis=1).astype(BF16)

    Bp, S = x_prompt.shape[:2]
    Mp = Bp * S
    memx = mem_prompt.reshape(Bp * MEM_TOKENS, D_MODEL)
    memkv = [norm_matmul(memx, mem_norm[l], w_mkv[l], tm=512, tn=512) for l in range(DEPTH)]
    mem5 = jnp.stack(memkv).reshape(DEPTH, Bp, MEM_TOKENS, 2, MEM_HEADS, DH)
    mem_k_p, mem_v_p = mem5[:, :, :, 0], mem5[:, :, :, 1]

    x = x_prompt.reshape(Mp, D_MODEL)
    x = ffn(x, ffn1_norm[0], ffn1_wi, ffn1_wo, 0, tm=512, tf=512)
    proja = norm_matmul(x, mix_norm[0], a_wr, tm=1024, tn=768)
    mix, C_fin, n_fin, m_fin = mlstm_prompt(proja, gate_bias, a_head_norm[0], Bp)
    mo = mem_attn_prompt(proja, 4 * TOK_WIDTH // MEM_WIDTH, memkv[0], Bp)
    x = outproj(x, mix, mo, w_o, 0, tm=512)
    x = ffn(x, ffn2_norm[0], ffn2_wi, ffn2_wo, 0, tm=512, tf=512)
    kvp, kv4 = kv_proj(x, kv_norm, w_kv_b, tm=1024)
    x = ffn(x, ffn1_norm[1], ffn1_wi, ffn1_wo, 1, tm=512, tf=512)
    projb = norm_matmul(x, mix_norm[1], b_wr, tm=1024, tn=768)
    kc = compress_prompt(kvp, 0, Bp, cmp_pe_k, cmp_w1_k, cmp_w2_k)
    vc = compress_prompt(kvp, 1, Bp, cmp_pe_v, cmp_w1_v, cmp_w2_v)
    oc, member = cmp_select_prompt(projb, kc, vc, Bp)
    mix = sel_win_combine(projb, kvp, oc, member, Bp)
    mo = mem_attn_prompt(projb, TOK_WIDTH // MEM_WIDTH, memkv[1], Bp)
    x = outproj(x, mix, mo, w_o, 1, tm=512)
    y_prompt = ffn(x, ffn2_norm[1], ffn2_wi, ffn2_wo, 1, final_norm, tm=512, tf=512).reshape(Bp, S, D_MODEL)
    C_p = C_fin[None]
    n_p = n_fin[None]
    m_p = m_fin[None, :, :A_HEADS, 0]
    kv_p = tuple(kv4[i].reshape(Bp, S, G, DH) for i in range(6))

    Bs = x_sample.shape[0]
    xs = x_sample.reshape(Bs, D_MODEL)
    xs = ffn(xs, ffn1_norm[0], ffn1_wi, ffn1_wo, 0, tm=Bs, tf=512)
    proja_s = norm_matmul(xs, mix_norm[0], a_wr, tm=Bs, tn=768)
    mix_s, C_new, n_new, m_new = mlstm_sample(proja_s, gate_bias, state_mlstm_m[0], a_head_norm[0],
                                              state_mlstm_C[0], state_mlstm_n[0])
    mo_s = mem_attn_sample(proja_s, 4 * TOK_WIDTH // MEM_WIDTH, cache_mem_k, cache_mem_v, 0)
    xs = outproj(xs, mix_s.reshape(Bs, TOK_WIDTH), mo_s, w_o, 0, tm=Bs)
    xs = ffn(xs, ffn2_norm[0], ffn2_wi, ffn2_wo, 0, tm=Bs, tf=512)
    kvp_s = norm_matmul(xs, kv_norm, w_kv_b, tm=Bs, tn=512)
    xs = ffn(xs, ffn1_norm[1], ffn1_wi, ffn1_wo, 1, tm=Bs, tf=512)
    projb_s = norm_matmul(xs, mix_norm[1], b_wr, tm=Bs, tn=768)
    kc_s = compress_sample(cache_cmp_k, page_table, cmp_pe_k, cmp_w1_k, cmp_w2_k)
    vc_s = compress_sample(cache_cmp_v, page_table, cmp_pe_v, cmp_w1_v, cmp_w2_v)
    mix_s = nsa_sample(projb_s, kvp_s, kc_s, vc_s, cache_slc_k, cache_slc_v, cache_win_k, cache_win_v, page_table)
    mo_s = mem_attn_sample(projb_s, TOK_WIDTH // MEM_WIDTH, cache_mem_k, cache_mem_v, 1)
    xs = outproj(xs, mix_s, mo_s, w_o, 1, tm=Bs)
    y_sample = ffn(xs, ffn2_norm[1], ffn2_wi, ffn2_wo, 1, final_norm, tm=Bs, tf=512).reshape(Bs, 1, D_MODEL)
    C_s = C_new[None]
    n_s = n_new[None]
    m_s = m_new[None, :, :A_HEADS, 0]
    kvs5 = kvp_s.reshape(Bs, 1, 6, G, DH)
    kv_s = tuple(kvs5[:, :, i] for i in range(6))

    w_p = min(WINDOW, S)
    cmp_k_p, cmp_v_p, slc_k_p, slc_v_p, win_k_all, win_v_all = kv_p
    win_k_p = win_k_all[:, S - w_p:]
    win_v_p = win_v_all[:, S - w_p:]
    cmp_k_s, cmp_v_s, slc_k_s, slc_v_s, win_k_s, win_v_s = kv_s
    return (y_prompt, y_sample, mem_k_p, mem_v_p, C_p, n_p, m_p, C_s, n_s, m_s,
            cmp_k_p, cmp_v_p, slc_k_p, slc_v_p, win_k_p, win_v_p,
            cmp_k_s, cmp_v_s, slc_k_s, slc_v_s, win_k_s, win_v_s)
```

```python
import functools

import jax
import jax.numpy as jnp
from jax import lax
from jax.experimental import pallas as pl
from jax.experimental.pallas import tpu as pltpu

D_MODEL = 2048
DEPTH = 2
PAST_LEN = 2048
PAGE_SIZE = 128
HEAD_DIM = 128
N_A_LAYERS = DEPTH // 2
MEM_TOKENS = 256
MEM_HEADS = 4
MEM_WIDTH = MEM_HEADS * HEAD_DIM
TOK_WIDTH = D_MODEL - MEM_WIDTH
A_HEADS = TOK_WIDTH // HEAD_DIM
B_HEADS = TOK_WIDTH // HEAD_DIM
B_KV_GROUPS = 4
B_GROUP = B_HEADS // B_KV_GROUPS
D_FF = ((8 * D_MODEL // 3 + 255) // 256) * 256
MLSTM_CHUNK = 128
CMP_STRIDE = 16
CMP_LEN = 2 * CMP_STRIDE
SLC_BLOCK = 64
N_SELECT = 16
WINDOW = 512
WIN_QBLK = 128
SEL_QBLK = 64
FORCED_SCORE = 1e4
NEG = -1e30
EPS = 1e-6
SCALE = HEAD_DIM ** -0.5

F32 = jnp.float32
BF16 = jnp.bfloat16
VMEM_LIMIT = 56 * 1024 * 1024


def _cparams(*sem):
    return pltpu.CompilerParams(dimension_semantics=sem, vmem_limit_bytes=VMEM_LIMIT)


def _rms_bf16(x, g):
    ms = jnp.mean(x * x, axis=-1, keepdims=True)
    return (x * lax.rsqrt(ms + EPS) * g).astype(BF16)


def _norm_matmul_kernel(x_ref, g_ref, w_ref, o_ref, h_ref):
    @pl.when(pl.program_id(1) == 0)
    def _():
        h_ref[...] = _rms_bf16(x_ref[...], g_ref[...])

    o_ref[...] = jnp.dot(h_ref[...], w_ref[...], preferred_element_type=F32).astype(o_ref.dtype)


def norm_matmul(x, g, w, *, tm, tn, out_dtype=F32):
    M, D = x.shape
    N = w.shape[1]
    return pl.pallas_call(
        _norm_matmul_kernel,
        grid=(M // tm, N // tn),
        in_specs=[pl.BlockSpec((tm, D), lambda i, j: (i, 0)),
                  pl.BlockSpec((1, D), lambda i, j: (0, 0)),
                  pl.BlockSpec((D, tn), lambda i, j: (0, j))],
        out_specs=pl.BlockSpec((tm, tn), lambda i, j: (i, j)),
        out_shape=jax.ShapeDtypeStruct((M, N), out_dtype),
        scratch_shapes=[pltpu.VMEM((tm, D), BF16)],
        compiler_params=_cparams("parallel", "arbitrary"),
        name="norm_matmul",
    )(x, g.reshape(1, D), w)


def _kv_proj_kernel(x_ref, g_ref, w_ref, o_ref, o4_ref, h_ref):
    @pl.when(pl.program_id(1) == 0)
    def _():
        h_ref[...] = _rms_bf16(x_ref[...], g_ref[...])

    y = jnp.dot(h_ref[...], w_ref[...], preferred_element_type=F32)
    o_ref[...] = y
    tm = y.shape[0]
    for g in range(B_KV_GROUPS):
        o4_ref[pl.ds(g, tm, stride=B_KV_GROUPS), :] = y[:, g * HEAD_DIM:(g + 1) * HEAD_DIM]


def kv_proj(x, g, w, *, tm):
    M, D = x.shape
    GD = B_KV_GROUPS * HEAD_DIM
    nbr = w.shape[1] // GD
    return pl.pallas_call(
        _kv_proj_kernel,
        grid=(M // tm, nbr),
        in_specs=[pl.BlockSpec((tm, D), lambda i, j: (i, 0)),
                  pl.BlockSpec((1, D), lambda i, j: (0, 0)),
                  pl.BlockSpec((D, GD), lambda i, j: (0, j))],
        out_specs=[pl.BlockSpec((tm, GD), lambda i, j: (i, j)),
                   pl.BlockSpec((None, tm * B_KV_GROUPS, HEAD_DIM), lambda i, j: (j, i, 0))],
        out_shape=[jax.ShapeDtypeStruct((M, nbr * GD), F32),
                   jax.ShapeDtypeStruct((nbr, M * B_KV_GROUPS, HEAD_DIM), F32)],
        scratch_shapes=[pltpu.VMEM((tm, D), BF16)],
        compiler_params=_cparams("parallel", "arbitrary"),
        name="kv_proj",
    )(x, g.reshape(1, D), w)


def _ffn_kernel(x_ref, g_ref, wg_ref, wu_ref, wo_ref, fg_ref, o_ref, h_ref, acc_ref, *, final_norm):
    f = pl.program_id(1)

    @pl.when(f == 0)
    def _():
        h_ref[...] = _rms_bf16(x_ref[...], g_ref[...])
        acc_ref[...] = jnp.zeros_like(acc_ref)

    h = h_ref[...]
    gate = jnp.dot(h, wg_ref[...], preferred_element_type=F32)
    up = jnp.dot(h, wu_ref[...], preferred_element_type=F32)
    act = (gate * jax.nn.sigmoid(gate) * up).astype(BF16)
    acc_ref[...] += jnp.dot(act, wo_ref[...], preferred_element_type=F32)

    @pl.when(f == pl.num_programs(1) - 1)
    def _():
        y = x_ref[...] + 0.5 * acc_ref[...]
        if final_norm:
            ms = jnp.mean(y * y, axis=-1, keepdims=True)
            y = y * lax.rsqrt(ms + EPS) * fg_ref[...]
        o_ref[...] = y


def ffn(x, g, w_in, w_out, layer, final_gain=None, *, tm, tf):
    M, D = x.shape
    F = w_out.shape[1]
    nf = F // tf
    fg = jnp.ones((D,), F32) if final_gain is None else final_gain
    return pl.pallas_call(
        functools.partial(_ffn_kernel, final_norm=final_gain is not None),
        grid=(M // tm, nf),
        in_specs=[pl.BlockSpec((tm, D), lambda i, f: (i, 0)),
                  pl.BlockSpec((1, D), lambda i, f: (0, 0)),
                  pl.BlockSpec((None, D, tf), lambda i, f: (layer, 0, f)),
                  pl.BlockSpec((None, D, tf), lambda i, f: (layer, 0, f + nf)),
                  pl.BlockSpec((None, tf, D), lambda i, f: (layer, f, 0)),
                  pl.BlockSpec((1, D), lambda i, f: (0, 0))],
        out_specs=pl.BlockSpec((tm, D), lambda i, f: (i, 0)),
        out_shape=jax.ShapeDtypeStruct((M, D), F32),
        scratch_shapes=[pltpu.VMEM((tm, D), BF16), pltpu.VMEM((tm, D), F32)],
        compiler_params=_cparams("parallel", "arbitrary"),
        name="ffn",
    )(x, g.reshape(1, D), w_in, w_in, w_out, fg.reshape(1, D))


def _outproj_kernel(x_ref, mix_ref, mo_ref, wa_ref, wb_ref, o_ref):
    y = jnp.dot(mix_ref[...].astype(BF16), wa_ref[...], preferred_element_type=F32)
    y += jnp.dot(mo_ref[...].astype(BF16), wb_ref[...], preferred_element_type=F32)
    o_ref[...] = x_ref[...] + y


def outproj(x, mix, mo, w, layer, *, tm):
    M, D = x.shape
    Ka, Kb = mix.shape[1], mo.shape[1]
    assert Ka % Kb == 0
    return pl.pallas_call(
        _outproj_kernel,
        grid=(M // tm,),
        in_specs=[pl.BlockSpec((tm, D), lambda i: (i, 0)),
                  pl.BlockSpec((tm, Ka), lambda i: (i, 0)),
                  pl.BlockSpec((tm, Kb), lambda i: (i, 0)),
                  pl.BlockSpec((None, Ka, D), lambda i: (layer, 0, 0)),
                  pl.BlockSpec((None, Kb, D), lambda i: (layer, Ka // Kb, 0))],
        out_specs=pl.BlockSpec((tm, D), lambda i: (i, 0)),
        out_shape=jax.ShapeDtypeStruct((M, D), F32),
        compiler_params=_cparams("parallel"),
        name="outproj",
    )(x, mix, mo, w, w)


def _compress_kernel(x_ref, pe_ref, w1_ref, w2_ref, o_ref, *, nch):
    G, DH = B_KV_GROUPS, HEAD_DIM
    lhs = jnp.concatenate(
        [jnp.concatenate([x_ref[:, c, g * DH:(g + 1) * DH] for g in range(G)] + [pe_ref[c]], axis=0).astype(BF16)
         for c in range(CMP_STRIDE)], axis=1)
    acc = jnp.dot(lhs, w1_ref[...], preferred_element_type=F32)
    bias_first = acc[G * nch:G * nch + 1, :DH]
    bias_second = acc[G * nch + 1:G * nch + 2, DH:]
    for g in range(G):
        first = acc[g * nch:(g + 1) * nch, :DH] + bias_first
        second = acc[g * nch:(g + 1) * nch, DH:] + bias_second
        hid = jax.nn.gelu(first + pltpu.roll(second, nch - 1, 0))
        o_ref[g] = jnp.dot(hid.astype(BF16), w2_ref[...], preferred_element_type=F32)


def _compress_weights(pe, w1, w2):
    DH = HEAD_DIM
    pe2 = pe.reshape(2, CMP_STRIDE, DH)
    pe_rows = jnp.concatenate([pe2[0][:, None], pe2[1][:, None], jnp.zeros((CMP_STRIDE, 6, DH), F32)], axis=1)
    w12 = w1.reshape(2, CMP_STRIDE, DH, -1)
    w1cat = jnp.concatenate([w12[0], w12[1]], axis=-1).astype(BF16)
    return pe_rows, w1cat.reshape(CMP_STRIDE * DH, -1), w2.astype(BF16)


def compress_prompt(kvp, col_block, n_batch, pe, w1, w2):
    M = kvp.shape[0]
    nch = M // n_batch // CMP_STRIDE
    pe_rows, w1cat, w2b = _compress_weights(pe, w1, w2)
    x = kvp.reshape(M // CMP_STRIDE, CMP_STRIDE, kvp.shape[1])
    return pl.pallas_call(
        functools.partial(_compress_kernel, nch=nch),
        grid=(n_batch,),
        in_specs=[pl.BlockSpec((nch, CMP_STRIDE, 512), lambda b: (b, 0, col_block)),
                  pl.BlockSpec((CMP_STRIDE, 8, HEAD_DIM), lambda b: (0, 0, 0)),
                  pl.BlockSpec((CMP_STRIDE * HEAD_DIM, 2 * HEAD_DIM), lambda b: (0, 0)),
                  pl.BlockSpec((HEAD_DIM, HEAD_DIM), lambda b: (0, 0))],
        out_specs=pl.BlockSpec((None, B_KV_GROUPS, nch, HEAD_DIM), lambda b: (b, 0, 0, 0)),
        out_shape=jax.ShapeDtypeStruct((n_batch, B_KV_GROUPS, nch, HEAD_DIM), F32),
        compiler_params=_cparams("parallel"),
        name="compress_prompt",
    )(x, pe_rows, w1cat, w2b)


_NT = (((1,), (1,)), ((), ()))


def _split2(x):
    hi = x.astype(BF16)
    return hi, (x - hi.astype(F32)).astype(BF16)


def _cmp_select_kernel(q_ref, kc_ref, vc_ref, oc_ref, mem_ref, sc_ref, *, T, nch, nsl, pos0):
    G, R, DH = B_KV_GROUPS, B_GROUP, HEAD_DIM
    base = pos0 + pl.program_id(1) * T
    pos_r = base + lax.broadcasted_iota(jnp.int32, (T, nch), 0)
    n_c = lax.broadcasted_iota(jnp.int32, (T, nch), 1)
    cmask = (n_c * CMP_STRIDE + CMP_LEN - 1) <= pos_r
    j_o = lax.broadcasted_iota(jnp.int32, (nsl, nch), 0)
    st_o = lax.broadcasted_iota(jnp.int32, (nsl, nch), 1) * CMP_STRIDE
    ovl_t = jnp.where((st_o < (j_o + 1) * SLC_BLOCK) & (st_o + CMP_LEN > j_o * SLC_BLOCK), 1.0, 0.0).astype(BF16)
    j_s = lax.broadcasted_iota(jnp.int32, (nsl, T), 0)
    pos_s = base + lax.broadcasted_iota(jnp.int32, (nsl, T), 1)
    cur = pos_s // SLC_BLOCK
    forced = (j_s == 0) | (j_s == cur) | (j_s == cur - 1)
    valid = j_s * SLC_BLOCK <= pos_s
    for g in range(G):
        kcg = kc_ref[g].astype(BF16)
        vcg = vc_ref[g].astype(BF16)
        psum = jnp.zeros((T, nch), F32)
        for r in range(R):
            h = g * R + r
            qh = q_ref[:, h * DH:(h + 1) * DH].astype(BF16)
            s = lax.dot_general(qh, kcg, _NT, preferred_element_type=F32) * SCALE
            sm = jnp.where(cmask, s, NEG)
            e = jnp.exp(sm - jnp.max(sm, axis=-1, keepdims=True))
            p = jnp.where(cmask, e / jnp.sum(e, axis=-1, keepdims=True), 0.0)
            oc_ref[:, h * DH:(h + 1) * DH] = jnp.dot(p.astype(BF16), vcg, preferred_element_type=F32).astype(oc_ref.dtype)
            psum = psum + p
        p_hi, p_lo = _split2(psum)
        imp_t = (lax.dot_general(ovl_t, p_hi, _NT, preferred_element_type=F32)
                 + lax.dot_general(ovl_t, p_lo, _NT, preferred_element_type=F32))
        score = jnp.where(forced, FORCED_SCORE, jnp.where(valid, imp_t, -1.0))
        sc_ref[...] = score
        rank = jnp.zeros((nsl, T), F32)
        for jp in range(nsl):
            row = sc_ref[jp:jp + 1, :]
            beats = (row > score) | ((row == score) & (j_s > jp))
            rank = rank + jnp.where(beats, 1.0, 0.0)
        mem_t = jnp.where(rank < min(N_SELECT, nsl), 1.0, 0.0)
        if nsl < 128:
            mem_t = jnp.concatenate([mem_t, jnp.zeros((128 - nsl, T), F32)], axis=0)
        mem_ref[g] = mem_t.T.astype(mem_ref.dtype)


def cmp_select_prompt(projb, kc, vc, n_batch, *, T=256):
    M = projb.shape[0]
    L = M // n_batch
    nt, nch, nsl = L // T, L // CMP_STRIDE, L // SLC_BLOCK
    return pl.pallas_call(
        functools.partial(_cmp_select_kernel, T=T, nch=nch, nsl=nsl, pos0=0),
        grid=(n_batch, nt),
        in_specs=[pl.BlockSpec((T, TOK_WIDTH), lambda b, i: (b * nt + i, 0)),
                  pl.BlockSpec((None, B_KV_GROUPS, nch, HEAD_DIM), lambda b, i: (b, 0, 0, 0)),
                  pl.BlockSpec((None, B_KV_GROUPS, nch, HEAD_DIM), lambda b, i: (b, 0, 0, 0))],
        out_specs=[pl.BlockSpec((T, TOK_WIDTH), lambda b, i: (b * nt + i, 0)),
                   pl.BlockSpec((None, B_KV_GROUPS, T, 128), lambda b, i: (b, 0, i, 0))],
        out_shape=[jax.ShapeDtypeStruct((M, TOK_WIDTH), BF16),
                   jax.ShapeDtypeStruct((n_batch, B_KV_GROUPS, L, 128), BF16)],
        scratch_shapes=[pltpu.VMEM((nsl, T), F32)],
        compiler_params=_cparams("parallel", "parallel"),
        name="cmp_select_prompt",
    )(projb, kc, vc)


def _sel_win_kernel(q_ref, ks_ref, vs_ref, kw0_ref, kw1_ref, kw2_ref, vw0_ref, vw1_ref, vw2_ref, oc_ref, mem_ref,
                    gt_ref, ex_ref, mix_ref, *, T, CK):
    R, DH = B_GROUP, HEAD_DIM
    g = pl.program_id(1)
    i = pl.program_id(2)
    q3 = jnp.concatenate([q_ref[:, r * DH:(r + 1) * DH] for r in range(R)], axis=0)
    q3 = (q3 * SCALE).astype(BF16)
    qpos = i * T + lax.broadcasted_iota(jnp.int32, (T, CK), 0)
    memb = mem_ref[...]
    n_chunks = (i * T + T + CK - 1) // CK

    def chunk(c, carry):
        m, l, acc = carry
        k0 = pl.multiple_of(c * CK, CK)
        kc = ks_ref[pl.ds(k0, CK), :].astype(BF16)
        vc = vs_ref[pl.ds(k0, CK), :].astype(BF16)
        sel = jnp.dot(memb, ex_ref[:, pl.ds(k0, CK)], preferred_element_type=F32) > 0.5
        ok = sel & ((k0 + lax.broadcasted_iota(jnp.int32, (T, CK), 1)) <= qpos)
        bias = jnp.where(ok, 0.0, NEG)
        s = lax.dot_general(q3, kc, _NT, preferred_element_type=F32) + jnp.concatenate([bias] * R, axis=0)
        m_new = jnp.maximum(m, jnp.max(s, axis=-1, keepdims=True))
        alpha = jnp.exp(m - m_new)
        p = jnp.exp(s - m_new)
        l = alpha * l + jnp.sum(p, axis=-1, keepdims=True)
        acc = alpha * acc + jnp.dot(p.astype(BF16), vc, preferred_element_type=F32)
        return m_new, l, acc

    m0 = jnp.full((R * T, 1), NEG, F32)
    _, l, acc = lax.fori_loop(0, n_chunks, chunk, (m0, jnp.zeros((R * T, 1), F32), jnp.zeros((R * T, DH), F32)))
    o_s = acc / l

    kw = jnp.concatenate([kw0_ref[...], kw1_ref[...], kw2_ref[...]], axis=0).astype(BF16)
    vw = jnp.concatenate([vw0_ref[...], vw1_ref[...], vw2_ref[...]], axis=0).astype(BF16)
    qp = i * T + lax.broadcasted_iota(jnp.int32, (T, 3 * T), 0)
    kp = (i - 2) * T + lax.broadcasted_iota(jnp.int32, (T, 3 * T), 1)
    biasw = jnp.where((kp <= qp) & (kp > qp - WINDOW) & (kp >= 0), 0.0, NEG)
    sw = lax.dot_general(q3, kw, _NT, preferred_element_type=F32) + jnp.concatenate([biasw] * R, axis=0)
    ew = jnp.exp(sw - jnp.max(sw, axis=-1, keepdims=True))
    pw = ew / jnp.sum(ew, axis=-1, keepdims=True)
    o_w = jnp.dot(pw.astype(BF16), vw, preferred_element_type=F32)

    gates = jax.nn.sigmoid(gt_ref[...])
    for r in range(R):
        gsel = _gate_cols(gates, g, r * 3)
        out = (gsel[0] * oc_ref[:, r * DH:(r + 1) * DH].astype(F32)
               + gsel[1] * o_s[r * T:(r + 1) * T] + gsel[2] * o_w[r * T:(r + 1) * T])
        mix_ref[:, r * DH:(r + 1) * DH] = out.astype(mix_ref.dtype)


def _gate_cols(gates, g, c0):
    lane = lax.broadcasted_iota(jnp.int32, gates.shape, 1)
    tgt = g * (B_GROUP * 3) + c0
    return [jnp.sum(jnp.where(lane == tgt + k, gates, 0.0), axis=-1, keepdims=True) for k in range(3)]


def sel_win_combine(projb, kvp, oc, mem, n_batch, *, T=256, CK=512):
    M = projb.shape[0]
    L = M // n_batch
    nt = L // T
    GW = B_GROUP * HEAD_DIM
    kcol = lambda base: (lambda b, g, i: (b, base + g))
    wspec = lambda base, d: pl.BlockSpec((T, HEAD_DIM), lambda b, g, i: (b * nt + jnp.maximum(i - d, 0), base + g))
    expand = (jnp.arange(L, dtype=jnp.int32)[None, :] // SLC_BLOCK
              == jnp.arange(128, dtype=jnp.int32)[:, None]).astype(BF16)
    return pl.pallas_call(
        functools.partial(_sel_win_kernel, T=T, CK=CK),
        grid=(n_batch, B_KV_GROUPS, nt),
        in_specs=[pl.BlockSpec((T, GW), lambda b, g, i: (b * nt + i, g)),
                  pl.BlockSpec((L, HEAD_DIM), kcol(8)),
                  pl.BlockSpec((L, HEAD_DIM), kcol(12)),
                  wspec(16, 2), wspec(16, 1), wspec(16, 0),
                  wspec(20, 2), wspec(20, 1), wspec(20, 0),
                  pl.BlockSpec((T, GW), lambda b, g, i: (b * nt + i, g)),
                  pl.BlockSpec((None, None, T, 128), lambda b, g, i: (b, g, i, 0)),
                  pl.BlockSpec((T, 128), lambda b, g, i: (b * nt + i, 16)),
                  pl.BlockSpec((128, L), lambda b, g, i: (0, 0))],
        out_specs=pl.BlockSpec((T, GW), lambda b, g, i: (b * nt + i, g)),
        out_shape=jax.ShapeDtypeStruct((M, TOK_WIDTH), BF16),
        compiler_params=_cparams("parallel", "parallel", "arbitrary"),
        name="sel_win_combine",
    )(projb, kvp, kvp, kvp, kvp, kvp, kvp, kvp, kvp, oc, mem, projb, expand)


def _mem_attn_kernel(q_ref, k_ref, v_ref, o_ref):
    DH = HEAD_DIM
    for h in range(MEM_HEADS):
        qh = (q_ref[:, h * DH:(h + 1) * DH] * SCALE).astype(BF16)
        kh = k_ref[:, h * DH:(h + 1) * DH].astype(BF16)
        vh = v_ref[:, h * DH:(h + 1) * DH].astype(BF16)
        s = lax.dot_general(qh, kh, _NT, preferred_element_type=F32)
        e = jnp.exp(s - jnp.max(s, axis=-1, keepdims=True))
        p = e / jnp.sum(e, axis=-1, keepdims=True)
        o_ref[:, h * DH:(h + 1) * DH] = jnp.dot(p.astype(BF16), vh, preferred_element_type=F32).astype(o_ref.dtype)


def mem_attn_prompt(proj, q_block, memkv, n_batch, *, T=512):
    M = proj.shape[0]
    nt = M // n_batch // T
    return pl.pallas_call(
        _mem_attn_kernel,
        grid=(n_batch, nt),
        in_specs=[pl.BlockSpec((T, MEM_WIDTH), lambda b, i: (b * nt + i, q_block)),
                  pl.BlockSpec((MEM_TOKENS, MEM_WIDTH), lambda b, i: (b, 0)),
                  pl.BlockSpec((MEM_TOKENS, MEM_WIDTH), lambda b, i: (b, 1))],
        out_specs=pl.BlockSpec((T, MEM_WIDTH), lambda b, i: (b * nt + i, 0)),
        out_shape=jax.ShapeDtypeStruct((M, MEM_WIDTH), BF16),
        compiler_params=_cparams("parallel", "parallel"),
        name="mem_attn_prompt",
    )(proj, memkv, memkv)


def _log_sigmoid(x):
    return jnp.minimum(x, 0.0) - jnp.log1p(jnp.exp(-jnp.abs(x)))


def _mlstm_kernel(q_ref, k_ref, v_ref, o_ref, gt_ref, gb_ref, hg_ref, mix_ref, c_ref, n_ref, m_ref, *, nb):
    @pl.when(pl.program_id(0) == 0)
    def _():
        c_ref[...] = jnp.zeros_like(c_ref)
        n_ref[...] = jnp.zeros_like(n_ref)
        m_ref[...] = jnp.zeros_like(m_ref)

    for b in range(nb):
        _mlstm_chunk(q_ref.at[b], k_ref.at[b], v_ref.at[b], o_ref.at[b], gt_ref.at[b], gb_ref, hg_ref,
                     mix_ref.at[b], c_ref.at[b], n_ref.at[b], m_ref.at[b])


def _mlstm_chunk(q_ref, k_ref, v_ref, o_ref, gt_ref, gb_ref, hg_ref, mix_ref, c_ref, n_ref, m_ref):
    H, DH, CL = A_HEADS, HEAD_DIM, MLSTM_CHUNK
    gt = (gt_ref[...] + gb_ref[...]).T
    ig = gt[0:16]
    lf = _log_sigmoid(gt[16:32])
    s_r = lax.broadcasted_iota(jnp.int32, (CL, CL), 0)
    s_c = lax.broadcasted_iota(jnp.int32, (CL, CL), 1)
    upper = jnp.where(s_r <= s_c, 1.0, 0.0).astype(BF16)
    lf_hi = lf.astype(BF16)
    r1 = lf - lf_hi.astype(F32)
    lf_mid = r1.astype(BF16)
    lf_lo = (r1 - lf_mid.astype(F32)).astype(BF16)
    b_rows = (jnp.dot(lf_hi, upper, preferred_element_type=F32) + jnp.dot(lf_mid, upper, preferred_element_type=F32)
              + jnp.dot(lf_lo, upper, preferred_element_type=F32))
    u_rows = ig - b_rows
    bl_all = jnp.sum(lf, axis=-1, keepdims=True)
    tril = s_c <= s_r
    ones_b = jnp.ones((CL, DH), BF16)
    for h in range(H):
        b_row = b_rows[h:h + 1]
        u_row = u_rows[h:h + 1]
        bl = bl_all[h:h + 1]
        m_h = m_ref[h:h + 1, :]
        b_col = jnp.broadcast_to(b_row, (CL, CL)).T
        dlog = jnp.where(tril, b_col + u_row, -jnp.inf)
        inter = b_col[:, 0:1] + m_h[:, 0:1]
        mt = jnp.maximum(inter, jnp.max(dlog, axis=-1, keepdims=True))
        a = jnp.exp(inter - mt)
        qh = q_ref[:, h * DH:(h + 1) * DH]
        kh = k_ref[:, h * DH:(h + 1) * DH] * SCALE
        qb, kb = qh.astype(BF16), kh.astype(BF16)
        v1 = jnp.concatenate([v_ref[:, h * DH:(h + 1) * DH].astype(BF16), ones_b], axis=1)
        w = jnp.exp(dlog - mt) * lax.dot_general(qb, kb, _NT, preferred_element_type=F32)
        c_old = c_ref[h]
        n_old = n_ref[h]
        qcn = jnp.dot(qb, jnp.concatenate([c_old.astype(BF16), n_old.astype(BF16)], axis=1), preferred_element_type=F32)
        wv1 = jnp.dot(w.astype(BF16), v1, preferred_element_type=F32)
        num = a * qcn[:, :DH] + wv1[:, :DH]
        den = a * qcn[:, DH:DH + 1] + wv1[:, DH:DH + 1]
        hval = num / jnp.maximum(jnp.abs(den), jnp.exp(-mt))
        g_row = bl + u_row
        m_new = jnp.maximum(bl + m_h, jnp.max(g_row, axis=-1, keepdims=True))
        decay = jnp.exp(bl + m_h - m_new)
        wk_row = jnp.exp(g_row - m_new)
        k_t = kh.T
        kv1 = jnp.dot((k_t * wk_row).astype(BF16), v1, preferred_element_type=F32)
        c_ref[h] = decay[:, 0:1] * c_old + kv1[:, :DH]
        n_ref[h] = decay * n_old + kv1[:, DH:]
        m_ref[h:h + 1, :] = m_new
        hn = hval * lax.rsqrt(jnp.mean(hval * hval, axis=-1, keepdims=True) + EPS) * hg_ref[:, h * DH:(h + 1) * DH]
        mix_ref[:, h * DH:(h + 1) * DH] = (jax.nn.sigmoid(o_ref[:, h * DH:(h + 1) * DH]) * hn).astype(mix_ref.dtype)


def mlstm_prompt(proja, gate_bias, head_gain, n_batch):
    M, N = proja.shape
    L = M // n_batch
    nc = L // MLSTM_CHUNK
    p3 = proja.reshape(n_batch, L, N)
    blk = lambda j: pl.BlockSpec((n_batch, MLSTM_CHUNK, TOK_WIDTH), lambda c: (0, c, j))
    full = lambda *tail: pl.BlockSpec((n_batch,) + tail, lambda c: (0,) * (1 + len(tail)))
    mix, c_fin, n_fin, m_fin = pl.pallas_call(
        functools.partial(_mlstm_kernel, nb=n_batch),
        grid=(nc,),
        in_specs=[blk(0), blk(1), blk(2), blk(3),
                  pl.BlockSpec((n_batch, MLSTM_CHUNK, 128), lambda c: (0, c, (4 * TOK_WIDTH + MEM_WIDTH) // 128)),
                  pl.BlockSpec((1, 128), lambda c: (0, 0)),
                  pl.BlockSpec((1, TOK_WIDTH), lambda c: (0, 0))],
        out_specs=[pl.BlockSpec((n_batch, MLSTM_CHUNK, TOK_WIDTH), lambda c: (0, c, 0)),
                   full(A_HEADS, HEAD_DIM, HEAD_DIM), full(A_HEADS, HEAD_DIM, 128), full(16, 128)],
        out_shape=[jax.ShapeDtypeStruct((n_batch, L, TOK_WIDTH), BF16),
                   jax.ShapeDtypeStruct((n_batch, A_HEADS, HEAD_DIM, HEAD_DIM), F32),
                   jax.ShapeDtypeStruct((n_batch, A_HEADS, HEAD_DIM, 128), F32),
                   jax.ShapeDtypeStruct((n_batch, 16, 128), F32)],
        compiler_params=_cparams("arbitrary"),
        name="mlstm_prompt",
    )(p3, p3, p3, p3, p3, gate_bias, head_gain.reshape(1, TOK_WIDTH))
    return mix.reshape(M, TOK_WIDTH), c_fin, n_fin[..., 0], m_fin


def _lane_bcast_rows(row):
    return jnp.broadcast_to(row, (128, 128)).T


def _mlstm_sample_kernel(p_ref, gb_ref, m0_ref, hg_ref, c0_ref, n0_ref, mix_ref, c_ref, n_ref, m_ref, *, nb):
    for b in range(nb):
        _mlstm_sample_one(p_ref.at[b], gb_ref, m0_ref.at[b], hg_ref, c0_ref.at[b], n0_ref.at[b],
                          mix_ref.at[b], c_ref.at[b], n_ref.at[b], m_ref.at[b])


def _mlstm_sample_one(p_ref, gb_ref, m0_ref, hg_ref, c0_ref, n0_ref, mix_ref, c_ref, n_ref, m_ref):
    H, DH, W = A_HEADS, HEAD_DIM, TOK_WIDTH
    gcol = 4 * W + MEM_WIDTH
    gates = _lane_bcast_rows(p_ref[:, gcol:gcol + 128] + gb_ref[...])
    m_all = _lane_bcast_rows(m0_ref[...])
    m_ref[...] = jnp.zeros_like(m_ref)
    diag = lax.broadcasted_iota(jnp.int32, (DH, DH), 0) == lax.broadcasted_iota(jnp.int32, (DH, DH), 1)
    for h in range(H):
        ig = gates[h:h + 1]
        lf = _log_sigmoid(gates[16 + h:17 + h])
        m_old = m_all[h:h + 1]
        m_new = jnp.maximum(lf + m_old, ig)
        a = jnp.exp(lf + m_old - m_new)
        wgt = jnp.exp(ig - m_new)
        q = p_ref[:, h * DH:(h + 1) * DH]
        k = p_ref[:, W + h * DH:W + (h + 1) * DH] * SCALE
        v = p_ref[:, 2 * W + h * DH:2 * W + (h + 1) * DH]
        o = p_ref[:, 3 * W + h * DH:3 * W + (h + 1) * DH]
        c_old = c0_ref[h]
        n_old = n0_ref[h:h + 1, :]
        q_c = jnp.dot(jnp.broadcast_to(q, (8, DH)).astype(BF16), c_old.astype(BF16), preferred_element_type=F32)[0:1]
        k_diag = jnp.where(diag, jnp.broadcast_to(k, (DH, DH)), 0.0).astype(BF16)
        kv = jnp.dot(k_diag, jnp.broadcast_to(v, (DH, DH)).astype(BF16), preferred_element_type=F32)
        wqk = wgt * jnp.sum(q * k, axis=-1, keepdims=True)
        num = a * q_c + wqk * v
        den = a * jnp.sum(q * n_old, axis=-1, keepdims=True) + wqk
        hval = num / jnp.maximum(jnp.abs(den), jnp.exp(-m_new))
        c_ref[h] = a * c_old + wgt * kv
        n_ref[h:h + 1, :] = a * n_old + wgt * k
        m_ref[h:h + 1, :] = m_new
        hn = hval * lax.rsqrt(jnp.mean(hval * hval, axis=-1, keepdims=True) + EPS) * hg_ref[:, h * DH:(h + 1) * DH]
        mix_ref[:, h * DH:(h + 1) * DH] = (jax.nn.sigmoid(o) * hn).astype(mix_ref.dtype)


def mlstm_sample(proja, gate_bias, m0, head_gain, c0, n0, *, nb=4):
    Bs, N = proja.shape
    m0p = jnp.pad(m0, ((0, 0), (0, 128 - A_HEADS))).reshape(Bs, 1, 128)
    row = lambda n: pl.BlockSpec((nb, 1, n), lambda b: (b, 0, 0))
    const = lambda n: pl.BlockSpec((1, n), lambda b: (0, 0))
    cspec = pl.BlockSpec((nb, A_HEADS, HEAD_DIM, HEAD_DIM), lambda b: (b, 0, 0, 0))
    nspec = pl.BlockSpec((nb, A_HEADS, HEAD_DIM), lambda b: (b, 0, 0))
    return pl.pallas_call(
        functools.partial(_mlstm_sample_kernel, nb=nb),
        grid=(Bs // nb,),
        in_specs=[row(N), const(128), row(128), const(TOK_WIDTH), cspec, nspec],
        out_specs=[row(TOK_WIDTH), cspec, nspec, pl.BlockSpec((nb, 16, 128), lambda b: (b, 0, 0))],
        out_shape=[jax.ShapeDtypeStruct((Bs, 1, TOK_WIDTH), BF16),
                   jax.ShapeDtypeStruct(c0.shape, F32), jax.ShapeDtypeStruct(n0.shape, F32),
                   jax.ShapeDtypeStruct((Bs, 16, 128), F32)],
        compiler_params=_cparams("parallel"),
        name="mlstm_sample",
    )(proja.reshape(Bs, 1, N), gate_bias, m0p, head_gain.reshape(1, TOK_WIDTH), c0, n0)


def _mem_attn_sample_kernel(q_ref, k_ref, v_ref, o_ref, *, nb):
    H, DH = MEM_HEADS, HEAD_DIM
    nrow = k_ref.shape[1]
    row_h = lax.broadcasted_iota(jnp.int32, (8, nrow), 0)
    col_h = lax.broadcasted_iota(jnp.int32, (8, nrow), 1) % H
    bias = jnp.where(row_h == col_h, 0.0, NEG)
    for b in range(nb):
        q8 = jnp.concatenate([q_ref[b:b + 1, h * DH:(h + 1) * DH] for h in range(H)] + [jnp.zeros((8 - H, DH), F32)], axis=0)
        s = lax.dot_general((q8 * SCALE).astype(BF16), k_ref[b].astype(BF16), _NT, preferred_element_type=F32) + bias
        e = jnp.exp(s - jnp.max(s, axis=-1, keepdims=True))
        p = e / jnp.sum(e, axis=-1, keepdims=True)
        o = jnp.dot(p.astype(BF16), v_ref[b].astype(BF16), preferred_element_type=F32)
        o_ref[b * H:(b + 1) * H, :] = o[0:H]


def mem_attn_sample(proj, q_block, cache_k, cache_v, layer, *, nb=8):
    Bs, N = proj.shape
    nrow = MEM_TOKENS * MEM_HEADS
    k2 = cache_k.reshape(cache_k.shape[0], Bs, nrow, HEAD_DIM)
    v2 = cache_v.reshape(cache_v.shape[0], Bs, nrow, HEAD_DIM)
    cspec = pl.BlockSpec((None, nb, nrow, HEAD_DIM), lambda i: (layer, i, 0, 0))
    return pl.pallas_call(
        functools.partial(_mem_attn_sample_kernel, nb=nb),
        grid=(Bs // nb,),
        in_specs=[pl.BlockSpec((None, nb, MEM_WIDTH), lambda i: (i, 0, q_block)), cspec, cspec],
        out_specs=pl.BlockSpec((nb * MEM_HEADS, HEAD_DIM), lambda i: (i, 0)),
        out_shape=jax.ShapeDtypeStruct((Bs * MEM_HEADS, HEAD_DIM), F32),
        compiler_params=_cparams("parallel"),
        name="mem_attn_sample",
    )(proj.reshape(Bs // nb, nb, N), k2, v2).reshape(Bs, MEM_WIDTH)


def compress_sample(cache, page_table, pe, w1, w2):
    Bs, n_pages = page_table.shape
    G, DH = B_KV_GROUPS, HEAD_DIM
    prow = PAGE_SIZE * G
    nrow = n_pages * (PAGE_SIZE // CMP_STRIDE) * G
    pe_rows, w1cat, w2b = _compress_weights(pe, w1, w2)
    pe_flat = jnp.transpose(pe_rows, (1, 0, 2)).reshape(8, CMP_STRIDE * DH)
    pages = cache.reshape(cache.shape[0], prow, DH)
    page_specs = [pl.BlockSpec((None, prow, DH), lambda b, pt, p=p: (pt[b, p], 0, 0)) for p in range(n_pages)]

    def body(pt_ref, *refs):
        _compress_paged_kernel(*refs, n_pages=n_pages)

    return pl.pallas_call(
        body,
        grid_spec=pltpu.PrefetchScalarGridSpec(
            num_scalar_prefetch=1, grid=(Bs,),
            in_specs=page_specs + [pl.BlockSpec((8, CMP_STRIDE * DH), lambda b, pt: (0, 0)),
                                   pl.BlockSpec((CMP_STRIDE * DH, 2 * DH), lambda b, pt: (0, 0)),
                                   pl.BlockSpec((DH, DH), lambda b, pt: (0, 0))],
            out_specs=pl.BlockSpec((None, nrow, DH), lambda b, pt: (b, 0, 0))),
        out_shape=jax.ShapeDtypeStruct((Bs, nrow, DH), F32),
        compiler_params=_cparams("parallel"),
        name="compress_sample",
    )(page_table, *([pages] * n_pages), pe_flat, w1cat, w2b)


def _compress_paged_kernel(*refs, n_pages):
    page_refs = refs[:n_pages]
    pe_ref, w1_ref, w2_ref, o_ref = refs[n_pages:]
    G, DH = B_KV_GROUPS, HEAD_DIM
    cpp = PAGE_SIZE // CMP_STRIDE
    rows = n_pages * cpp * G
    low = lax.broadcasted_iota(jnp.int32, (cpp // 2, 8, DH), 1) < G
    pieces = [[] for _ in range(CMP_STRIDE)]
    for p in range(n_pages):
        x5 = page_refs[p][...].reshape(cpp // 2, 2, CMP_STRIDE // 2, 8, DH)
        for k in range(CMP_STRIDE // 2):
            a, b = x5[:, 0, k], x5[:, 1, k]
            even = jnp.where(low, a, pltpu.roll(b, G, 1))
            odd = jnp.where(low, pltpu.roll(a, G, 1), b)
            pieces[2 * k].append(even.reshape(cpp * G, DH).astype(BF16))
            pieces[2 * k + 1].append(odd.reshape(cpp * G, DH).astype(BF16))
    lhs = jnp.concatenate([jnp.concatenate(pc, axis=0) for pc in pieces], axis=1)
    lhs = jnp.concatenate([lhs, pe_ref[...].astype(BF16)], axis=0)
    acc = jnp.dot(lhs, w1_ref[...], preferred_element_type=F32)
    first = acc[:rows, :DH] + acc[rows:rows + 1, :DH]
    second = acc[:rows, DH:] + acc[rows + 1:rows + 2, DH:]
    hid = jax.nn.gelu(first + pltpu.roll(second, rows - G, 0))
    o_ref[...] = jnp.dot(hid.astype(BF16), w2_ref[...], preferred_element_type=F32)


def _nsa_sample_kernel(pt_ref, *refs, n_pages):
    p_ref, kv_ref, kc_ref, vc_ref, ex_ref = refs[:5]
    sk_refs = refs[5:5 + n_pages]
    sv_refs = refs[5 + n_pages:5 + 2 * n_pages]
    wk_ref, wv_ref, mix_ref = refs[5 + 2 * n_pages:]
    G, R, DH, W = B_KV_GROUPS, B_GROUP, HEAD_DIM, TOK_WIDTH
    pos = PAST_LEN
    crow = kc_ref.shape[0]
    ncol = n_pages * sk_refs[0].shape[0]
    nsl = -(-(pos + 1) // SLC_BLOCK)
    wrow = wk_ref.shape[0]
    w_buf = wrow // G

    j_l = lax.broadcasted_iota(jnp.int32, (8, 128), 1)
    cur = pos // SLC_BLOCK
    forced = (j_l == 0) | (j_l == cur) | (j_l == cur - 1)
    jr = lax.broadcasted_iota(jnp.int32, (128, 128), 0)
    jc = lax.broadcasted_iota(jnp.int32, (128, 128), 1)
    wcol = lax.broadcasted_iota(jnp.int32, (8, wrow), 1)
    wvalid = (pos - w_buf + wcol // G) > pos - WINDOW
    gates = jax.nn.sigmoid(p_ref[:, W + MEM_WIDTH:W + MEM_WIDTH + 128])

    q32r = jnp.concatenate([p_ref[:, (g * R + r) * DH:(g * R + r + 1) * DH] if r < R else jnp.zeros((1, DH), F32)
                            for g in range(G) for r in range(8)], axis=0)
    q32 = q32r * SCALE
    q32b = q32.astype(BF16)

    c_row = lax.broadcasted_iota(jnp.int32, (8 * G, crow), 0) // 8
    c_col = lax.broadcasted_iota(jnp.int32, (8 * G, crow), 1)
    cvalid = (c_col % G == c_row) & ((c_col // G) * CMP_STRIDE + CMP_LEN - 1 <= pos)
    s = lax.dot_general(q32r.astype(BF16), kc_ref[...].astype(BF16), _NT, preferred_element_type=F32) * SCALE
    sm = jnp.where(cvalid, s, NEG)
    e = jnp.exp(sm - jnp.max(sm, axis=-1, keepdims=True))
    p = jnp.where(cvalid, e / jnp.sum(e, axis=-1, keepdims=True), 0.0)
    o_c = jnp.dot(p.astype(BF16), vc_ref[...].astype(BF16), preferred_element_type=F32)
    psum = jnp.concatenate([p[8 * g:8 * g + 1] + p[8 * g + 1:8 * g + 2] + p[8 * g + 2:8 * g + 3] for g in range(G)]
                           + [jnp.zeros((8 - G, crow), F32)], axis=0)
    n_o = (lax.broadcasted_iota(jnp.int32, (crow, 128), 0) // G) * CMP_STRIDE
    j_o = lax.broadcasted_iota(jnp.int32, (crow, 128), 1)
    ovl = jnp.where((n_o < (j_o + 1) * SLC_BLOCK) & (n_o + CMP_LEN > j_o * SLC_BLOCK), 1.0, 0.0).astype(BF16)
    p_hi, p_lo = _split2(psum)
    imp = jnp.dot(p_hi, ovl, preferred_element_type=F32) + jnp.dot(p_lo, ovl, preferred_element_type=F32)
    score_all = jnp.where(forced, FORCED_SCORE, jnp.where(j_l * SLC_BLOCK <= pos, imp, -1.0))
    score_all = jnp.where(j_l < nsl, score_all, -2.0)

    sb_rows, wb_rows = [], []
    for g in range(G):
        sc_c = jnp.broadcast_to(score_all[g:g + 1], (128, 128))
        sc_r = sc_c.T
        beats = (sc_r > sc_c) | ((sc_r == sc_c) & (jr < jc))
        rank = jnp.sum(jnp.where(beats, 1.0, 0.0), axis=0, keepdims=True)
        member = jnp.where((rank < min(N_SELECT, nsl)) & (j_l[0:1] < nsl), 1.0, 0.0)
        sb_rows.append(jnp.broadcast_to(member, (8, 128)))
        wb_rows.append(jnp.where(wvalid & (wcol % G == g), 0.0, NEG))

    mexp = jnp.dot(jnp.concatenate(sb_rows, axis=0).astype(BF16), ex_ref[...], preferred_element_type=F32)
    row_g = lax.broadcasted_iota(jnp.int32, (8 * G, ncol), 0) // 8
    col_g = lax.broadcasted_iota(jnp.int32, (8 * G, ncol), 1) % G
    sbias = jnp.where((mexp > 0.5) & (col_g == row_g), 0.0, NEG)

    def new_rows(base):
        return jnp.concatenate([jnp.broadcast_to(kv_ref[:, base + g * DH:base + (g + 1) * DH], (8, DH)) for g in range(G)], axis=0)

    def branch(k_rows, v_rows, bias, k_base, v_base):
        s = lax.dot_general(q32b, k_rows, _NT, preferred_element_type=F32) + bias
        s_new = jnp.sum(q32 * new_rows(k_base), axis=-1, keepdims=True)
        mx = jnp.maximum(jnp.max(s, axis=-1, keepdims=True), s_new)
        p = jnp.exp(s - mx)
        p_new = jnp.exp(s_new - mx)
        num = jnp.dot(p.astype(BF16), v_rows, preferred_element_type=F32) + p_new * new_rows(v_base)
        return num / (jnp.sum(p, axis=-1, keepdims=True) + p_new)

    ks = jnp.concatenate([r_[...].astype(BF16) for r_ in sk_refs], axis=0)
    vs = jnp.concatenate([r_[...].astype(BF16) for r_ in sv_refs], axis=0)
    o_s = branch(ks, vs, sbias, 2 * G * DH, 3 * G * DH)
    o_w = branch(wk_ref[...].astype(BF16), wv_ref[...].astype(BF16), jnp.concatenate(wb_rows, axis=0), 4 * G * DH, 5 * G * DH)
    for g in range(G):
        for r in range(R):
            h, row = g * R + r, g * 8 + r
            gsel = _gate_cols(gates, g, r * 3)
            out = gsel[0] * o_c[row:row + 1] + gsel[1] * o_s[row:row + 1] + gsel[2] * o_w[row:row + 1]
            mix_ref[:, h * DH:(h + 1) * DH] = out.astype(mix_ref.dtype)


def nsa_sample(projb, kvp, kc, vc, cache_slc_k, cache_slc_v, cache_win_k, cache_win_v, page_table):
    Bs, n_pages = page_table.shape
    G = B_KV_GROUPS
    prow = PAGE_SIZE * G
    wrow = cache_win_k.shape[1] * G
    sk = cache_slc_k.reshape(cache_slc_k.shape[0], prow, HEAD_DIM)
    sv = cache_slc_v.reshape(cache_slc_v.shape[0], prow, HEAD_DIM)
    wk = cache_win_k.reshape(Bs, wrow, HEAD_DIM)
    wv = cache_win_v.reshape(Bs, wrow, HEAD_DIM)
    ncol = n_pages * prow
    expand = (jnp.arange(ncol, dtype=jnp.int32)[None, :] // (G * SLC_BLOCK)
              == jnp.arange(128, dtype=jnp.int32)[:, None]).astype(BF16)
    row = lambda n: pl.BlockSpec((None, 1, n), lambda b, pt: (b, 0, 0))
    cspec = pl.BlockSpec((None, kc.shape[1], HEAD_DIM), lambda b, pt: (b, 0, 0))
    page_specs = [pl.BlockSpec((None, prow, HEAD_DIM), lambda b, pt, p=p: (pt[b, p], 0, 0)) for p in range(n_pages)]
    wspec = pl.BlockSpec((None, wrow, HEAD_DIM), lambda b, pt: (b, 0, 0))
    return pl.pallas_call(
        functools.partial(_nsa_sample_kernel, n_pages=n_pages),
        grid_spec=pltpu.PrefetchScalarGridSpec(
            num_scalar_prefetch=1, grid=(Bs,),
            in_specs=[row(projb.shape[1]), row(kvp.shape[1]), cspec, cspec,
                      pl.BlockSpec((128, ncol), lambda b, pt: (0, 0))] + page_specs + page_specs + [wspec, wspec],
            out_specs=row(TOK_WIDTH)),
        out_shape=jax.ShapeDtypeStruct((Bs, 1, TOK_WIDTH), BF16),
        compiler_params=_cparams("parallel"),
        name="nsa_sample",
    )(page_table, projb.reshape(Bs, 1, -1), kvp.reshape(Bs, 1, -1), kc, vc, expand, *([sk] * n_pages),
      *([sv] * n_pages), wk, wv).reshape(Bs, TOK_WIDTH)


def kernel(x_prompt, x_sample, mem_prompt, cache_mem_k, cache_mem_v, state_mlstm_C, state_mlstm_n, state_mlstm_m, cache_cmp_k, cache_cmp_v, cache_slc_k, cache_slc_v, cache_win_k, cache_win_v, page_table, ffn1_norm, ffn1_w_in, ffn1_w_out, ffn2_norm, ffn2_w_in, ffn2_w_out, mix_norm, a_w_in, a_b_i, a_b_f, a_head_norm, b_w_in, mem_norm, w_mem_kv, w_out, kv_norm, w_kv, cmp_pe_k, cmp_w1_k, cmp_w2_k, cmp_pe_v, cmp_w1_v, cmp_w2_v, final_norm):
    G, R, DH = B_KV_GROUPS, B_GROUP, HEAD_DIM
    ffn1_wi, ffn1_wo = ffn1_w_in.astype(BF16), ffn1_w_out.astype(BF16)
    ffn2_wi, ffn2_wo = ffn2_w_in.astype(BF16), ffn2_w_out.astype(BF16)
    w_o = w_out.astype(BF16)
    w_kv_b = w_kv.astype(BF16)
    w_mkv = w_mem_kv.astype(BF16)

    wa = a_w_in[0]
    zc = lambda n: jnp.zeros((D_MODEL, n), F32)
    a_wr = jnp.concatenate([wa[:, :4 * TOK_WIDTH], wa[:, 4 * TOK_WIDTH + 2 * A_HEADS:],
                            wa[:, 4 * TOK_WIDTH:4 * TOK_WIDTH + A_HEADS], zc(16 - A_HEADS),
                            wa[:, 4 * TOK_WIDTH + A_HEADS:4 * TOK_WIDTH + 2 * A_HEADS], zc(128 - 16 - A_HEADS),
                            zc(128)], axis=1).astype(BF16)
    gate_bias = jnp.concatenate([a_b_i[0], jnp.zeros((16 - A_HEADS,), F32), a_b_f[0],
                                 jnp.zeros((128 - 16 - A_HEADS,), F32)]).reshape(1, 128)
    wb = b_w_in[0]
    b_wr = jnp.concatenate([wb[:, :TOK_WIDTH], wb[:, TOK_WIDTH + 3 * B_HEADS:],
                            wb[:, TOK_WIDTH:TOK_WIDTH + 3 * B_HEADS], zc(256 - 3 * B_HEADS)], axis=1).astype(BF16)

    Bp, S = x_prompt.shape[:2]
    Mp = Bp * S
    memx = mem_prompt.reshape(Bp * MEM_TOKENS, D_MODEL)
    memkv = [norm_matmul(memx, mem_norm[l], w_mkv[l], tm=512, tn=512) for l in range(DEPTH)]
    mem5 = jnp.stack(memkv).reshape(DEPTH, Bp, MEM_TOKENS, 2, MEM_HEADS, DH)
    mem_k_p, mem_v_p = mem5[:, :, :, 0], mem5[:, :, :, 1]

    x = x_prompt.reshape(Mp, D_MODEL)
    x = ffn(x, ffn1_norm[0], ffn1_wi, ffn1_wo, 0, tm=512, tf=512)
    proja = norm_matmul(x, mix_norm[0], a_wr, tm=1024, tn=768)
    mix, C_fin, n_fin, m_fin = mlstm_prompt(proja, gate_bias, a_head_norm[0], Bp)
    mo = mem_attn_prompt(proja, 4 * TOK_WIDTH // MEM_WIDTH, memkv[0], Bp)
    x = outproj(x, mix, mo, w_o, 0, tm=512)
    x = ffn(x, ffn2_norm[0], ffn2_wi, ffn2_wo, 0, tm=512, tf=512)
    kvp, kv4 = kv_proj(x, kv_norm, w_kv_b, tm=1024)
    x = ffn(x, ffn1_norm[1], ffn1_wi, ffn1_wo, 1, tm=512, tf=512)
    projb = norm_matmul(x, mix_norm[1], b_wr, tm=1024, tn=768)
    kc = compress_prompt(kvp, 0, Bp, cmp_pe_k, cmp_w1_k, cmp_w2_k)
    vc = compress_prompt(kvp, 1, Bp, cmp_pe_v, cmp_w1_v, cmp_w2_v)
    oc, member = cmp_select_prompt(projb, kc, vc, Bp)
    mix = sel_win_combine(projb, kvp, oc, member, Bp)
    mo = mem_attn_prompt(projb, TOK_WIDTH // MEM_WIDTH, memkv[1], Bp)
    x = outproj(x, mix, mo, w_o, 1, tm=512)
    y_prompt = ffn(x, ffn2_norm[1], ffn2_wi, ffn2_wo, 1, final_norm, tm=512, tf=512).reshape(Bp, S, D_MODEL)
    C_p = C_fin[None]
    n_p = n_fin[None]
    m_p = m_fin[None, :, :A_HEADS, 0]
    kv_p = tuple(kv4[i].reshape(Bp, S, G, DH) for i in range(6))

    Bs = x_sample.shape[0]
    xs = x_sample.reshape(Bs, D_MODEL)
    xs = ffn(xs, ffn1_norm[0], ffn1_wi, ffn1_wo, 0, tm=Bs, tf=512)
    proja_s = norm_matmul(xs, mix_norm[0], a_wr, tm=Bs, tn=768)
    mix_s, C_new, n_new, m_new = mlstm_sample(proja_s, gate_bias, state_mlstm_m[0], a_head_norm[0],
                                              state_mlstm_C[0], state_mlstm_n[0])
    mo_s = mem_attn_sample(proja_s, 4 * TOK_WIDTH // MEM_WIDTH, cache_mem_k, cache_mem_v, 0)
    xs = outproj(xs, mix_s.reshape(Bs, TOK_WIDTH), mo_s, w_o, 0, tm=Bs)
    xs = ffn(xs, ffn2_norm[0], ffn2_wi, ffn2_wo, 0, tm=Bs, tf=512)
    kvp_s = norm_matmul(xs, kv_norm, w_kv_b, tm=Bs, tn=512)
    xs = ffn(xs, ffn1_norm[1], ffn1_wi, ffn1_wo, 1, tm=Bs, tf=512)
    projb_s = norm_matmul(xs, mix_norm[1], b_wr, tm=Bs, tn=768)
    kc_s = compress_sample(cache_cmp_k, page_table, cmp_pe_k, cmp_w1_k, cmp_w2_k)
    vc_s = compress_sample(cache_cmp_v, page_table, cmp_pe_v, cmp_w1_v, cmp_w2_v)
    mix_s = nsa_sample(projb_s, kvp_s, kc_s, vc_s, cache_slc_k, cache_slc_v, cache_win_k, cache_win_v, page_table)
    mo_s = mem_attn_sample(projb_s, TOK_WIDTH // MEM_WIDTH, cache_mem_k, cache_mem_v, 1)
    xs = outproj(xs, mix_s, mo_s, w_o, 1, tm=Bs)
    y_sample = ffn(xs, ffn2_norm[1], ffn2_wi, ffn2_wo, 1, final_norm, tm=Bs, tf=512).reshape(Bs, 1, D_MODEL)
    C_s = C_new[None]
    n_s = n_new[None]
    m_s = m_new[None, :, :A_HEADS, 0]
    kvs5 = kvp_s.reshape(Bs, 1, 6, G, DH)
    kv_s = tuple(kvs5[:, :, i] for i in range(6))

    w_p = min(WINDOW, S)
    cmp_k_p, cmp_v_p, slc_k_p, slc_v_p, win_k_all, win_v_all = kv_p
    win_k_p = win_k_all[:, S - w_p:]
    win_v_p = win_v_all[:, S - w_p:]
    cmp_k_s, cmp_v_s, slc_k_s, slc_v_s, win_k_s, win_v_s = kv_s
    return (y_prompt, y_sample, mem_k_p, mem_v_p, C_p, n_p, m_p, C_s, n_s, m_s,
            cmp_k_p, cmp_v_p, slc_k_p, slc_v_p, win_k_p, win_v_p,
            cmp_k_s, cmp_v_s, slc_k_s, slc_v_s, win_k_s, win_v_s)
```

```python
import functools

import jax
import jax.numpy as jnp
from jax import lax
from jax.experimental import pallas as pl
from jax.experimental.pallas import tpu as pltpu

D_MODEL = 2048
DEPTH = 2
PAST_LEN = 2048
PAGE_SIZE = 128
HEAD_DIM = 128
N_A_LAYERS = DEPTH // 2
MEM_TOKENS = 256
MEM_HEADS = 4
MEM_WIDTH = MEM_HEADS * HEAD_DIM
TOK_WIDTH = D_MODEL - MEM_WIDTH
A_HEADS = TOK_WIDTH // HEAD_DIM
B_HEADS = TOK_WIDTH // HEAD_DIM
B_KV_GROUPS = 4
B_GROUP = B_HEADS // B_KV_GROUPS
D_FF = ((8 * D_MODEL // 3 + 255) // 256) * 256
MLSTM_CHUNK = 128
CMP_STRIDE = 16
CMP_LEN = 2 * CMP_STRIDE
SLC_BLOCK = 64
N_SELECT = 16
WINDOW = 512
WIN_QBLK = 128
SEL_QBLK = 64
FORCED_SCORE = 1e4
NEG = -1e30
EPS = 1e-6
SCALE = HEAD_DIM ** -0.5

F32 = jnp.float32
BF16 = jnp.bfloat16
VMEM_LIMIT = 56 * 1024 * 1024


def _cparams(*sem):
    return pltpu.CompilerParams(dimension_semantics=sem, vmem_limit_bytes=VMEM_LIMIT)


def _rms_bf16(x, g):
    ms = jnp.mean(x * x, axis=-1, keepdims=True)
    return (x * lax.rsqrt(ms + EPS) * g).astype(BF16)


def _norm_matmul_kernel(x_ref, g_ref, w_ref, o_ref, h_ref):
    @pl.when(pl.program_id(1) == 0)
    def _():
        h_ref[...] = _rms_bf16(x_ref[...], g_ref[...])

    o_ref[...] = jnp.dot(h_ref[...], w_ref[...], preferred_element_type=F32).astype(o_ref.dtype)


def norm_matmul(x, g, w, *, tm, tn, out_dtype=F32):
    M, D = x.shape
    N = w.shape[1]
    return pl.pallas_call(
        _norm_matmul_kernel,
        grid=(M // tm, N // tn),
        in_specs=[pl.BlockSpec((tm, D), lambda i, j: (i, 0)),
                  pl.BlockSpec((1, D), lambda i, j: (0, 0)),
                  pl.BlockSpec((D, tn), lambda i, j: (0, j))],
        out_specs=pl.BlockSpec((tm, tn), lambda i, j: (i, j)),
        out_shape=jax.ShapeDtypeStruct((M, N), out_dtype),
        scratch_shapes=[pltpu.VMEM((tm, D), BF16)],
        compiler_params=_cparams("parallel", "arbitrary"),
        name="norm_matmul",
    )(x, g.reshape(1, D), w)


def _kv_proj_kernel(x_ref, g_ref, w_ref, o_ref, o4_ref, ob_ref, h_ref):
    @pl.when(pl.program_id(1) == 0)
    def _():
        h_ref[...] = _rms_bf16(x_ref[...], g_ref[...])

    y = jnp.dot(h_ref[...], w_ref[...], preferred_element_type=F32)
    o_ref[...] = y
    ob_ref[...] = y.astype(BF16)
    tm = y.shape[0]
    for g in range(B_KV_GROUPS):
        o4_ref[pl.ds(g, tm, stride=B_KV_GROUPS), :] = y[:, g * HEAD_DIM:(g + 1) * HEAD_DIM]


def kv_proj(x, g, w, *, tm):
    M, D = x.shape
    GD = B_KV_GROUPS * HEAD_DIM
    nbr = w.shape[1] // GD
    return pl.pallas_call(
        _kv_proj_kernel,
        grid=(M // tm, nbr),
        in_specs=[pl.BlockSpec((tm, D), lambda i, j: (i, 0)),
                  pl.BlockSpec((1, D), lambda i, j: (0, 0)),
                  pl.BlockSpec((D, GD), lambda i, j: (0, j))],
        out_specs=[pl.BlockSpec((tm, GD), lambda i, j: (i, j)),
                   pl.BlockSpec((None, tm * B_KV_GROUPS, HEAD_DIM), lambda i, j: (j, i, 0)),
                   pl.BlockSpec((tm, GD), lambda i, j: (i, j))],
        out_shape=[jax.ShapeDtypeStruct((M, nbr * GD), F32),
                   jax.ShapeDtypeStruct((nbr, M * B_KV_GROUPS, HEAD_DIM), F32),
                   jax.ShapeDtypeStruct((M, nbr * GD), BF16)],
        scratch_shapes=[pltpu.VMEM((tm, D), BF16)],
        compiler_params=_cparams("parallel", "arbitrary"),
        name="kv_proj",
    )(x, g.reshape(1, D), w)


def _ffn_kernel(x_ref, g_ref, wg_ref, wu_ref, wo_ref, fg_ref, o_ref, h_ref, acc_ref, *, final_norm):
    f = pl.program_id(1)

    @pl.when(f == 0)
    def _():
        h_ref[...] = _rms_bf16(x_ref[...], g_ref[...])
        acc_ref[...] = jnp.zeros_like(acc_ref)

    h = h_ref[...]
    gate = jnp.dot(h, wg_ref[...], preferred_element_type=F32)
    up = jnp.dot(h, wu_ref[...], preferred_element_type=F32)
    act = (gate * jax.nn.sigmoid(gate) * up).astype(BF16)
    acc_ref[...] += jnp.dot(act, wo_ref[...], preferred_element_type=F32)

    @pl.when(f == pl.num_programs(1) - 1)
    def _():
        y = x_ref[...] + 0.5 * acc_ref[...]
        if final_norm:
            ms = jnp.mean(y * y, axis=-1, keepdims=True)
            y = y * lax.rsqrt(ms + EPS) * fg_ref[...]
        o_ref[...] = y


def ffn(x, g, w_in, w_out, layer, final_gain=None, *, tm, tf):
    M, D = x.shape
    F = w_out.shape[1]
    nf = F // tf
    fg = jnp.ones((D,), F32) if final_gain is None else final_gain
    return pl.pallas_call(
        functools.partial(_ffn_kernel, final_norm=final_gain is not None),
        grid=(M // tm, nf),
        in_specs=[pl.BlockSpec((tm, D), lambda i, f: (i, 0)),
                  pl.BlockSpec((1, D), lambda i, f: (0, 0)),
                  pl.BlockSpec((None, D, tf), lambda i, f: (layer, 0, f)),
                  pl.BlockSpec((None, D, tf), lambda i, f: (layer, 0, f + nf)),
                  pl.BlockSpec((None, tf, D), lambda i, f: (layer, f, 0)),
                  pl.BlockSpec((1, D), lambda i, f: (0, 0))],
        out_specs=pl.BlockSpec((tm, D), lambda i, f: (i, 0)),
        out_shape=jax.ShapeDtypeStruct((M, D), F32),
        scratch_shapes=[pltpu.VMEM((tm, D), BF16), pltpu.VMEM((tm, D), F32)],
        compiler_params=_cparams("parallel", "arbitrary"),
        name="ffn",
    )(x, g.reshape(1, D), w_in, w_in, w_out, fg.reshape(1, D))


def _outproj_kernel(x_ref, mix_ref, mo_ref, wa_ref, wb_ref, o_ref):
    y = jnp.dot(mix_ref[...].astype(BF16), wa_ref[...], preferred_element_type=F32)
    y += jnp.dot(mo_ref[...].astype(BF16), wb_ref[...], preferred_element_type=F32)
    o_ref[...] = x_ref[...] + y


def outproj(x, mix, mo, w, layer, *, tm):
    M, D = x.shape
    Ka, Kb = mix.shape[1], mo.shape[1]
    assert Ka % Kb == 0
    return pl.pallas_call(
        _outproj_kernel,
        grid=(M // tm,),
        in_specs=[pl.BlockSpec((tm, D), lambda i: (i, 0)),
                  pl.BlockSpec((tm, Ka), lambda i: (i, 0)),
                  pl.BlockSpec((tm, Kb), lambda i: (i, 0)),
                  pl.BlockSpec((None, Ka, D), lambda i: (layer, 0, 0)),
                  pl.BlockSpec((None, Kb, D), lambda i: (layer, Ka // Kb, 0))],
        out_specs=pl.BlockSpec((tm, D), lambda i: (i, 0)),
        out_shape=jax.ShapeDtypeStruct((M, D), F32),
        compiler_params=_cparams("parallel"),
        name="outproj",
    )(x, mix, mo, w, w)


def _compress_kernel(x_ref, pe_ref, w1_ref, w2_ref, o_ref, *, nch):
    G, DH = B_KV_GROUPS, HEAD_DIM
    lhs = jnp.concatenate(
        [jnp.concatenate([x_ref[:, c, g * DH:(g + 1) * DH] for g in range(G)] + [pe_ref[c]], axis=0).astype(BF16)
         for c in range(CMP_STRIDE)], axis=1)
    acc = jnp.dot(lhs, w1_ref[...], preferred_element_type=F32)
    bias_first = acc[G * nch:G * nch + 1, :DH]
    bias_second = acc[G * nch + 1:G * nch + 2, DH:]
    for g in range(G):
        first = acc[g * nch:(g + 1) * nch, :DH] + bias_first
        second = acc[g * nch:(g + 1) * nch, DH:] + bias_second
        hid = jax.nn.gelu(first + pltpu.roll(second, nch - 1, 0))
        o_ref[g] = jnp.dot(hid.astype(BF16), w2_ref[...], preferred_element_type=F32)


def _compress_weights(pe, w1, w2):
    DH = HEAD_DIM
    pe2 = pe.reshape(2, CMP_STRIDE, DH)
    pe_rows = jnp.concatenate([pe2[0][:, None], pe2[1][:, None], jnp.zeros((CMP_STRIDE, 6, DH), F32)], axis=1)
    w12 = w1.reshape(2, CMP_STRIDE, DH, -1)
    w1cat = jnp.concatenate([w12[0], w12[1]], axis=-1).astype(BF16)
    return pe_rows, w1cat.reshape(CMP_STRIDE * DH, -1), w2.astype(BF16)


def compress_prompt(kvp, col_block, n_batch, pe, w1, w2):
    M = kvp.shape[0]
    nch = M // n_batch // CMP_STRIDE
    pe_rows, w1cat, w2b = _compress_weights(pe, w1, w2)
    x = kvp.reshape(M // CMP_STRIDE, CMP_STRIDE, kvp.shape[1])
    return pl.pallas_call(
        functools.partial(_compress_kernel, nch=nch),
        grid=(n_batch,),
        in_specs=[pl.BlockSpec((nch, CMP_STRIDE, 512), lambda b: (b, 0, col_block)),
                  pl.BlockSpec((CMP_STRIDE, 8, HEAD_DIM), lambda b: (0, 0, 0)),
                  pl.BlockSpec((CMP_STRIDE * HEAD_DIM, 2 * HEAD_DIM), lambda b: (0, 0)),
                  pl.BlockSpec((HEAD_DIM, HEAD_DIM), lambda b: (0, 0))],
        out_specs=pl.BlockSpec((None, B_KV_GROUPS, nch, HEAD_DIM), lambda b: (b, 0, 0, 0)),
        out_shape=jax.ShapeDtypeStruct((n_batch, B_KV_GROUPS, nch, HEAD_DIM), F32),
        compiler_params=_cparams("parallel"),
        name="compress_prompt",
    )(x, pe_rows, w1cat, w2b)


_NT = (((1,), (1,)), ((), ()))


def _split2(x):
    hi = x.astype(BF16)
    return hi, (x - hi.astype(F32)).astype(BF16)


def _cmp_select_kernel(q_ref, kc_ref, vc_ref, oc_ref, mem_ref, sc_ref, *, T, nch, nsl, pos0):
    G, R, DH = B_KV_GROUPS, B_GROUP, HEAD_DIM
    base = pos0 + pl.program_id(1) * T
    pos_r = base + lax.broadcasted_iota(jnp.int32, (T, nch), 0)
    n_c = lax.broadcasted_iota(jnp.int32, (T, nch), 1)
    cmask = (n_c * CMP_STRIDE + CMP_LEN - 1) <= pos_r
    j_o = lax.broadcasted_iota(jnp.int32, (nsl, nch), 0)
    st_o = lax.broadcasted_iota(jnp.int32, (nsl, nch), 1) * CMP_STRIDE
    ovl_t = jnp.where((st_o < (j_o + 1) * SLC_BLOCK) & (st_o + CMP_LEN > j_o * SLC_BLOCK), 1.0, 0.0).astype(BF16)
    j_s = lax.broadcasted_iota(jnp.int32, (nsl, T), 0)
    pos_s = base + lax.broadcasted_iota(jnp.int32, (nsl, T), 1)
    cur = pos_s // SLC_BLOCK
    forced = (j_s == 0) | (j_s == cur) | (j_s == cur - 1)
    valid = j_s * SLC_BLOCK <= pos_s
    for g in range(G):
        kcg = kc_ref[g].astype(BF16)
        vcg = vc_ref[g].astype(BF16)
        psum = jnp.zeros((T, nch), F32)
        for r in range(R):
            h = g * R + r
            qh = q_ref[:, h * DH:(h + 1) * DH].astype(BF16)
            s = lax.dot_general(qh, kcg, _NT, preferred_element_type=F32) * SCALE
            sm = jnp.where(cmask, s, NEG)
            e = jnp.exp(sm - jnp.max(sm, axis=-1, keepdims=True))
            p = jnp.where(cmask, e / jnp.sum(e, axis=-1, keepdims=True), 0.0)
            oc_ref[:, h * DH:(h + 1) * DH] = jnp.dot(p.astype(BF16), vcg, preferred_element_type=F32).astype(oc_ref.dtype)
            psum = psum + p
        p_hi, p_lo = _split2(psum)
        imp_t = (lax.dot_general(ovl_t, p_hi, _NT, preferred_element_type=F32)
                 + lax.dot_general(ovl_t, p_lo, _NT, preferred_element_type=F32))
        score = jnp.where(forced, FORCED_SCORE, jnp.where(valid, imp_t, -1.0))
        sc_ref[...] = score
        rank = jnp.zeros((nsl, T), F32)
        for jp in range(nsl):
            row = sc_ref[jp:jp + 1, :]
            beats = (row > score) | ((row == score) & (j_s > jp))
            rank = rank + jnp.where(beats, 1.0, 0.0)
        mem_t = jnp.where(rank < min(N_SELECT, nsl), 1.0, 0.0)
        if nsl < 128:
            mem_t = jnp.concatenate([mem_t, jnp.zeros((128 - nsl, T), F32)], axis=0)
        mem_ref[g] = mem_t.T.astype(mem_ref.dtype)


def cmp_select_prompt(projb, kc, vc, n_batch, *, T=256):
    M = projb.shape[0]
    L = M // n_batch
    nt, nch, nsl = L // T, L // CMP_STRIDE, L // SLC_BLOCK
    return pl.pallas_call(
        functools.partial(_cmp_select_kernel, T=T, nch=nch, nsl=nsl, pos0=0),
        grid=(n_batch, nt),
        in_specs=[pl.BlockSpec((T, TOK_WIDTH), lambda b, i: (b * nt + i, 0)),
                  pl.BlockSpec((None, B_KV_GROUPS, nch, HEAD_DIM), lambda b, i: (b, 0, 0, 0)),
                  pl.BlockSpec((None, B_KV_GROUPS, nch, HEAD_DIM), lambda b, i: (b, 0, 0, 0))],
        out_specs=[pl.BlockSpec((T, TOK_WIDTH), lambda b, i: (b * nt + i, 0)),
                   pl.BlockSpec((None, B_KV_GROUPS, T, 128), lambda b, i: (b, 0, i, 0))],
        out_shape=[jax.ShapeDtypeStruct((M, TOK_WIDTH), BF16),
                   jax.ShapeDtypeStruct((n_batch, B_KV_GROUPS, L, 128), BF16)],
        scratch_shapes=[pltpu.VMEM((nsl, T), F32)],
        compiler_params=_cparams("parallel", "parallel"),
        name="cmp_select_prompt",
    )(projb, kc, vc)


def _sel_win_kernel(q_ref, ks_ref, vs_ref, kw0_ref, kw1_ref, kw2_ref, vw0_ref, vw1_ref, vw2_ref, oc_ref, mem_ref,
                    gt_ref, ex_ref, mix_ref, *, T, CK):
    R, DH = B_GROUP, HEAD_DIM
    g = pl.program_id(1)
    i = pl.program_id(2)
    q3 = jnp.concatenate([q_ref[:, r * DH:(r + 1) * DH] for r in range(R)], axis=0)
    q3 = (q3 * SCALE).astype(BF16)
    qpos = i * T + lax.broadcasted_iota(jnp.int32, (T, CK), 0)
    memb = mem_ref[...]
    n_chunks = (i * T + T + CK - 1) // CK

    def chunk(c, carry):
        m, l, acc = carry
        k0 = pl.multiple_of(c * CK, CK)
        kc = ks_ref[pl.ds(k0, CK), :].astype(BF16)
        vc = vs_ref[pl.ds(k0, CK), :].astype(BF16)
        sel = jnp.dot(memb, ex_ref[:, pl.ds(k0, CK)], preferred_element_type=F32) > 0.5
        ok = sel & ((k0 + lax.broadcasted_iota(jnp.int32, (T, CK), 1)) <= qpos)
        bias = jnp.where(ok, 0.0, NEG)
        s = lax.dot_general(q3, kc, _NT, preferred_element_type=F32) + jnp.concatenate([bias] * R, axis=0)
        m_new = jnp.maximum(m, jnp.max(s, axis=-1, keepdims=True))
        alpha = jnp.exp(m - m_new)
        p = jnp.exp(s - m_new)
        l = alpha * l + jnp.sum(p, axis=-1, keepdims=True)
        acc = alpha * acc + jnp.dot(p.astype(BF16), vc, preferred_element_type=F32)
        return m_new, l, acc

    m0 = jnp.full((R * T, 1), NEG, F32)
    _, l, acc = lax.fori_loop(0, n_chunks, chunk, (m0, jnp.zeros((R * T, 1), F32), jnp.zeros((R * T, DH), F32)))
    o_s = acc / l

    kw = jnp.concatenate([kw0_ref[...], kw1_ref[...], kw2_ref[...]], axis=0).astype(BF16)
    vw = jnp.concatenate([vw0_ref[...], vw1_ref[...], vw2_ref[...]], axis=0).astype(BF16)
    qp = i * T + lax.broadcasted_iota(jnp.int32, (T, 3 * T), 0)
    kp = (i - 2) * T + lax.broadcasted_iota(jnp.int32, (T, 3 * T), 1)
    biasw = jnp.where((kp <= qp) & (kp > qp - WINDOW) & (kp >= 0), 0.0, NEG)
    sw = lax.dot_general(q3, kw, _NT, preferred_element_type=F32) + jnp.concatenate([biasw] * R, axis=0)
    ew = jnp.exp(sw - jnp.max(sw, axis=-1, keepdims=True))
    pw = ew / jnp.sum(ew, axis=-1, keepdims=True)
    o_w = jnp.dot(pw.astype(BF16), vw, preferred_element_type=F32)

    gates = jax.nn.sigmoid(gt_ref[...])
    for r in range(R):
        gsel = _gate_cols(gates, g, r * 3)
        out = (gsel[0] * oc_ref[:, r * DH:(r + 1) * DH].astype(F32)
               + gsel[1] * o_s[r * T:(r + 1) * T] + gsel[2] * o_w[r * T:(r + 1) * T])
        mix_ref[:, r * DH:(r + 1) * DH] = out.astype(mix_ref.dtype)


def _gate_cols(gates, g, c0):
    lane = lax.broadcasted_iota(jnp.int32, gates.shape, 1)
    tgt = g * (B_GROUP * 3) + c0
    return [jnp.sum(jnp.where(lane == tgt + k, gates, 0.0), axis=-1, keepdims=True) for k in range(3)]


def sel_win_combine(projb, kvp, oc, mem, n_batch, *, T=256, CK=512):
    M = projb.shape[0]
    L = M // n_batch
    nt = L // T
    GW = B_GROUP * HEAD_DIM
    kcol = lambda base: (lambda b, g, i: (b, base + g))
    wspec = lambda base, d: pl.BlockSpec((T, HEAD_DIM), lambda b, g, i: (b * nt + jnp.maximum(i - d, 0), base + g))
    expand = (jnp.arange(L, dtype=jnp.int32)[None, :] // SLC_BLOCK
              == jnp.arange(128, dtype=jnp.int32)[:, None]).astype(BF16)
    return pl.pallas_call(
        functools.partial(_sel_win_kernel, T=T, CK=CK),
        grid=(n_batch, B_KV_GROUPS, nt),
        in_specs=[pl.BlockSpec((T, GW), lambda b, g, i: (b * nt + i, g)),
                  pl.BlockSpec((L, HEAD_DIM), kcol(8)),
                  pl.BlockSpec((L, HEAD_DIM), kcol(12)),
                  wspec(16, 2), wspec(16, 1), wspec(16, 0),
                  wspec(20, 2), wspec(20, 1), wspec(20, 0),
                  pl.BlockSpec((T, GW), lambda b, g, i: (b * nt + i, g)),
                  pl.BlockSpec((None, None, T, 128), lambda b, g, i: (b, g, i, 0)),
                  pl.BlockSpec((T, 128), lambda b, g, i: (b * nt + i, 16)),
                  pl.BlockSpec((128, L), lambda b, g, i: (0, 0))],
        out_specs=pl.BlockSpec((T, GW), lambda b, g, i: (b * nt + i, g)),
        out_shape=jax.ShapeDtypeStruct((M, TOK_WIDTH), BF16),
        compiler_params=_cparams("parallel", "parallel", "arbitrary"),
        name="sel_win_combine",
    )(projb, kvp, kvp, kvp, kvp, kvp, kvp, kvp, kvp, oc, mem, projb, expand)


def _mem_attn_kernel(q_ref, k_ref, v_ref, o_ref):
    DH = HEAD_DIM
    for h in range(MEM_HEADS):
        qh = (q_ref[:, h * DH:(h + 1) * DH] * SCALE).astype(BF16)
        kh = k_ref[:, h * DH:(h + 1) * DH].astype(BF16)
        vh = v_ref[:, h * DH:(h + 1) * DH].astype(BF16)
        s = lax.dot_general(qh, kh, _NT, preferred_element_type=F32)
        e = jnp.exp(s - jnp.max(s, axis=-1, keepdims=True))
        p = e / jnp.sum(e, axis=-1, keepdims=True)
        o_ref[:, h * DH:(h + 1) * DH] = jnp.dot(p.astype(BF16), vh, preferred_element_type=F32).astype(o_ref.dtype)


def mem_attn_prompt(proj, q_block, memkv, n_batch, *, T=512):
    M = proj.shape[0]
    nt = M // n_batch // T
    return pl.pallas_call(
        _mem_attn_kernel,
        grid=(n_batch, nt),
        in_specs=[pl.BlockSpec((T, MEM_WIDTH), lambda b, i: (b * nt + i, q_block)),
                  pl.BlockSpec((MEM_TOKENS, MEM_WIDTH), lambda b, i: (b, 0)),
                  pl.BlockSpec((MEM_TOKENS, MEM_WIDTH), lambda b, i: (b, 1))],
        out_specs=pl.BlockSpec((T, MEM_WIDTH), lambda b, i: (b * nt + i, 0)),
        out_shape=jax.ShapeDtypeStruct((M, MEM_WIDTH), BF16),
        compiler_params=_cparams("parallel", "parallel"),
        name="mem_attn_prompt",
    )(proj, memkv, memkv)


def _log_sigmoid(x):
    return jnp.minimum(x, 0.0) - jnp.log1p(jnp.exp(-jnp.abs(x)))


def _mlstm_kernel(q_ref, k_ref, v_ref, o_ref, gt_ref, gb_ref, hg_ref, mix_ref, c_ref, n_ref, m_ref, *, nb):
    @pl.when(pl.program_id(0) == 0)
    def _():
        c_ref[...] = jnp.zeros_like(c_ref)
        n_ref[...] = jnp.zeros_like(n_ref)
        m_ref[...] = jnp.zeros_like(m_ref)

    for b in range(nb):
        _mlstm_chunk(q_ref.at[b], k_ref.at[b], v_ref.at[b], o_ref.at[b], gt_ref.at[b], gb_ref, hg_ref,
                     mix_ref.at[b], c_ref.at[b], n_ref.at[b], m_ref.at[b])


def _mlstm_chunk(q_ref, k_ref, v_ref, o_ref, gt_ref, gb_ref, hg_ref, mix_ref, c_ref, n_ref, m_ref):
    H, DH, CL = A_HEADS, HEAD_DIM, MLSTM_CHUNK
    gt = (gt_ref[...] + gb_ref[...]).T
    ig = gt[0:16]
    lf = _log_sigmoid(gt[16:32])
    s_r = lax.broadcasted_iota(jnp.int32, (CL, CL), 0)
    s_c = lax.broadcasted_iota(jnp.int32, (CL, CL), 1)
    upper = jnp.where(s_r <= s_c, 1.0, 0.0).astype(BF16)
    lf_hi = lf.astype(BF16)
    r1 = lf - lf_hi.astype(F32)
    lf_mid = r1.astype(BF16)
    lf_lo = (r1 - lf_mid.astype(F32)).astype(BF16)
    b_rows = (jnp.dot(lf_hi, upper, preferred_element_type=F32) + jnp.dot(lf_mid, upper, preferred_element_type=F32)
              + jnp.dot(lf_lo, upper, preferred_element_type=F32))
    u_rows = ig - b_rows
    bl_all = jnp.sum(lf, axis=-1, keepdims=True)
    tril = s_c <= s_r
    ones_b = jnp.ones((CL, DH), BF16)
    for h in range(H):
        b_row = b_rows[h:h + 1]
        u_row = u_rows[h:h + 1]
        bl = bl_all[h:h + 1]
        m_h = m_ref[h:h + 1, :]
        b_col = jnp.broadcast_to(b_row, (CL, CL)).T
        dlog = jnp.where(tril, b_col + u_row, -jnp.inf)
        inter = b_col[:, 0:1] + m_h[:, 0:1]
        mt = jnp.maximum(inter, jnp.max(dlog, axis=-1, keepdims=True))
        a = jnp.exp(inter - mt)
        qh = q_ref[:, h * DH:(h + 1) * DH]
        kh = k_ref[:, h * DH:(h + 1) * DH] * SCALE
        qb, kb = qh.astype(BF16), kh.astype(BF16)
        v1 = jnp.concatenate([v_ref[:, h * DH:(h + 1) * DH].astype(BF16), ones_b], axis=1)
        w = jnp.exp(dlog - mt) * lax.dot_general(qb, kb, _NT, preferred_element_type=F32)
        c_old = c_ref[h]
        n_old = n_ref[h]
        qcn = jnp.dot(qb, jnp.concatenate([c_old.astype(BF16), n_old.astype(BF16)], axis=1), preferred_element_type=F32)
        wv1 = jnp.dot(w.astype(BF16), v1, preferred_element_type=F32)
        num = a * qcn[:, :DH] + wv1[:, :DH]
        den = a * qcn[:, DH:DH + 1] + wv1[:, DH:DH + 1]
        hval = num / jnp.maximum(jnp.abs(den), jnp.exp(-mt))
        g_row = bl + u_row
        m_new = jnp.maximum(bl + m_h, jnp.max(g_row, axis=-1, keepdims=True))
        decay = jnp.exp(bl + m_h - m_new)
        wk_row = jnp.exp(g_row - m_new)
        k_t = kh.T
        kv1 = jnp.dot((k_t * wk_row).astype(BF16), v1, preferred_element_type=F32)
        c_ref[h] = decay[:, 0:1] * c_old + kv1[:, :DH]
        n_ref[h] = decay * n_old + kv1[:, DH:]
        m_ref[h:h + 1, :] = m_new
        hn = hval * lax.rsqrt(jnp.mean(hval * hval, axis=-1, keepdims=True) + EPS) * hg_ref[:, h * DH:(h + 1) * DH]
        mix_ref[:, h * DH:(h + 1) * DH] = (jax.nn.sigmoid(o_ref[:, h * DH:(h + 1) * DH]) * hn).astype(mix_ref.dtype)


def mlstm_prompt(proja, gate_bias, head_gain, n_batch):
    M, N = proja.shape
    L = M // n_batch
    nc = L // MLSTM_CHUNK
    p3 = proja.reshape(n_batch, L, N)
    blk = lambda j: pl.BlockSpec((n_batch, MLSTM_CHUNK, TOK_WIDTH), lambda c: (0, c, j))
    full = lambda *tail: pl.BlockSpec((n_batch,) + tail, lambda c: (0,) * (1 + len(tail)))
    mix, c_fin, n_fin, m_fin = pl.pallas_call(
        functools.partial(_mlstm_kernel, nb=n_batch),
        grid=(nc,),
        in_specs=[blk(0), blk(1), blk(2), blk(3),
                  pl.BlockSpec((n_batch, MLSTM_CHUNK, 128), lambda c: (0, c, (4 * TOK_WIDTH + MEM_WIDTH) // 128)),
                  pl.BlockSpec((1, 128), lambda c: (0, 0)),
                  pl.BlockSpec((1, TOK_WIDTH), lambda c: (0, 0))],
        out_specs=[pl.BlockSpec((n_batch, MLSTM_CHUNK, TOK_WIDTH), lambda c: (0, c, 0)),
                   full(A_HEADS, HEAD_DIM, HEAD_DIM), full(A_HEADS, HEAD_DIM, 128), full(16, 128)],
        out_shape=[jax.ShapeDtypeStruct((n_batch, L, TOK_WIDTH), BF16),
                   jax.ShapeDtypeStruct((n_batch, A_HEADS, HEAD_DIM, HEAD_DIM), F32),
                   jax.ShapeDtypeStruct((n_batch, A_HEADS, HEAD_DIM, 128), F32),
                   jax.ShapeDtypeStruct((n_batch, 16, 128), F32)],
        compiler_params=_cparams("arbitrary"),
        name="mlstm_prompt",
    )(p3, p3, p3, p3, p3, gate_bias, head_gain.reshape(1, TOK_WIDTH))
    return mix.reshape(M, TOK_WIDTH), c_fin, n_fin[..., 0], m_fin


def _lane_bcast_rows(row):
    return jnp.broadcast_to(row, (128, 128)).T


def _mlstm_sample_kernel(p_ref, gb_ref, m0_ref, hg_ref, c0_ref, n0_ref, mix_ref, c_ref, n_ref, m_ref, *, nb):
    for b in range(nb):
        _mlstm_sample_one(p_ref.at[b], gb_ref, m0_ref.at[b], hg_ref, c0_ref.at[b], n0_ref.at[b],
                          mix_ref.at[b], c_ref.at[b], n_ref.at[b], m_ref.at[b])


def _mlstm_sample_one(p_ref, gb_ref, m0_ref, hg_ref, c0_ref, n0_ref, mix_ref, c_ref, n_ref, m_ref):
    H, DH, W = A_HEADS, HEAD_DIM, TOK_WIDTH
    gcol = 4 * W + MEM_WIDTH
    gates = _lane_bcast_rows(p_ref[:, gcol:gcol + 128] + gb_ref[...])
    m_all = _lane_bcast_rows(m0_ref[...])
    m_ref[...] = jnp.zeros_like(m_ref)
    diag = lax.broadcasted_iota(jnp.int32, (DH, DH), 0) == lax.broadcasted_iota(jnp.int32, (DH, DH), 1)
    for h in range(H):
        ig = gates[h:h + 1]
        lf = _log_sigmoid(gates[16 + h:17 + h])
        m_old = m_all[h:h + 1]
        m_new = jnp.maximum(lf + m_old, ig)
        a = jnp.exp(lf + m_old - m_new)
        wgt = jnp.exp(ig - m_new)
        q = p_ref[:, h * DH:(h + 1) * DH]
        k = p_ref[:, W + h * DH:W + (h + 1) * DH] * SCALE
        v = p_ref[:, 2 * W + h * DH:2 * W + (h + 1) * DH]
        o = p_ref[:, 3 * W + h * DH:3 * W + (h + 1) * DH]
        c_old = c0_ref[h]
        n_old = n0_ref[h:h + 1, :]
        q_c = jnp.dot(jnp.broadcast_to(q, (8, DH)).astype(BF16), c_old.astype(BF16), preferred_element_type=F32)[0:1]
        k_diag = jnp.where(diag, jnp.broadcast_to(k, (DH, DH)), 0.0).astype(BF16)
        kv = jnp.dot(k_diag, jnp.broadcast_to(v, (DH, DH)).astype(BF16), preferred_element_type=F32)
        wqk = wgt * jnp.sum(q * k, axis=-1, keepdims=True)
        num = a * q_c + wqk * v
        den = a * jnp.sum(q * n_old, axis=-1, keepdims=True) + wqk
        hval = num / jnp.maximum(jnp.abs(den), jnp.exp(-m_new))
        c_ref[h] = a * c_old + wgt * kv
        n_ref[h:h + 1, :] = a * n_old + wgt * k
        m_ref[h:h + 1, :] = m_new
        hn = hval * lax.rsqrt(jnp.mean(hval * hval, axis=-1, keepdims=True) + EPS) * hg_ref[:, h * DH:(h + 1) * DH]
        mix_ref[:, h * DH:(h + 1) * DH] = (jax.nn.sigmoid(o) * hn).astype(mix_ref.dtype)


def mlstm_sample(proja, gate_bias, m0, head_gain, c0, n0, *, nb=4):
    Bs, N = proja.shape
    m0p = jnp.pad(m0, ((0, 0), (0, 128 - A_HEADS))).reshape(Bs, 1, 128)
    row = lambda n: pl.BlockSpec((nb, 1, n), lambda b: (b, 0, 0))
    const = lambda n: pl.BlockSpec((1, n), lambda b: (0, 0))
    cspec = pl.BlockSpec((nb, A_HEADS, HEAD_DIM, HEAD_DIM), lambda b: (b, 0, 0, 0))
    nspec = pl.BlockSpec((nb, A_HEADS, HEAD_DIM), lambda b: (b, 0, 0))
    return pl.pallas_call(
        functools.partial(_mlstm_sample_kernel, nb=nb),
        grid=(Bs // nb,),
        in_specs=[row(N), const(128), row(128), const(TOK_WIDTH), cspec, nspec],
        out_specs=[row(TOK_WIDTH), cspec, nspec, pl.BlockSpec((nb, 16, 128), lambda b: (b, 0, 0))],
        out_shape=[jax.ShapeDtypeStruct((Bs, 1, TOK_WIDTH), BF16),
                   jax.ShapeDtypeStruct(c0.shape, F32), jax.ShapeDtypeStruct(n0.shape, F32),
                   jax.ShapeDtypeStruct((Bs, 16, 128), F32)],
        compiler_params=_cparams("parallel"),
        name="mlstm_sample",
    )(proja.reshape(Bs, 1, N), gate_bias, m0p, head_gain.reshape(1, TOK_WIDTH), c0, n0)


def _mem_attn_sample_kernel(q_ref, k_ref, v_ref, o_ref, *, nb):
    H, DH = MEM_HEADS, HEAD_DIM
    nrow = k_ref.shape[1]
    row_h = lax.broadcasted_iota(jnp.int32, (8, nrow), 0)
    col_h = lax.broadcasted_iota(jnp.int32, (8, nrow), 1) % H
    bias = jnp.where(row_h == col_h, 0.0, NEG)
    for b in range(nb):
        q8 = jnp.concatenate([q_ref[b:b + 1, h * DH:(h + 1) * DH] for h in range(H)] + [jnp.zeros((8 - H, DH), F32)], axis=0)
        s = lax.dot_general((q8 * SCALE).astype(BF16), k_ref[b].astype(BF16), _NT, preferred_element_type=F32) + bias
        e = jnp.exp(s - jnp.max(s, axis=-1, keepdims=True))
        p = e / jnp.sum(e, axis=-1, keepdims=True)
        o = jnp.dot(p.astype(BF16), v_ref[b].astype(BF16), preferred_element_type=F32)
        o_ref[b * H:(b + 1) * H, :] = o[0:H]


def mem_attn_sample(proj, q_block, cache_k, cache_v, layer, *, nb=8):
    Bs, N = proj.shape
    nrow = MEM_TOKENS * MEM_HEADS
    k2 = cache_k.reshape(cache_k.shape[0], Bs, nrow, HEAD_DIM)
    v2 = cache_v.reshape(cache_v.shape[0], Bs, nrow, HEAD_DIM)
    cspec = pl.BlockSpec((None, nb, nrow, HEAD_DIM), lambda i: (layer, i, 0, 0))
    return pl.pallas_call(
        functools.partial(_mem_attn_sample_kernel, nb=nb),
        grid=(Bs // nb,),
        in_specs=[pl.BlockSpec((None, nb, MEM_WIDTH), lambda i: (i, 0, q_block)), cspec, cspec],
        out_specs=pl.BlockSpec((nb * MEM_HEADS, HEAD_DIM), lambda i: (i, 0)),
        out_shape=jax.ShapeDtypeStruct((Bs * MEM_HEADS, HEAD_DIM), F32),
        compiler_params=_cparams("parallel"),
        name="mem_attn_sample",
    )(proj.reshape(Bs // nb, nb, N), k2, v2).reshape(Bs, MEM_WIDTH)


def compress_sample(cache, page_table, pe, w1, w2):
    Bs, n_pages = page_table.shape
    G, DH = B_KV_GROUPS, HEAD_DIM
    prow = PAGE_SIZE * G
    nrow = n_pages * (PAGE_SIZE // CMP_STRIDE) * G
    pe_rows, w1cat, w2b = _compress_weights(pe, w1, w2)
    pe_flat = jnp.transpose(pe_rows, (1, 0, 2)).reshape(8, CMP_STRIDE * DH)
    pages = cache.reshape(cache.shape[0], prow, DH)
    page_specs = [pl.BlockSpec((None, prow, DH), lambda b, pt, p=p: (pt[b, p], 0, 0)) for p in range(n_pages)]

    def body(pt_ref, *refs):
        _compress_paged_kernel(*refs, n_pages=n_pages)

    return pl.pallas_call(
        body,
        grid_spec=pltpu.PrefetchScalarGridSpec(
            num_scalar_prefetch=1, grid=(Bs,),
            in_specs=page_specs + [pl.BlockSpec((8, CMP_STRIDE * DH), lambda b, pt: (0, 0)),
                                   pl.BlockSpec((CMP_STRIDE * DH, 2 * DH), lambda b, pt: (0, 0)),
                                   pl.BlockSpec((DH, DH), lambda b, pt: (0, 0))],
            out_specs=pl.BlockSpec((None, nrow, DH), lambda b, pt: (b, 0, 0))),
        out_shape=jax.ShapeDtypeStruct((Bs, nrow, DH), F32),
        compiler_params=_cparams("parallel"),
        name="compress_sample",
    )(page_table, *([pages] * n_pages), pe_flat, w1cat, w2b)


def _compress_paged_kernel(*refs, n_pages):
    page_refs = refs[:n_pages]
    pe_ref, w1_ref, w2_ref, o_ref = refs[n_pages:]
    G, DH = B_KV_GROUPS, HEAD_DIM
    cpp = PAGE_SIZE // CMP_STRIDE
    rows = n_pages * cpp * G
    low = lax.broadcasted_iota(jnp.int32, (cpp // 2, 8, DH), 1) < G
    pieces = [[] for _ in range(CMP_STRIDE)]
    for p in range(n_pages):
        x5 = page_refs[p][...].reshape(cpp // 2, 2, CMP_STRIDE // 2, 8, DH)
        for k in range(CMP_STRIDE // 2):
            a, b = x5[:, 0, k], x5[:, 1, k]
            even = jnp.where(low, a, pltpu.roll(b, G, 1))
            odd = jnp.where(low, pltpu.roll(a, G, 1), b)
            pieces[2 * k].append(even.reshape(cpp * G, DH).astype(BF16))
            pieces[2 * k + 1].append(odd.reshape(cpp * G, DH).astype(BF16))
    lhs = jnp.concatenate([jnp.concatenate(pc, axis=0) for pc in pieces], axis=1)
    lhs = jnp.concatenate([lhs, pe_ref[...].astype(BF16)], axis=0)
    acc = jnp.dot(lhs, w1_ref[...], preferred_element_type=F32)
    first = acc[:rows, :DH] + acc[rows:rows + 1, :DH]
    second = acc[:rows, DH:] + acc[rows + 1:rows + 2, DH:]
    hid = jax.nn.gelu(first + pltpu.roll(second, rows - G, 0))
    o_ref[...] = jnp.dot(hid.astype(BF16), w2_ref[...], preferred_element_type=F32)


def _nsa_sample_kernel(pt_ref, *refs, n_pages):
    p_ref, kv_ref, kc_ref, vc_ref, ex_ref = refs[:5]
    sk_refs = refs[5:5 + n_pages]
    sv_refs = refs[5 + n_pages:5 + 2 * n_pages]
    wk_ref, wv_ref, mix_ref = refs[5 + 2 * n_pages:]
    G, R, DH, W = B_KV_GROUPS, B_GROUP, HEAD_DIM, TOK_WIDTH
    pos = PAST_LEN
    crow = kc_ref.shape[0]
    ncol = n_pages * sk_refs[0].shape[0]
    nsl = -(-(pos + 1) // SLC_BLOCK)
    wrow = wk_ref.shape[0]
    w_buf = wrow // G

    j_l = lax.broadcasted_iota(jnp.int32, (8, 128), 1)
    cur = pos // SLC_BLOCK
    forced = (j_l == 0) | (j_l == cur) | (j_l == cur - 1)
    jr = lax.broadcasted_iota(jnp.int32, (128, 128), 0)
    jc = lax.broadcasted_iota(jnp.int32, (128, 128), 1)
    wcol = lax.broadcasted_iota(jnp.int32, (8, wrow), 1)
    wvalid = (pos - w_buf + wcol // G) > pos - WINDOW
    gates = jax.nn.sigmoid(p_ref[:, W + MEM_WIDTH:W + MEM_WIDTH + 128])

    q32r = jnp.concatenate([p_ref[:, (g * R + r) * DH:(g * R + r + 1) * DH] if r < R else jnp.zeros((1, DH), F32)
                            for g in range(G) for r in range(8)], axis=0)
    q32 = q32r * SCALE
    q32b = q32.astype(BF16)

    c_row = lax.broadcasted_iota(jnp.int32, (8 * G, crow), 0) // 8
    c_col = lax.broadcasted_iota(jnp.int32, (8 * G, crow), 1)
    cvalid = (c_col % G == c_row) & ((c_col // G) * CMP_STRIDE + CMP_LEN - 1 <= pos)
    s = lax.dot_general(q32r.astype(BF16), kc_ref[...].astype(BF16), _NT, preferred_element_type=F32) * SCALE
    sm = jnp.where(cvalid, s, NEG)
    e = jnp.exp(sm - jnp.max(sm, axis=-1, keepdims=True))
    p = jnp.where(cvalid, e / jnp.sum(e, axis=-1, keepdims=True), 0.0)
    o_c = jnp.dot(p.astype(BF16), vc_ref[...].astype(BF16), preferred_element_type=F32)
    psum = jnp.concatenate([p[8 * g:8 * g + 1] + p[8 * g + 1:8 * g + 2] + p[8 * g + 2:8 * g + 3] for g in range(G)]
                           + [jnp.zeros((8 - G, crow), F32)], axis=0)
    n_o = (lax.broadcasted_iota(jnp.int32, (crow, 128), 0) // G) * CMP_STRIDE
    j_o = lax.broadcasted_iota(jnp.int32, (crow, 128), 1)
    ovl = jnp.where((n_o < (j_o + 1) * SLC_BLOCK) & (n_o + CMP_LEN > j_o * SLC_BLOCK), 1.0, 0.0).astype(BF16)
    p_hi, p_lo = _split2(psum)
    imp = jnp.dot(p_hi, ovl, preferred_element_type=F32) + jnp.dot(p_lo, ovl, preferred_element_type=F32)
    score_all = jnp.where(forced, FORCED_SCORE, jnp.where(j_l * SLC_BLOCK <= pos, imp, -1.0))
    score_all = jnp.where(j_l < nsl, score_all, -2.0)

    sb_rows, wb_rows = [], []
    for g in range(G):
        sc_c = jnp.broadcast_to(score_all[g:g + 1], (128, 128))
        sc_r = sc_c.T
        beats = (sc_r > sc_c) | ((sc_r == sc_c) & (jr < jc))
        rank = jnp.sum(jnp.where(beats, 1.0, 0.0), axis=0, keepdims=True)
        member = jnp.where((rank < min(N_SELECT, nsl)) & (j_l[0:1] < nsl), 1.0, 0.0)
        sb_rows.append(jnp.broadcast_to(member, (8, 128)))
        wb_rows.append(jnp.where(wvalid & (wcol % G == g), 0.0, NEG))

    mexp = jnp.dot(jnp.concatenate(sb_rows, axis=0).astype(BF16), ex_ref[...], preferred_element_type=F32)
    row_g = lax.broadcasted_iota(jnp.int32, (8 * G, ncol), 0) // 8
    col_g = lax.broadcasted_iota(jnp.int32, (8 * G, ncol), 1) % G
    sbias = jnp.where((mexp > 0.5) & (col_g == row_g), 0.0, NEG)

    def new_rows(base):
        return jnp.concatenate([jnp.broadcast_to(kv_ref[:, base + g * DH:base + (g + 1) * DH], (8, DH)) for g in range(G)], axis=0)

    def branch(k_rows, v_rows, bias, k_base, v_base):
        s = lax.dot_general(q32b, k_rows, _NT, preferred_element_type=F32) + bias
        s_new = jnp.sum(q32 * new_rows(k_base), axis=-1, keepdims=True)
        mx = jnp.maximum(jnp.max(s, axis=-1, keepdims=True), s_new)
        p = jnp.exp(s - mx)
        p_new = jnp.exp(s_new - mx)
        num = jnp.dot(p.astype(BF16), v_rows, preferred_element_type=F32) + p_new * new_rows(v_base)
        return num / (jnp.sum(p, axis=-1, keepdims=True) + p_new)

    ks = jnp.concatenate([r_[...].astype(BF16) for r_ in sk_refs], axis=0)
    vs = jnp.concatenate([r_[...].astype(BF16) for r_ in sv_refs], axis=0)
    o_s = branch(ks, vs, sbias, 2 * G * DH, 3 * G * DH)
    o_w = branch(wk_ref[...].astype(BF16), wv_ref[...].astype(BF16), jnp.concatenate(wb_rows, axis=0), 4 * G * DH, 5 * G * DH)
    for g in range(G):
        for r in range(R):
            h, row = g * R + r, g * 8 + r
            gsel = _gate_cols(gates, g, r * 3)
            out = gsel[0] * o_c[row:row + 1] + gsel[1] * o_s[row:row + 1] + gsel[2] * o_w[row:row + 1]
            mix_ref[:, h * DH:(h + 1) * DH] = out.astype(mix_ref.dtype)


def nsa_sample(projb, kvp, kc, vc, cache_slc_k, cache_slc_v, cache_win_k, cache_win_v, page_table):
    Bs, n_pages = page_table.shape
    G = B_KV_GROUPS
    prow = PAGE_SIZE * G
    wrow = cache_win_k.shape[1] * G
    sk = cache_slc_k.reshape(cache_slc_k.shape[0], prow, HEAD_DIM)
    sv = cache_slc_v.reshape(cache_slc_v.shape[0], prow, HEAD_DIM)
    wk = cache_win_k.reshape(Bs, wrow, HEAD_DIM)
    wv = cache_win_v.reshape(Bs, wrow, HEAD_DIM)
    ncol = n_pages * prow
    expand = (jnp.arange(ncol, dtype=jnp.int32)[None, :] // (G * SLC_BLOCK)
              == jnp.arange(128, dtype=jnp.int32)[:, None]).astype(BF16)
    row = lambda n: pl.BlockSpec((None, 1, n), lambda b, pt: (b, 0, 0))
    cspec = pl.BlockSpec((None, kc.shape[1], HEAD_DIM), lambda b, pt: (b, 0, 0))
    page_specs = [pl.BlockSpec((None, prow, HEAD_DIM), lambda b, pt, p=p: (pt[b, p], 0, 0)) for p in range(n_pages)]
    wspec = pl.BlockSpec((None, wrow, HEAD_DIM), lambda b, pt: (b, 0, 0))
    return pl.pallas_call(
        functools.partial(_nsa_sample_kernel, n_pages=n_pages),
        grid_spec=pltpu.PrefetchScalarGridSpec(
            num_scalar_prefetch=1, grid=(Bs,),
            in_specs=[row(projb.shape[1]), row(kvp.shape[1]), cspec, cspec,
                      pl.BlockSpec((128, ncol), lambda b, pt: (0, 0))] + page_specs + page_specs + [wspec, wspec],
            out_specs=row(TOK_WIDTH)),
        out_shape=jax.ShapeDtypeStruct((Bs, 1, TOK_WIDTH), BF16),
        compiler_params=_cparams("parallel"),
        name="nsa_sample",
    )(page_table, projb.reshape(Bs, 1, -1), kvp.reshape(Bs, 1, -1), kc, vc, expand, *([sk] * n_pages),
      *([sv] * n_pages), wk, wv).reshape(Bs, TOK_WIDTH)


def kernel(x_prompt, x_sample, mem_prompt, cache_mem_k, cache_mem_v, state_mlstm_C, state_mlstm_n, state_mlstm_m, cache_cmp_k, cache_cmp_v, cache_slc_k, cache_slc_v, cache_win_k, cache_win_v, page_table, ffn1_norm, ffn1_w_in, ffn1_w_out, ffn2_norm, ffn2_w_in, ffn2_w_out, mix_norm, a_w_in, a_b_i, a_b_f, a_head_norm, b_w_in, mem_norm, w_mem_kv, w_out, kv_norm, w_kv, cmp_pe_k, cmp_w1_k, cmp_w2_k, cmp_pe_v, cmp_w1_v, cmp_w2_v, final_norm):
    G, R, DH = B_KV_GROUPS, B_GROUP, HEAD_DIM
    ffn1_wi, ffn1_wo = ffn1_w_in.astype(BF16), ffn1_w_out.astype(BF16)
    ffn2_wi, ffn2_wo = ffn2_w_in.astype(BF16), ffn2_w_out.astype(BF16)
    w_o = w_out.astype(BF16)
    w_kv_b = w_kv.astype(BF16)
    w_mkv = w_mem_kv.astype(BF16)

    wa = a_w_in[0]
    zc = lambda n: jnp.zeros((D_MODEL, n), F32)
    a_wr = jnp.concatenate([wa[:, :4 * TOK_WIDTH], wa[:, 4 * TOK_WIDTH + 2 * A_HEADS:],
                            wa[:, 4 * TOK_WIDTH:4 * TOK_WIDTH + A_HEADS], zc(16 - A_HEADS),
                            wa[:, 4 * TOK_WIDTH + A_HEADS:4 * TOK_WIDTH + 2 * A_HEADS], zc(128 - 16 - A_HEADS),
                            zc(128)], axis=1).astype(BF16)
    gate_bias = jnp.concatenate([a_b_i[0], jnp.zeros((16 - A_HEADS,), F32), a_b_f[0],
                                 jnp.zeros((128 - 16 - A_HEADS,), F32)]).reshape(1, 128)
    wb = b_w_in[0]
    b_wr = jnp.concatenate([wb[:, :TOK_WIDTH], wb[:, TOK_WIDTH + 3 * B_HEADS:],
                            wb[:, TOK_WIDTH:TOK_WIDTH + 3 * B_HEADS], zc(256 - 3 * B_HEADS)], axis=1).astype(BF16)

    Bp, S = x_prompt.shape[:2]
    Mp = Bp * S
    memx = mem_prompt.reshape(Bp * MEM_TOKENS, D_MODEL)
    memkv = [norm_matmul(memx, mem_norm[l], w_mkv[l], tm=512, tn=512) for l in range(DEPTH)]
    mem5 = jnp.stack(memkv).reshape(DEPTH, Bp, MEM_TOKENS, 2, MEM_HEADS, DH)
    mem_k_p, mem_v_p = mem5[:, :, :, 0], mem5[:, :, :, 1]

    x = x_prompt.reshape(Mp, D_MODEL)
    x = ffn(x, ffn1_norm[0], ffn1_wi, ffn1_wo, 0, tm=512, tf=512)
    proja = norm_matmul(x, mix_norm[0], a_wr, tm=1024, tn=768)
    mix, C_fin, n_fin, m_fin = mlstm_prompt(proja, gate_bias, a_head_norm[0], Bp)
    mo = mem_attn_prompt(proja, 4 * TOK_WIDTH // MEM_WIDTH, memkv[0], Bp)
    x = outproj(x, mix, mo, w_o, 0, tm=512)
    x = ffn(x, ffn2_norm[0], ffn2_wi, ffn2_wo, 0, tm=512, tf=512)
    kvp, kv4, kvb = kv_proj(x, kv_norm, w_kv_b, tm=1024)
    x = ffn(x, ffn1_norm[1], ffn1_wi, ffn1_wo, 1, tm=512, tf=512)
    projb = norm_matmul(x, mix_norm[1], b_wr, tm=1024, tn=768)
    kc = compress_prompt(kvp, 0, Bp, cmp_pe_k, cmp_w1_k, cmp_w2_k)
    vc = compress_prompt(kvp, 1, Bp, cmp_pe_v, cmp_w1_v, cmp_w2_v)
    oc, member = cmp_select_prompt(projb, kc, vc, Bp)
    mix = sel_win_combine(projb, kvb, oc, member, Bp)
    mo = mem_attn_prompt(projb, TOK_WIDTH // MEM_WIDTH, memkv[1], Bp)
    x = outproj(x, mix, mo, w_o, 1, tm=512)
    y_prompt = ffn(x, ffn2_norm[1], ffn2_wi, ffn2_wo, 1, final_norm, tm=512, tf=512).reshape(Bp, S, D_MODEL)
    C_p = C_fin[None]
    n_p = n_fin[None]
    m_p = m_fin[None, :, :A_HEADS, 0]
    kv_p = tuple(kv4[i].reshape(Bp, S, G, DH) for i in range(6))

    Bs = x_sample.shape[0]
    xs = x_sample.reshape(Bs, D_MODEL)
    xs = ffn(xs, ffn1_norm[0], ffn1_wi, ffn1_wo, 0, tm=Bs, tf=512)
    proja_s = norm_matmul(xs, mix_norm[0], a_wr, tm=Bs, tn=768)
    mix_s, C_new, n_new, m_new = mlstm_sample(proja_s, gate_bias, state_mlstm_m[0], a_head_norm[0],
                                              state_mlstm_C[0], state_mlstm_n[0])
    mo_s = mem_attn_sample(proja_s, 4 * TOK_WIDTH // MEM_WIDTH, cache_mem_k, cache_mem_v, 0)
    xs = outproj(xs, mix_s.reshape(Bs, TOK_WIDTH), mo_s, w_o, 0, tm=Bs)
    xs = ffn(xs, ffn2_norm[0], ffn2_wi, ffn2_wo, 0, tm=Bs, tf=512)
    kvp_s = norm_matmul(xs, kv_norm, w_kv_b, tm=Bs, tn=512)
    xs = ffn(xs, ffn1_norm[1], ffn1_wi, ffn1_wo, 1, tm=Bs, tf=512)
    projb_s = norm_matmul(xs, mix_norm[1], b_wr, tm=Bs, tn=768)
    kc_s = compress_sample(cache_cmp_k, page_table, cmp_pe_k, cmp_w1_k, cmp_w2_k)
    vc_s = compress_sample(cache_cmp_v, page_table, cmp_pe_v, cmp_w1_v, cmp_w2_v)
    mix_s = nsa_sample(projb_s, kvp_s, kc_s, vc_s, cache_slc_k, cache_slc_v, cache_win_k, cache_win_v, page_table)
    mo_s = mem_attn_sample(projb_s, TOK_WIDTH // MEM_WIDTH, cache_mem_k, cache_mem_v, 1)
    xs = outproj(xs, mix_s, mo_s, w_o, 1, tm=Bs)
    y_sample = ffn(xs, ffn2_norm[1], ffn2_wi, ffn2_wo, 1, final_norm, tm=Bs, tf=512).reshape(Bs, 1, D_MODEL)
    C_s = C_new[None]
    n_s = n_new[None]
    m_s = m_new[None, :, :A_HEADS, 0]
    kvs5 = kvp_s.reshape(Bs, 1, 6, G, DH)
    kv_s = tuple(kvs5[:, :, i] for i in range(6))

    w_p = min(WINDOW, S)
    cmp_k_p, cmp_v_p, slc_k_p, slc_v_p, win_k_all, win_v_all = kv_p
    win_k_p = win_k_all[:, S - w_p:]
    win_v_p = win_v_all[:, S - w_p:]
    cmp_k_s, cmp_v_s, slc_k_s, slc_v_s, win_k_s, win_v_s = kv_s
    return (y_prompt, y_sample, mem_k_p, mem_v_p, C_p, n_p, m_p, C_s, n_s, m_s,
            cmp_k_p, cmp_v_p, slc_k_p, slc_v_p, win_k_p, win_v_p,
            cmp_k_s, cmp_v_s, slc_k_s, slc_v_s, win_k_s, win_v_s)
```
